```python
import math
import jax, jax.numpy as jnp
from jax import lax
import numpy as np


D_MODEL = 1024
BATCH = 4
SEQ = 4096
DEPTH = 4
DEC_BATCH = 128
DEC_SEQ = 1
PAST_LEN = 2048
PAGE_SIZE = 128

N_A_LAYERS = DEPTH // 2
N_B_LAYERS = DEPTH - N_A_LAYERS
HEAD_DIM = 64
MIX_W = 3 * D_MODEL // 4
MEM_W = D_MODEL // 4
MEM_HEADS = MEM_W // HEAD_DIM
MEM_TOKENS = 256
SSM_GROUP = 16
SSM_GROUPS = MIX_W // SSM_GROUP
SSM_STATE = 64
DT_MIN = 0.001
DT_MAX = 0.1
MOBA_HEADS = MIX_W // HEAD_DIM
MOBA_KV_HEADS = 4
KV_W = MOBA_KV_HEADS * HEAD_DIM
MOBA_BLOCK = 256
MOBA_TOPK = 3
Q_BLOCK = 64
D_FF = ((-(-8 * D_MODEL // 3) + 255) // 256) * 256
EPS = 1e-6

kernel_name = 'yoco_s5_moba_memory_decoder_step'


def _rmsnorm(x, g):
    x32 = x.astype(jnp.float32)
    y = x32 * lax.rsqrt(jnp.mean(x32 * x32, axis=-1, keepdims=True) + EPS)
    return (y * g.astype(jnp.float32)).astype(x.dtype)


def _swiglu(a, w_gu, w_down):
    g, u = jnp.split(a @ w_gu, 2, axis=-1)
    return (jax.nn.silu(g) * u) @ w_down


def _complex_affine_combine(e1, e2):
    a1r, a1i, b1r, b1i = e1
    a2r, a2i, b2r, b2i = e2
    return (a2r * a1r - a2i * a1i, a2r * a1i + a2i * a1r,
            a2r * b1r - a2i * b1i + b2r, a2r * b1i + a2i * b1r + b2i)


def _s5_mixer(u, h0_re, h0_im, a_re, a_im, log_dt, b_re, b_im, c_re, c_im, d_skip, w_glu):
    f32 = jnp.float32
    bsz, seq, _ = u.shape
    uf = u.astype(f32).reshape(bsz, seq, SSM_GROUPS, SSM_GROUP)
    dt = jnp.exp(log_dt.astype(f32))[:, None]
    lam_re = jnp.minimum(a_re.astype(f32), -1e-4)
    lam_im = a_im.astype(f32)
    mag = jnp.exp(dt * lam_re)
    ang = dt * lam_im
    ab_re, ab_im = mag * jnp.cos(ang), mag * jnp.sin(ang)
    den = lam_re * lam_re + lam_im * lam_im
    num_re = ab_re - 1.0
    f_re = (num_re * lam_re + ab_im * lam_im) / den
    f_im = (ab_im * lam_re - num_re * lam_im) / den
    b_re32, b_im32 = b_re.astype(f32), b_im.astype(f32)
    bb_re = f_re[..., None] * b_re32 - f_im[..., None] * b_im32
    bb_im = f_re[..., None] * b_im32 + f_im[..., None] * b_re32
    bu_re = jnp.einsum('blgh,gph->lbgp', uf, bb_re)
    bu_im = jnp.einsum('blgh,gph->lbgp', uf, bb_im)
    a_l_re = jnp.broadcast_to(ab_re[None, None], (seq, 1) + ab_re.shape)
    a_l_im = jnp.broadcast_to(ab_im[None, None], (seq, 1) + ab_im.shape)
    cum_re, cum_im, x_re, x_im = lax.associative_scan(
        _complex_affine_combine, (a_l_re, a_l_im, bu_re, bu_im), axis=0)
    h_re = h0_re.astype(f32)[None]
    h_im = h0_im.astype(f32)[None]
    xr = x_re + cum_re * h_re - cum_im * h_im
    xi = x_im + cum_re * h_im + cum_im * h_re
    y = (jnp.einsum('lbgp,ghp->blgh', xr, c_re.astype(f32))
         - jnp.einsum('lbgp,ghp->blgh', xi, c_im.astype(f32))
         + d_skip.astype(f32) * uf)
    y = jax.nn.gelu(y.reshape(bsz, seq, MIX_W))
    y = y * jax.nn.sigmoid(y @ w_glu.astype(f32))
    return y.astype(u.dtype), xr[-1], xi[-1]


def _moba_attend(q, k, v, q_pos0):
    f32 = jnp.float32
    bq, lq, nh, hd = q.shape
    t_len = k.shape[1]
    nb = -(-t_len // MOBA_BLOCK)
    pad = nb * MOBA_BLOCK - t_len
    kp = jnp.pad(k, ((0, 0), (0, pad), (0, 0), (0, 0))).reshape(bq, nb, MOBA_BLOCK, MOBA_KV_HEADS, hd)
    vp = jnp.pad(v, ((0, 0), (0, pad), (0, 0), (0, 0))).reshape(bq, nb, MOBA_BLOCK, MOBA_KV_HEADS, hd)
    kmean = jnp.mean(kp.astype(f32), axis=2)
    kv_of_h = jnp.arange(nh, dtype=jnp.int32) // (nh // MOBA_KV_HEADS)
    qb = math.gcd(lq, Q_BLOCK)
    nq = lq // qb
    k_sel = min(MOBA_TOPK, nb)
    q_items = q.reshape(bq * nq, qb, nh, hd)
    b_items = jnp.repeat(jnp.arange(bq, dtype=jnp.int32), nq)
    p_items = q_pos0 + jnp.tile(jnp.arange(nq, dtype=jnp.int32) * qb, bq)
    scale = HEAD_DIM ** -0.5

    def one(item):
        qi, b, p0 = item
        qi32 = qi.astype(f32)
        pos = p0 + jnp.arange(qb, dtype=jnp.int32)
        blk = pos // MOBA_BLOCK
        gate = jnp.einsum('qhd,jhd->qhj', qi32, kmean[b][:, kv_of_h])
        past = jnp.arange(nb, dtype=jnp.int32)[None, None, :] < blk[:, None, None]
        gate = jnp.where(past, gate, -jnp.inf)
        _, top = lax.top_k(gate, k_sel)
        slot_ok = jnp.broadcast_to(jnp.arange(k_sel, dtype=jnp.int32)[None, None, :] < blk[:, None, None], (qb, nh, k_sel))
        own = jnp.broadcast_to(blk[:, None, None], (qb, nh, 1))
        idx = jnp.concatenate([top.astype(jnp.int32), own], axis=-1)
        ok = jnp.concatenate([slot_ok, jnp.ones((qb, nh, 1), bool)], axis=-1)
        kh = kv_of_h[None, :, None]
        kg = kp[b, idx, :, kh].astype(f32)
        vg = vp[b, idx, :, kh].astype(f32)
        s = jnp.einsum('qhd,qhsnd->qhsn', qi32, kg) * scale
        kpos = idx[..., None] * MOBA_BLOCK + jnp.arange(MOBA_BLOCK, dtype=jnp.int32)
        mask = ok[..., None] & (kpos <= pos[:, None, None, None])
        s = jnp.where(mask, s, -jnp.inf)
        pr = jax.nn.softmax(s.reshape(qb, nh, -1), axis=-1).reshape(s.shape)
        return jnp.einsum('qhsn,qhsnd->qhd', pr, vg).astype(q.dtype)

    out = lax.map(one, (q_items, b_items, p_items))
    return out.reshape(bq, lq, nh, hd)


def _mem_kv(mem, g_mem, w_mem_kv, g_mk):
    m = _rmsnorm(mem, g_mem)
    mk, mv = jnp.split(m @ w_mem_kv, 2, axis=-1)
    bsz, nm, _ = mem.shape
    mk = _rmsnorm(mk.reshape(bsz, nm, MEM_HEADS, HEAD_DIM), g_mk)
    return mk, mv.reshape(bsz, nm, MEM_HEADS, HEAD_DIM)


def _mem_attend(q, mk, mv):
    s = jnp.einsum('blhd,bmhd->bhlm', q.astype(jnp.float32), mk.astype(jnp.float32)) * (HEAD_DIM ** -0.5)
    pr = jax.nn.softmax(s, axis=-1)
    return jnp.einsum('bhlm,bmhd->blhd', pr, mv.astype(jnp.float32)).astype(q.dtype)


def _shared_kv(h, g_kv, w_kv, g_k):
    a = _rmsnorm(h, g_kv)
    k, v = jnp.split(a @ w_kv, 2, axis=-1)
    bsz, seq, _ = h.shape
    k = _rmsnorm(k.reshape(bsz, seq, MOBA_KV_HEADS, HEAD_DIM), g_k)
    return k, v.reshape(bsz, seq, MOBA_KV_HEADS, HEAD_DIM)


def _run_trunk(x, pos0, h0_re, h0_im, mem_k, mem_v, k_past, v_past, p):
    bsz, seq, _ = x.shape
    h = x
    fin_re, fin_im = [], []
    k_new = v_new = k_all = v_all = None
    for l in range(DEPTH):
        a = _rmsnorm(h, p['g_mix'][l])
        proj = a @ p['w_in'][l]
        mix_in, mq = proj[..., :MIX_W], proj[..., MIX_W:]
        if l < N_A_LAYERS:
            mix_out, fr, fi = _s5_mixer(mix_in, h0_re[l], h0_im[l], p['ssm_a_re'][l], p['ssm_a_im'][l],
                                        p['ssm_log_dt'][l], p['ssm_b_re'][l], p['ssm_b_im'][l],
                                        p['ssm_c_re'][l], p['ssm_c_im'][l], p['ssm_d'][l], p['w_glu'][l])
            fin_re.append(fr.astype(h0_re.dtype))
            fin_im.append(fi.astype(h0_im.dtype))
        else:
            j = l - N_A_LAYERS
            q = _rmsnorm(mix_in.reshape(bsz, seq, MOBA_HEADS, HEAD_DIM), p['g_q'][j])
            mix_out = _moba_attend(q, k_all, v_all, pos0).reshape(bsz, seq, MIX_W)
        mqh = _rmsnorm(mq.reshape(bsz, seq, MEM_HEADS, HEAD_DIM), p['g_mq'][l])
        mem_out = _mem_attend(mqh, mem_k[l], mem_v[l]).reshape(bsz, seq, MEM_W)
        h = h + jnp.concatenate([mix_out, mem_out], axis=-1) @ p['w_out'][l]
        h = h + _swiglu(_rmsnorm(h, p['g_ffn'][l]), p['w_gu'][l], p['w_down'][l])
        if l == N_A_LAYERS - 1:
            k_new, v_new = _shared_kv(h, p['g_kv'], p['w_kv'], p['g_k'])
            k_all = jnp.concatenate([k_past.astype(k_new.dtype), k_new], axis=1)
            v_all = jnp.concatenate([v_past.astype(v_new.dtype), v_new], axis=1)
    return h, jnp.stack(fin_re), jnp.stack(fin_im), k_new, v_new


def setup_inputs(seed: int = 0) -> dict:
    key = jax.random.key(seed)
    ks = jax.random.split(key, 40)
    f32 = jnp.float32

    def nrm(i, shape, scale):
        return jax.random.normal(ks[i], shape, f32) * scale

    n_pages = PAST_LEN // PAGE_SIZE
    n_used = DEC_BATCH * n_pages
    n_phys = n_used + max(1, n_used // 4)
    page_table = jax.random.permutation(ks[0], n_phys)[:n_used].reshape(DEC_BATCH, n_pages).astype(jnp.int32)
    mix_total = MIX_W + MEM_W
    return {
        'x_prompt': nrm(1, (BATCH, SEQ, D_MODEL), 1.0),
        'x_sample': nrm(2, (DEC_BATCH, DEC_SEQ, D_MODEL), 1.0),
        'mem_prompt': nrm(3, (BATCH, MEM_TOKENS, D_MODEL), 1.0),
        'state_ssm_re': nrm(4, (N_A_LAYERS, DEC_BATCH, SSM_GROUPS, SSM_STATE), 0.1),
        'state_ssm_im': nrm(5, (N_A_LAYERS, DEC_BATCH, SSM_GROUPS, SSM_STATE), 0.1),
        'cache_k': nrm(6, (n_phys, PAGE_SIZE, MOBA_KV_HEADS, HEAD_DIM), 1.0),
        'cache_v': nrm(7, (n_phys, PAGE_SIZE, MOBA_KV_HEADS, HEAD_DIM), 1.0),
        'page_table': page_table,
        'cache_mem_k': nrm(8, (DEPTH, DEC_BATCH, MEM_TOKENS, MEM_HEADS, HEAD_DIM), 1.0),
        'cache_mem_v': nrm(9, (DEPTH, DEC_BATCH, MEM_TOKENS, MEM_HEADS, HEAD_DIM), 1.0),
        'g_mix': 1.0 + nrm(10, (DEPTH, D_MODEL), 0.02),
        'w_in': nrm(11, (DEPTH, D_MODEL, mix_total), D_MODEL ** -0.5),
        'w_out': nrm(12, (DEPTH, mix_total, D_MODEL), mix_total ** -0.5),
        'g_ffn': 1.0 + nrm(13, (DEPTH, D_MODEL), 0.02),
        'w_gu': nrm(14, (DEPTH, D_MODEL, 2 * D_FF), D_MODEL ** -0.5),
        'w_down': nrm(15, (DEPTH, D_FF, D_MODEL), D_FF ** -0.5),
        'ssm_a_re': -0.5 + nrm(16, (N_A_LAYERS, SSM_GROUPS, SSM_STATE), 0.01),
        'ssm_a_im': jnp.pi * jnp.arange(SSM_STATE, dtype=f32) + nrm(17, (N_A_LAYERS, SSM_GROUPS, SSM_STATE), 0.01),
        'ssm_log_dt': jax.random.uniform(ks[18], (N_A_LAYERS, SSM_GROUPS), f32, math.log(DT_MIN), math.log(DT_MAX)),
        'ssm_b_re': nrm(19, (N_A_LAYERS, SSM_GROUPS, SSM_STATE, SSM_GROUP), (2.0 * SSM_GROUP) ** -0.5),
        'ssm_b_im': nrm(20, (N_A_LAYERS, SSM_GROUPS, SSM_STATE, SSM_GROUP), (2.0 * SSM_GROUP) ** -0.5),
        'ssm_c_re': nrm(21, (N_A_LAYERS, SSM_GROUPS, SSM_GROUP, SSM_STATE), SSM_STATE ** -0.25),
        'ssm_c_im': nrm(22, (N_A_LAYERS, SSM_GROUPS, SSM_GROUP, SSM_STATE), SSM_STATE ** -0.25),
        'ssm_d': 1.0 + nrm(23, (N_A_LAYERS, SSM_GROUPS, SSM_GROUP), 0.02),
        'w_glu': nrm(24, (N_A_LAYERS, MIX_W, MIX_W), MIX_W ** -0.5),
        'g_q': 1.0 + nrm(25, (N_B_LAYERS, HEAD_DIM), 0.02),
        'g_mq': 1.0 + nrm(26, (DEPTH, HEAD_DIM), 0.02),
        'g_mem': 1.0 + nrm(27, (DEPTH, D_MODEL), 0.02),
        'w_mem_kv': nrm(28, (DEPTH, D_MODEL, 2 * MEM_W), D_MODEL ** -0.5),
        'g_mk': 1.0 + nrm(29, (DEPTH, HEAD_DIM), 0.02),
        'g_kv': 1.0 + nrm(30, (D_MODEL,), 0.02),
        'w_kv': nrm(31, (D_MODEL, 2 * KV_W), D_MODEL ** -0.5),
        'g_k': 1.0 + nrm(32, (HEAD_DIM,), 0.02),
    }


def reference(x_prompt, x_sample, mem_prompt, state_ssm_re, state_ssm_im, cache_k, cache_v, page_table,
              cache_mem_k, cache_mem_v, g_mix, w_in, w_out, g_ffn, w_gu, w_down, ssm_a_re, ssm_a_im,
              ssm_log_dt, ssm_b_re, ssm_b_im, ssm_c_re, ssm_c_im, ssm_d, w_glu, g_q, g_mq, g_mem,
              w_mem_kv, g_mk, g_kv, w_kv, g_k):
    p = dict(g_mix=g_mix, w_in=w_in, w_out=w_out, g_ffn=g_ffn, w_gu=w_gu, w_down=w_down,
             ssm_a_re=ssm_a_re, ssm_a_im=ssm_a_im, ssm_log_dt=ssm_log_dt, ssm_b_re=ssm_b_re,
             ssm_b_im=ssm_b_im, ssm_c_re=ssm_c_re, ssm_c_im=ssm_c_im, ssm_d=ssm_d, w_glu=w_glu,
             g_q=g_q, g_mq=g_mq, g_kv=g_kv, w_kv=w_kv, g_k=g_k)
    mks, mvs = [], []
    for l in range(DEPTH):
        mk, mv = _mem_kv(mem_prompt, g_mem[l], w_mem_kv[l], g_mk[l])
        mks.append(mk)
        mvs.append(mv)
    p_mem_k = jnp.stack(mks)
    p_mem_v = jnp.stack(mvs)
    bp = x_prompt.shape[0]
    h0 = jnp.zeros((N_A_LAYERS, bp, SSM_GROUPS, SSM_STATE), state_ssm_re.dtype)
    kv_empty = jnp.zeros((bp, 0, MOBA_KV_HEADS, HEAD_DIM), x_prompt.dtype)
    y_prompt, p_ssm_re, p_ssm_im, p_k, p_v = _run_trunk(
        x_prompt, 0, h0, h0, p_mem_k, p_mem_v, kv_empty, kv_empty, p)
    bs = x_sample.shape[0]
    n_pages = page_table.shape[1]
    past_len = n_pages * cache_k.shape[1]
    k_past = cache_k[page_table].reshape(bs, past_len, MOBA_KV_HEADS, HEAD_DIM)
    v_past = cache_v[page_table].reshape(bs, past_len, MOBA_KV_HEADS, HEAD_DIM)
    y_sample, s_ssm_re, s_ssm_im, s_k, s_v = _run_trunk(
        x_sample, past_len, state_ssm_re, state_ssm_im, cache_mem_k, cache_mem_v, k_past, v_past, p)
    return (y_prompt, y_sample, p_ssm_re, p_ssm_im, p_k, p_v, p_mem_k, p_mem_v, s_ssm_re, s_ssm_im, s_k, s_v)
```

```python
import functools
import math

import jax
import jax.numpy as jnp
from jax import lax
from jax.experimental import pallas as pl
from jax.experimental.pallas import tpu as pltpu

F32 = jnp.float32
BF16 = jnp.bfloat16

D_MODEL = 1024
DEPTH = 4
N_A_LAYERS = DEPTH // 2
HEAD_DIM = 64
MIX_W = 3 * D_MODEL // 4
MEM_W = D_MODEL // 4
MEM_HEADS = MEM_W // HEAD_DIM
SSM_GROUP = 16
SSM_GROUPS = MIX_W // SSM_GROUP
SSM_STATE = 64
N_STATE = SSM_GROUPS * SSM_STATE
MOBA_HEADS = MIX_W // HEAD_DIM
MOBA_KV_HEADS = 4
Q_PER_KV = MOBA_HEADS // MOBA_KV_HEADS
KV_W = MOBA_KV_HEADS * HEAD_DIM
MOBA_BLOCK = 256
MOBA_TOPK = 3
D_FF = ((-(-8 * D_MODEL // 3) + 255) // 256) * 256
EPS = 1e-6
ATTN_SCALE = HEAD_DIM ** -0.5
MASK_VALUE = -1e30

SCAN_SUB = 8
SCAN_STEPS = 32
SCAN_T = SCAN_SUB * SCAN_STEPS
SSM_KB = 256
N_KB = MIX_W // SSM_KB
KB_STATES = (SSM_KB // SSM_GROUP) * SSM_STATE
SCAN_CW = 512

VMEM_LIMIT_BYTES = 56 * 1024 * 1024


def _cparams(*sem):
    return pltpu.CompilerParams(dimension_semantics=sem, vmem_limit_bytes=VMEM_LIMIT_BYTES)


def _rms(x, g_row):
    ms = jnp.mean(x * x, axis=-1, keepdims=True)
    return x * lax.rsqrt(ms + EPS) * g_row


def _seg_rms(x, seg_mat, g_row):
    sq = x * x
    hi = sq.astype(BF16)
    lo = (sq - hi.astype(F32)).astype(BF16)
    ms = (jnp.dot(hi, seg_mat, preferred_element_type=F32)
          + jnp.dot(lo, seg_mat, preferred_element_type=F32))
    return x * lax.rsqrt(ms + EPS) * g_row


def _dot_nt(a, b, precision=None):
    return lax.dot_general(a, b, (((1,), (1,)), ((), ())), precision=precision,
                           preferred_element_type=F32)


def _topk_keep(gate, n_valid, axis):
    n = gate.shape[axis]
    idx = lax.broadcasted_iota(jnp.int32, gate.shape, axis)
    rank = jnp.zeros(gate.shape, jnp.int32)
    for j in range(n):
        gj = lax.slice_in_dim(gate, j, j + 1, axis=axis)
        beats = (gj > gate) | ((gj == gate) & (j < idx))
        rank = rank + jnp.where(beats, jnp.where(j < n_valid, 1, 0), 0)
    return (rank < MOBA_TOPK) & (idx < n_valid)


def _norm_matmul_kernel(x_ref, g_ref, w_ref, o_ref):
    a = _rms(x_ref[...], g_ref[...]).astype(BF16)
    o_ref[...] = jnp.dot(a, w_ref[...], preferred_element_type=F32)


def _norm_matmul(x, g_row, w, tm):
    n, d = x.shape
    nout = w.shape[1]
    return pl.pallas_call(
        _norm_matmul_kernel,
        grid=(n // tm,),
        in_specs=[pl.BlockSpec((tm, d), lambda i: (i, 0)),
                  pl.BlockSpec((1, d), lambda i: (0, 0)),
                  pl.BlockSpec((d, nout), lambda i: (0, 0))],
        out_specs=pl.BlockSpec((tm, nout), lambda i: (i, 0)),
        out_shape=jax.ShapeDtypeStruct((n, nout), F32),
        compiler_params=_cparams("parallel"),
        name="norm_matmul",
    )(x, g_row, w)


def _kv_kernel(x_ref, g_ref, w_ref, seg_ref, gk_ref, k_ref, v_ref, *extra, half, with_moba):
    a = _rms(x_ref[...], g_ref[0]).astype(BF16)
    kv = jnp.dot(a, w_ref[0], preferred_element_type=F32)
    k = _seg_rms(kv[:, :half], seg_ref[...], gk_ref[0])
    k_ref[0] = k
    v_ref[0] = kv[:, half:]
    if with_moba:
        kt_ref, kmean_ref = extra
        kt_ref[0] = k.T.astype(BF16)
        kmean_ref[0] = jnp.mean(k, axis=0, keepdims=True)


def _kv_proj(x, g_rows, w, seg_mat, gk_rows, tm, with_moba=False):
    n, d = x.shape
    layers, _, two_half = w.shape
    half = two_half // 2
    out_shape = [jax.ShapeDtypeStruct((layers, n, half), F32)] * 2
    out_specs = [pl.BlockSpec((1, tm, half), lambda l, i: (l, i, 0))] * 2
    if with_moba:
        out_shape += [jax.ShapeDtypeStruct((layers, half, n), BF16),
                      jax.ShapeDtypeStruct((layers * (n // tm), 1, half), F32)]
        out_specs += [pl.BlockSpec((1, half, tm), lambda l, i: (l, 0, i)),
                      pl.BlockSpec((1, 1, half), lambda l, i: (l * (n // tm) + i, 0, 0))]
    return pl.pallas_call(
        functools.partial(_kv_kernel, half=half, with_moba=with_moba),
        grid=(layers, n // tm),
        in_specs=[pl.BlockSpec((tm, d), lambda l, i: (i, 0)),
                  pl.BlockSpec((1, 1, d), lambda l, i: (l, 0, 0)),
                  pl.BlockSpec((1, d, two_half), lambda l, i: (l, 0, 0)),
                  pl.BlockSpec((half, half), lambda l, i: (0, 0)),
                  pl.BlockSpec((1, 1, half), lambda l, i: (l, 0, 0))],
        out_specs=out_specs,
        out_shape=out_shape,
        compiler_params=_cparams("parallel", "parallel"),
        name="kv_proj",
    )(x, g_rows, w, seg_mat, gk_rows)


def _out_proj_kernel(h_ref, mix_ref, mem_ref, w1_ref, w2_ref, o_ref):
    o_ref[...] = (h_ref[...]
                  + jnp.dot(mix_ref[...], w1_ref[...], preferred_element_type=F32)
                  + jnp.dot(mem_ref[...], w2_ref[...], preferred_element_type=F32))


def _out_proj(h, mix, mem, w_mix, w_mem, tm):
    n, d = h.shape
    return pl.pallas_call(
        _out_proj_kernel,
        grid=(n // tm,),
        in_specs=[pl.BlockSpec((tm, d), lambda i: (i, 0)),
                  pl.BlockSpec((tm, MIX_W), lambda i: (i, 0)),
                  pl.BlockSpec((tm, MEM_W), lambda i: (i, 0)),
                  pl.BlockSpec((MIX_W, d), lambda i: (0, 0)),
                  pl.BlockSpec((MEM_W, d), lambda i: (0, 0))],
        out_specs=pl.BlockSpec((tm, d), lambda i: (i, 0)),
        out_shape=jax.ShapeDtypeStruct((n, d), F32),
        compiler_params=_cparams("parallel"),
        name="out_proj",
    )(h, mix, mem, w_mix, w_mem)


def _ffn_kernel(h_ref, g_ref, wg_ref, wu_ref, wd_ref, o_ref, a_scr, acc_scr):
    j = pl.program_id(1)

    @pl.when(j == 0)
    def _():
        a_scr[...] = _rms(h_ref[...], g_ref[...]).astype(BF16)
        acc_scr[...] = jnp.zeros_like(acc_scr)

    a = a_scr[...]
    gate = jnp.dot(a, wg_ref[...], preferred_element_type=F32)
    up = jnp.dot(a, wu_ref[...], preferred_element_type=F32)
    act = (gate * jax.nn.sigmoid(gate) * up).astype(BF16)
    acc_scr[...] += jnp.dot(act, wd_ref[...], preferred_element_type=F32)

    @pl.when(j == pl.num_programs(1) - 1)
    def _():
        o_ref[...] = h_ref[...] + acc_scr[...]


def _ffn(h, g_row, w_gu, w_down, tm, fc):
    n, d = h.shape
    nchunk = D_FF // fc
    return pl.pallas_call(
        _ffn_kernel,
        grid=(n // tm, nchunk),
        in_specs=[pl.BlockSpec((tm, d), lambda i, j: (i, 0)),
                  pl.BlockSpec((1, d), lambda i, j: (0, 0)),
                  pl.BlockSpec((d, fc), lambda i, j: (0, j)),
                  pl.BlockSpec((d, fc), lambda i, j: (0, nchunk + j)),
                  pl.BlockSpec((fc, d), lambda i, j: (j, 0))],
        out_specs=pl.BlockSpec((tm, d), lambda i, j: (i, 0)),
        out_shape=jax.ShapeDtypeStruct((n, d), F32),
        scratch_shapes=[pltpu.VMEM((tm, d), BF16), pltpu.VMEM((tm, d), F32)],
        compiler_params=_cparams("parallel", "arbitrary"),
        name="ffn",
    )(h, g_row, w_gu, w_gu, w_down)


def _ssm_prep_kernel(are_ref, aim_ref, ldt_ref, bre_ref, bim_ref, abr_ref, abi_ref, bbr_ref, bbi_ref):
    dt = jnp.exp(ldt_ref[0])
    lam_re = jnp.minimum(are_ref[0], -1e-4)
    lam_im = aim_ref[0]
    mag = jnp.exp(dt * lam_re)
    ang = dt * lam_im
    ab_re = mag * jnp.cos(ang)
    ab_im = mag * jnp.sin(ang)
    den = lam_re * lam_re + lam_im * lam_im
    num_re = ab_re - 1.0
    f_re = (num_re * lam_re + ab_im * lam_im) / den
    f_im = (ab_im * lam_re - num_re * lam_im) / den
    b_re = bre_ref[0]
    b_im = bim_ref[0]
    abr_ref[0] = ab_re
    abi_ref[0] = ab_im
    bbr_ref[0] = f_re * b_re - f_im * b_im
    bbi_ref[0] = f_re * b_im + f_im * b_re


def _ssm_prep(a_re, a_im, log_dt, b_re, b_im):
    layers = a_re.shape[0]
    big = pl.BlockSpec((1, MIX_W, SSM_STATE), lambda l: (l, 0, 0))
    return pl.pallas_call(
        _ssm_prep_kernel,
        grid=(layers,),
        in_specs=[big, big, pl.BlockSpec((1, MIX_W, 1), lambda l: (l, 0, 0)), big, big],
        out_specs=[big] * 4,
        out_shape=[jax.ShapeDtypeStruct((layers, MIX_W, SSM_STATE), F32)] * 4,
        compiler_params=_cparams("parallel"),
        name="ssm_prep",
    )(a_re, a_im, log_dt, b_re, b_im)


def _re_cols(n0):
    kb, off = divmod(n0, KB_STATES)
    return kb * 2 * KB_STATES + off


def _s5_output(y_state, u, d_ref, wglu_ref):
    y = jax.nn.gelu(y_state + d_ref[...] * u)
    z = jnp.dot(y.astype(BF16), wglu_ref[...], preferred_element_type=F32)
    return (y * jax.nn.sigmoid(z)).astype(BF16)


def _s5_scan_kernel(u_ref, perm_ref, permt_ref, bmat_ref, are_ref, aim_ref, cmat_ref, d_ref, wglu_ref,
                    y_ref, fre_ref, fim_ref, x_scr, apr_scr, api_scr, str_scr, sti_scr):
    c = pl.program_id(1)

    @pl.when(c == 0)
    def _():
        str_scr[...] = jnp.zeros_like(str_scr)
        sti_scr[...] = jnp.zeros_like(sti_scr)
        ar, ai = are_ref[...], aim_ref[...]
        pr, pi = ar, ai
        for i in range(SCAN_STEPS):
            apr_scr[i:i + 1, :] = pr
            api_scr[i:i + 1, :] = pi
            pr, pi = pr * ar - pi * ai, pr * ai + pi * ar

    u = u_ref[...]
    up = jnp.dot(perm_ref[...], u.astype(BF16), preferred_element_type=F32).astype(BF16)
    for kb in range(N_KB):
        x_scr[:, kb * 2 * KB_STATES:(kb + 1) * 2 * KB_STATES] = jnp.dot(
            up[:, kb * SSM_KB:(kb + 1) * SSM_KB], bmat_ref[kb], preferred_element_type=F32)

    sub = lax.broadcasted_iota(jnp.int32, (SCAN_SUB, SCAN_CW), 0)
    for n0 in range(0, N_STATE, SCAN_CW):
        rc = _re_cols(n0)
        ic = rc + KB_STATES
        ar = jnp.broadcast_to(are_ref[:, n0:n0 + SCAN_CW], (SCAN_SUB, SCAN_CW))
        ai = jnp.broadcast_to(aim_ref[:, n0:n0 + SCAN_CW], (SCAN_SUB, SCAN_CW))

        def local_step(i, carry, rc=rc, ic=ic, ar=ar, ai=ai):
            xr, xi = carry
            r0 = pl.multiple_of(i * SCAN_SUB, SCAN_SUB)
            nxr = ar * xr - ai * xi + x_scr[pl.ds(r0, SCAN_SUB), rc:rc + SCAN_CW]
            nxi = ar * xi + ai * xr + x_scr[pl.ds(r0, SCAN_SUB), ic:ic + SCAN_CW]
            x_scr[pl.ds(r0, SCAN_SUB), rc:rc + SCAN_CW] = nxr
            x_scr[pl.ds(r0, SCAN_SUB), ic:ic + SCAN_CW] = nxi
            return nxr, nxi

        zero = jnp.zeros((SCAN_SUB, SCAN_CW), F32)
        fin_r, fin_i = lax.fori_loop(0, SCAN_STEPS, local_step, (zero, zero), unroll=4)

        a_t_r = apr_scr[SCAN_STEPS - 1:SCAN_STEPS, n0:n0 + SCAN_CW]
        a_t_i = api_scr[SCAN_STEPS - 1:SCAN_STEPS, n0:n0 + SCAN_CW]
        cr = str_scr[:, n0:n0 + SCAN_CW]
        ci = sti_scr[:, n0:n0 + SCAN_CW]
        car_r = jnp.zeros((SCAN_SUB, SCAN_CW), F32)
        car_i = jnp.zeros((SCAN_SUB, SCAN_CW), F32)
        for j in range(SCAN_SUB):
            car_r = jnp.where(sub == j, cr, car_r)
            car_i = jnp.where(sub == j, ci, car_i)
            cr, ci = (fin_r[j:j + 1] + a_t_r * cr - a_t_i * ci,
                      fin_i[j:j + 1] + a_t_r * ci + a_t_i * cr)
        str_scr[:, n0:n0 + SCAN_CW] = cr
        sti_scr[:, n0:n0 + SCAN_CW] = ci

        def fix_step(i, _, rc=rc, ic=ic, n0=n0, car_r=car_r, car_i=car_i):
            r0 = pl.multiple_of(i * SCAN_SUB, SCAN_SUB)
            pr = apr_scr[pl.ds(i, 1), n0:n0 + SCAN_CW]
            pi = api_scr[pl.ds(i, 1), n0:n0 + SCAN_CW]
            xr = x_scr[pl.ds(r0, SCAN_SUB), rc:rc + SCAN_CW]
            xi = x_scr[pl.ds(r0, SCAN_SUB), ic:ic + SCAN_CW]
            x_scr[pl.ds(r0, SCAN_SUB), rc:rc + SCAN_CW] = xr + (pr * car_r - pi * car_i)
            x_scr[pl.ds(r0, SCAN_SUB), ic:ic + SCAN_CW] = xi + (pr * car_i + pi * car_r)
            return 0

        lax.fori_loop(0, SCAN_STEPS, fix_step, 0, unroll=4)

    ys = [jnp.dot(x_scr[:, kb * 2 * KB_STATES:(kb + 1) * 2 * KB_STATES].astype(BF16), cmat_ref[kb],
                  preferred_element_type=F32) for kb in range(N_KB)]
    yp = jnp.concatenate(ys, axis=1)
    hi = yp.astype(BF16)
    lo = (yp - hi.astype(F32)).astype(BF16)
    y_state = (jnp.dot(permt_ref[...], hi, preferred_element_type=F32)
               + jnp.dot(permt_ref[...], lo, preferred_element_type=F32))
    y_ref[...] = _s5_output(y_state, u, d_ref, wglu_ref)
    fre_ref[0] = str_scr[...]
    fim_ref[0] = sti_scr[...]


def _s5_scan(proj, bsz, seq, perm, perm_t, bmat, a_re, a_im, cmat, d_row, w_glu):
    nchunk = seq // SCAN_T
    const2 = lambda b, c: (0, 0)
    const3 = lambda b, c: (0, 0, 0)
    state_spec = pl.BlockSpec((1, 1, N_STATE), lambda b, c: (b, 0, 0))
    return pl.pallas_call(
        _s5_scan_kernel,
        grid=(bsz, nchunk),
        in_specs=[pl.BlockSpec((SCAN_T, MIX_W), lambda b, c: (b * nchunk + c, 0)),
                  pl.BlockSpec((SCAN_T, SCAN_T), const2),
                  pl.BlockSpec((SCAN_T, SCAN_T), const2),
                  pl.BlockSpec((N_KB, SSM_KB, 2 * KB_STATES), const3),
                  pl.BlockSpec((1, N_STATE), const2),
                  pl.BlockSpec((1, N_STATE), const2),
                  pl.BlockSpec((N_KB, 2 * KB_STATES, SSM_KB), const3),
                  pl.BlockSpec((1, MIX_W), const2),
                  pl.BlockSpec((MIX_W, MIX_W), const2)],
        out_specs=[pl.BlockSpec((SCAN_T, MIX_W), lambda b, c: (b * nchunk + c, 0)), state_spec, state_spec],
        out_shape=[jax.ShapeDtypeStruct((bsz * seq, MIX_W), BF16),
                   jax.ShapeDtypeStruct((bsz, 1, N_STATE), F32),
                   jax.ShapeDtypeStruct((bsz, 1, N_STATE), F32)],
        scratch_shapes=[pltpu.VMEM((SCAN_T, 2 * N_STATE), F32),
                        pltpu.VMEM((SCAN_STEPS, N_STATE), F32),
                        pltpu.VMEM((SCAN_STEPS, N_STATE), F32),
                        pltpu.VMEM((1, N_STATE), F32),
                        pltpu.VMEM((1, N_STATE), F32)],
        compiler_params=_cparams("parallel", "arbitrary"),
        name="s5_scan",
    )(proj, perm, perm_t, bmat, a_re, a_im, cmat, d_row, w_glu)


def _s5_step_kernel(u_ref, hre_ref, him_ref, bmat_ref, are_ref, aim_ref, cmat_ref, d_ref, wglu_ref,
                    y_ref, xre_ref, xim_ref):
    u = u_ref[...]
    ub = u.astype(BF16)
    ys = []
    for kb in range(N_KB):
        bu = jnp.dot(ub[:, kb * SSM_KB:(kb + 1) * SSM_KB], bmat_ref[kb], preferred_element_type=F32)
        n0 = kb * KB_STATES
        ar, ai = are_ref[:, n0:n0 + KB_STATES], aim_ref[:, n0:n0 + KB_STATES]
        hr, hi = hre_ref[:, n0:n0 + KB_STATES], him_ref[:, n0:n0 + KB_STATES]
        xr = ar * hr - ai * hi + bu[:, :KB_STATES]
        xi = ar * hi + ai * hr + bu[:, KB_STATES:]
        xre_ref[:, n0:n0 + KB_STATES] = xr
        xim_ref[:, n0:n0 + KB_STATES] = xi
        x = jnp.concatenate([xr, xi], axis=1).astype(BF16)
        ys.append(jnp.dot(x, cmat_ref[kb], preferred_element_type=F32))
    y_ref[...] = _s5_output(jnp.concatenate(ys, axis=1), u, d_ref, wglu_ref)


def _s5_step(proj, h_re, h_im, bmat, a_re, a_im, cmat, d_row, w_glu):
    n = proj.shape[0]
    const2 = lambda i: (0, 0)
    const3 = lambda i: (0, 0, 0)
    return pl.pallas_call(
        _s5_step_kernel,
        grid=(1,),
        in_specs=[pl.BlockSpec((n, MIX_W), const2),
                  pl.BlockSpec((n, N_STATE), const2),
                  pl.BlockSpec((n, N_STATE), const2),
                  pl.BlockSpec((N_KB, SSM_KB, 2 * KB_STATES), const3),
                  pl.BlockSpec((1, N_STATE), const2),
                  pl.BlockSpec((1, N_STATE), const2),
                  pl.BlockSpec((N_KB, 2 * KB_STATES, SSM_KB), const3),
                  pl.BlockSpec((1, MIX_W), const2),
                  pl.BlockSpec((MIX_W, MIX_W), const2)],
        out_specs=[pl.BlockSpec((n, MIX_W), const2),
                   pl.BlockSpec((n, N_STATE), const2),
                   pl.BlockSpec((n, N_STATE), const2)],
        out_shape=[jax.ShapeDtypeStruct((n, MIX_W), BF16),
                   jax.ShapeDtypeStruct((n, N_STATE), F32),
                   jax.ShapeDtypeStruct((n, N_STATE), F32)],
        compiler_params=_cparams("arbitrary"),
        name="s5_step",
    )(proj, h_re, h_im, bmat, a_re, a_im, cmat, d_row, w_glu)


def _mem_attn_kernel(q_ref, k_ref, v_ref, seg_ref, g_ref, o_ref):
    q = _seg_rms(q_ref[...], seg_ref[...], g_ref[...]) * ATTN_SCALE
    k = k_ref[0]
    v = v_ref[0]
    lane_head = lax.broadcasted_iota(jnp.int32, q.shape, 1) // HEAD_DIM
    out = jnp.zeros(q.shape, F32)
    for h in range(MEM_HEADS):
        qh = jnp.where(lane_head == h, q, 0.0).astype(BF16)
        s = _dot_nt(qh, k)
        p = jnp.exp(s - jnp.max(s, axis=-1, keepdims=True))
        o = jnp.dot(p.astype(BF16), v, preferred_element_type=F32) / jnp.sum(p, axis=-1, keepdims=True)
        out = jnp.where(lane_head == h, o, out)
    o_ref[...] = out.astype(BF16)


def _mem_attn(proj, mem_k, mem_v, seg_mat, g_row, bsz, seq, tq):
    nq = seq // tq
    tokens = mem_k.shape[1]
    qcol = MIX_W // MEM_W
    kv_spec = pl.BlockSpec((1, tokens, MEM_W), lambda b, i: (b, 0, 0))
    return pl.pallas_call(
        _mem_attn_kernel,
        grid=(bsz, nq),
        in_specs=[pl.BlockSpec((tq, MEM_W), lambda b, i: (b * nq + i, qcol)),
                  kv_spec, kv_spec,
                  pl.BlockSpec((MEM_W, MEM_W), lambda b, i: (0, 0)),
                  pl.BlockSpec((1, MEM_W), lambda b, i: (0, 0))],
        out_specs=pl.BlockSpec((tq, MEM_W), lambda b, i: (b * nq + i, 0)),
        out_shape=jax.ShapeDtypeStruct((bsz * seq, MEM_W), BF16),
        compiler_params=_cparams("parallel", "parallel"),
        name="mem_attn",
    )(proj, mem_k, mem_v, seg_mat, g_row)


def _mem_attn_rows_kernel(q_ref, k_ref, v_ref, seg_ref, g_ref, o_ref, *, rows):
    q = _seg_rms(q_ref[...], seg_ref[...], g_ref[...]) * ATTN_SCALE
    hshape = (8, MEM_W)
    own = lax.broadcasted_iota(jnp.int32, hshape, 0) == lax.broadcasted_iota(jnp.int32, hshape, 1) // HEAD_DIM
    outs = []
    for r in range(rows):
        qh = jnp.where(own, jnp.broadcast_to(q[r:r + 1], hshape), 0.0).astype(BF16)
        s = _dot_nt(qh, k_ref[0, r].astype(BF16))
        p = jnp.exp(s - jnp.max(s, axis=-1, keepdims=True))
        o = (jnp.dot(p.astype(BF16), v_ref[0, r].astype(BF16), preferred_element_type=F32)
             / jnp.sum(p, axis=-1, keepdims=True))
        outs.append(jnp.sum(jnp.where(own, o, 0.0), axis=0, keepdims=True))
    o_ref[...] = jnp.concatenate(outs, axis=0).astype(BF16)


def _mem_attn_rows(proj, mem_k, mem_v, layer, seg_mat, g_row, rows=8):
    n = proj.shape[0]
    tokens = mem_k.shape[2]
    qcol = MIX_W // MEM_W
    kv_spec = pl.BlockSpec((1, rows, tokens, MEM_W), lambda i: (layer, i, 0, 0))
    return pl.pallas_call(
        functools.partial(_mem_attn_rows_kernel, rows=rows),
        grid=(n // rows,),
        in_specs=[pl.BlockSpec((rows, MEM_W), lambda i: (i, qcol)),
                  kv_spec, kv_spec,
                  pl.BlockSpec((MEM_W, MEM_W), lambda i: (0, 0)),
                  pl.BlockSpec((1, MEM_W), lambda i: (0, 0))],
        out_specs=pl.BlockSpec((rows, MEM_W), lambda i: (i, 0)),
        out_shape=jax.ShapeDtypeStruct((n, MEM_W), BF16),
        compiler_params=_cparams("parallel"),
        name="mem_attn_rows",
    )(proj, mem_k, mem_v, seg_mat, g_row)


def _moba_prefill_kernel(q_ref, kt_ref, v_ref, kmean_ref, seg_ref, g_ref, eye_ref, o_ref,
                         m_scr, l_scr, acc_scr, *, nblk):
    i = pl.program_id(1)
    blk = MOBA_BLOCK
    qn = _seg_rms(q_ref[...], seg_ref[...], g_ref[...])
    eye = eye_ref[...]
    rows = Q_PER_KV * blk
    r_idx = lax.broadcasted_iota(jnp.int32, (rows, blk), 0) % blk
    c_idx = lax.broadcasted_iota(jnp.int32, (rows, blk), 1)
    causal = c_idx <= r_idx
    i0 = pl.multiple_of(i * blk, blk)

    for g in range(MOBA_KV_HEADS):
        km = kmean_ref[0][:, g * HEAD_DIM:(g + 1) * HEAD_DIM]
        parts = []
        for hh in range(Q_PER_KV):
            h = g * Q_PER_KV + hh
            qh = qn[:, h * HEAD_DIM:(h + 1) * HEAD_DIM]
            gate_t = _dot_nt(km, qh, precision=lax.Precision.HIGHEST)
            keep_t = _topk_keep(gate_t, i, axis=0)
            past_t = lax.broadcasted_iota(jnp.int32, gate_t.shape, 0) < i
            bias_t = jnp.where(past_t & ~keep_t, MASK_VALUE, 0.0).astype(BF16)
            bias = _dot_nt(eye, bias_t)
            parts.append(jnp.concatenate([qh * ATTN_SCALE, bias], axis=1).astype(BF16))
        qa = jnp.concatenate(parts, axis=0)

        s = jnp.dot(qa, kt_ref[0, g, :, pl.ds(i0, blk)], preferred_element_type=F32)
        s = jnp.where(causal, s, MASK_VALUE)
        m = jnp.max(s, axis=-1, keepdims=True)
        p = jnp.exp(s - m)
        m_scr[...] = m
        l_scr[...] = jnp.sum(p, axis=-1, keepdims=True)
        acc_scr[...] = jnp.dot(p.astype(BF16), v_ref[0, g, pl.ds(i0, blk), :], preferred_element_type=F32)

        def kv_step(j, _, g=g, qa=qa):
            j0 = pl.multiple_of(j * blk, blk)
            s = jnp.dot(qa, kt_ref[0, g, :, pl.ds(j0, blk)], preferred_element_type=F32)
            m_old = m_scr[...]
            m_new = jnp.maximum(m_old, jnp.max(s, axis=-1, keepdims=True))
            alpha = jnp.exp(m_old - m_new)
            p = jnp.exp(s - m_new)
            l_scr[...] = alpha * l_scr[...] + jnp.sum(p, axis=-1, keepdims=True)
            acc_scr[...] = alpha * acc_scr[...] + jnp.dot(
                p.astype(BF16), v_ref[0, g, pl.ds(j0, blk), :], preferred_element_type=F32)
            m_scr[...] = m_new
            return 0

        lax.fori_loop(0, i, kv_step, 0)
        out = acc_scr[...] / l_scr[...]
        for hh in range(Q_PER_KV):
            h = g * Q_PER_KV + hh
            o_ref[:, h * HEAD_DIM:(h + 1) * HEAD_DIM] = out[hh * blk:(hh + 1) * blk].astype(BF16)


def _moba_prefill(proj, kt_aug, v_heads, kmean, seg_mat, g_row, eye, bsz, seq):
    nblk = seq // MOBA_BLOCK
    rows = Q_PER_KV * MOBA_BLOCK
    return pl.pallas_call(
        functools.partial(_moba_prefill_kernel, nblk=nblk),
        grid=(bsz, nblk),
        in_specs=[pl.BlockSpec((MOBA_BLOCK, MIX_W), lambda b, i: (b * nblk + i, 0)),
                  pl.BlockSpec((1, MOBA_KV_HEADS, HEAD_DIM + nblk, seq), lambda b, i: (b, 0, 0, 0)),
                  pl.BlockSpec((1, MOBA_KV_HEADS, seq, HEAD_DIM), lambda b, i: (b, 0, 0, 0)),
                  pl.BlockSpec((1, nblk, KV_W), lambda b, i: (b, 0, 0)),
                  pl.BlockSpec((MIX_W, MIX_W), lambda b, i: (0, 0)),
                  pl.BlockSpec((1, MIX_W), lambda b, i: (0, 0)),
                  pl.BlockSpec((MOBA_BLOCK, MOBA_BLOCK), lambda b, i: (0, 0))],
        out_specs=pl.BlockSpec((MOBA_BLOCK, MIX_W), lambda b, i: (b * nblk + i, 0)),
        out_shape=jax.ShapeDtypeStruct((bsz * seq, MIX_W), BF16),
        scratch_shapes=[pltpu.VMEM((rows, 1), F32), pltpu.VMEM((rows, 1), F32),
                        pltpu.VMEM((rows, HEAD_DIM), F32)],
        compiler_params=_cparams("parallel", "arbitrary"),
        name="moba_prefill",
    )(proj, kt_aug, v_heads, kmean, seg_mat, g_row, eye)


def _moba_decode_kernel(pt_ref, q_ref, knew_ref, vnew_ref, *rest, n_pages, page):
    k_pages = rest[:n_pages]
    v_pages = rest[n_pages:2 * n_pages]
    seg_ref, g_ref, place_ref, unplace_ref, ind_ref, o_ref = rest[2 * n_pages:]
    del pt_ref
    past = n_pages * page
    nblk = past // MOBA_BLOCK
    pages_per_blk = MOBA_BLOCK // page
    hrows = 16

    k = jnp.concatenate([r[...] for r in k_pages], axis=0)
    v = jnp.concatenate([r[...] for r in v_pages], axis=0).astype(BF16)
    kmean = jnp.concatenate(
        [jnp.mean(k[j * MOBA_BLOCK:(j + 1) * MOBA_BLOCK], axis=0, keepdims=True) for j in range(nblk)], axis=0)
    del pages_per_blk

    qn = _seg_rms(q_ref[:, :MIX_W], seg_ref[...], g_ref[...])
    rshape = (hrows, MIX_W)
    own = lax.broadcasted_iota(jnp.int32, rshape, 0) == lax.broadcasted_iota(jnp.int32, rshape, 1) // HEAD_DIM
    qrep = jnp.where(own, jnp.broadcast_to(qn, rshape), 0.0)
    qexp = jnp.dot(qrep, place_ref[...].astype(F32), precision=lax.Precision.HIGHEST,
                   preferred_element_type=F32)

    gate = _dot_nt(qexp, kmean, precision=lax.Precision.HIGHEST)
    keep = _topk_keep(gate, nblk, axis=1)
    bias = jnp.dot(jnp.where(keep, 0.0, MASK_VALUE).astype(BF16), ind_ref[...],
                   preferred_element_type=F32)
    s = _dot_nt(qexp.astype(BF16), k.astype(BF16)) * ATTN_SCALE + bias
    s_own = jnp.sum(qexp * knew_ref[...], axis=-1, keepdims=True) * ATTN_SCALE
    m = jnp.maximum(jnp.max(s, axis=-1, keepdims=True), s_own)
    p = jnp.exp(s - m)
    p_own = jnp.exp(s_own - m)
    denom = jnp.sum(p, axis=-1, keepdims=True) + p_own
    o = (jnp.dot(p.astype(BF16), v, preferred_element_type=F32) + p_own * vnew_ref[...]) / denom
    orep = jnp.dot(o.astype(BF16), unplace_ref[...], preferred_element_type=F32)
    o_ref[...] = jnp.sum(jnp.where(own, orep, 0.0), axis=0, keepdims=True).astype(BF16)


def _moba_decode(proj, k_new, v_new, cache_k, cache_v, page_table, seg_mat, g_row, place, unplace, ind):
    n = proj.shape[0]
    page = cache_k.shape[1]
    n_pages = page_table.shape[0] // n
    past = n_pages * page
    nblk = past // MOBA_BLOCK

    def page_spec(p):
        return pl.BlockSpec((None, page, KV_W), lambda b, pt, p=p: (pt[b * n_pages + p], 0, 0))

    const2 = lambda b, pt: (0, 0)
    grid_spec = pltpu.PrefetchScalarGridSpec(
        num_scalar_prefetch=1,
        grid=(n,),
        in_specs=([pl.BlockSpec((None, 1, D_MODEL), lambda b, pt: (b, 0, 0)),
                   pl.BlockSpec((None, 1, KV_W), lambda b, pt: (b, 0, 0)),
                   pl.BlockSpec((None, 1, KV_W), lambda b, pt: (b, 0, 0))]
                  + [page_spec(p) for p in range(n_pages)] * 1
                  + [page_spec(p) for p in range(n_pages)]
                  + [pl.BlockSpec((MIX_W, MIX_W), const2),
                     pl.BlockSpec((1, MIX_W), const2),
                     pl.BlockSpec((MIX_W, KV_W), const2),
                     pl.BlockSpec((KV_W, MIX_W), const2),
                     pl.BlockSpec((nblk, past), const2)]),
        out_specs=pl.BlockSpec((None, 1, MIX_W), lambda b, pt: (b, 0, 0)),
    )
    return pl.pallas_call(
        functools.partial(_moba_decode_kernel, n_pages=n_pages, page=page),
        grid_spec=grid_spec,
        out_shape=jax.ShapeDtypeStruct((n, 1, MIX_W), BF16),
        compiler_params=_cparams("parallel"),
        name="moba_decode",
    )(page_table, proj, k_new, v_new, *([cache_k] * n_pages), *([cache_v] * n_pages),
      seg_mat, g_row, place, unplace, ind)


def _seg_matrix(width):
    head = jnp.arange(width) // HEAD_DIM
    return (head[:, None] == head[None, :]).astype(BF16) / HEAD_DIM


def _scan_perm():
    r = jnp.arange(SCAN_T)
    t = (r % SCAN_SUB) * SCAN_STEPS + r // SCAN_SUB
    return (t[:, None] == jnp.arange(SCAN_T)[None, :]).astype(BF16)


def _block_diag(x, row_axes):
    eye = jnp.eye(x.shape[1], dtype=x.dtype)
    if row_axes == "gh":
        full = jnp.einsum("kghp,gG->kghGp", x, eye)
    else:
        full = jnp.einsum("kghp,gG->kgpGh", x, eye)
    k, g, a, _, b = full.shape
    return full.reshape(k, g * a, g * b)


def _s5_constants(p, l, ab_re, ab_im, bb_re, bb_im):
    gpb = SSM_KB // SSM_GROUP
    shape4 = (N_KB, gpb, SSM_GROUP, SSM_STATE)
    bmat = jnp.concatenate([_block_diag(bb_re[l].reshape(shape4), "gh"),
                            _block_diag(bb_im[l].reshape(shape4), "gh")], axis=-1).astype(BF16)
    c_re = p["ssm_c_re"][l].reshape(shape4)
    c_im = p["ssm_c_im"][l].reshape(shape4)
    cmat = jnp.concatenate([_block_diag(c_re, "gp"), -_block_diag(c_im, "gp")], axis=1).astype(BF16)
    a_re = ab_re[l, ::SSM_GROUP].reshape(1, N_STATE)
    a_im = ab_im[l, ::SSM_GROUP].reshape(1, N_STATE)
    d_row = p["ssm_d"][l].reshape(1, MIX_W)
    return bmat, a_re, a_im, cmat, d_row, p["w_glu"][l].astype(BF16)


def kernel(x_prompt, x_sample, mem_prompt, state_ssm_re, state_ssm_im, cache_k, cache_v, page_table,
           cache_mem_k, cache_mem_v, g_mix, w_in, w_out, g_ffn, w_gu, w_down, ssm_a_re, ssm_a_im,
           ssm_log_dt, ssm_b_re, ssm_b_im, ssm_c_re, ssm_c_im, ssm_d, w_glu, g_q, g_mq, g_mem,
           w_mem_kv, g_mk, g_kv, w_kv, g_k):
    bsz, seq, _ = x_prompt.shape
    n_p = bsz * seq
    n_s = x_sample.shape[0]
    tokens = mem_prompt.shape[1]
    p = dict(ssm_c_re=ssm_c_re, ssm_c_im=ssm_c_im, ssm_d=ssm_d, w_glu=w_glu)

    w_in_b = w_in.astype(BF16)
    w_out_b = w_out.astype(BF16)
    w_gu_b = w_gu.astype(BF16)
    w_down_b = w_down.astype(BF16)
    seg_kv = _seg_matrix(KV_W)
    seg_mix = _seg_matrix(MIX_W)
    gq_rows = jnp.tile(g_q, (1, MOBA_HEADS))
    gmq_rows = jnp.tile(g_mq, (1, MEM_HEADS))

    rep = lambda a: jnp.repeat(a, SSM_GROUP, axis=1)
    b_rows = lambda b: jnp.swapaxes(b, 2, 3).reshape(N_A_LAYERS, MIX_W, SSM_STATE)
    ab_re, ab_im, bb_re, bb_im = _ssm_prep(rep(ssm_a_re), rep(ssm_a_im), rep(ssm_log_dt)[..., None],
                                           b_rows(ssm_b_re), b_rows(ssm_b_im))
    s5 = [_s5_constants(p, l, ab_re, ab_im, bb_re, bb_im) for l in range(N_A_LAYERS)]
    perm = _scan_perm()
    perm_t = perm.T

    mem_flat = mem_prompt.reshape(bsz * tokens, D_MODEL)
    pm_k, pm_v = _kv_proj(mem_flat, g_mem[:, None, :], w_mem_kv.astype(BF16), seg_kv,
                          jnp.tile(g_mk, (1, MEM_HEADS))[:, None, :], tm=256)
    p_mem_k = pm_k.reshape(DEPTH, bsz, tokens, MEM_HEADS, HEAD_DIM)
    p_mem_v = pm_v.reshape(DEPTH, bsz, tokens, MEM_HEADS, HEAD_DIM)
    pm_k_b = pm_k.reshape(DEPTH, bsz, tokens, MEM_W).astype(BF16)
    pm_v_b = pm_v.reshape(DEPTH, bsz, tokens, MEM_W).astype(BF16)

    g_kv_rows = g_kv.reshape(1, 1, D_MODEL)
    w_kv_b = w_kv.astype(BF16)[None]
    g_k_rows = jnp.tile(g_k, MOBA_KV_HEADS).reshape(1, 1, KV_W)

    h = x_prompt.reshape(n_p, D_MODEL)
    p_fin_re, p_fin_im = [], []
    kt_aug = v_heads = kmean = p_k = p_v = None
    nblk = seq // MOBA_BLOCK
    for l in range(DEPTH):
        proj = _norm_matmul(h, g_mix[l][None], w_in_b[l], tm=512)
        if l < N_A_LAYERS:
            mix, f_re, f_im = _s5_scan(proj, bsz, seq, perm, perm_t, *s5[l])
            p_fin_re.append(f_re.reshape(bsz, SSM_GROUPS, SSM_STATE))
            p_fin_im.append(f_im.reshape(bsz, SSM_GROUPS, SSM_STATE))
        else:
            mix = _moba_prefill(proj, kt_aug, v_heads, kmean, seg_mix, gq_rows[l - N_A_LAYERS][None],
                                jnp.eye(MOBA_BLOCK, dtype=BF16), bsz, seq)
        mem = _mem_attn(proj, pm_k_b[l], pm_v_b[l], seg_kv, gmq_rows[l][None], bsz, seq, tq=512)
        h = _out_proj(h, mix, mem, w_out_b[l, :MIX_W], w_out_b[l, MIX_W:], tm=512)
        h = _ffn(h, g_ffn[l][None], w_gu_b[l], w_down_b[l], tm=512, fc=D_FF // 2)
        if l == N_A_LAYERS - 1:
            k_new, v_new, k_t, kmean = _kv_proj(h, g_kv_rows, w_kv_b, seg_kv, g_k_rows, tm=MOBA_BLOCK,
                                                with_moba=True)
            p_k = k_new.reshape(bsz, seq, MOBA_KV_HEADS, HEAD_DIM)
            p_v = v_new.reshape(bsz, seq, MOBA_KV_HEADS, HEAD_DIM)
            k_t = k_t.reshape(MOBA_KV_HEADS, HEAD_DIM, bsz, seq).transpose(2, 0, 1, 3)
            onehot = (jnp.arange(seq)[None, :] // MOBA_BLOCK == jnp.arange(nblk)[:, None]).astype(BF16)
            kt_aug = jnp.concatenate(
                [k_t, jnp.broadcast_to(onehot, (bsz, MOBA_KV_HEADS, nblk, seq))], axis=2)
            v_heads = p_v.transpose(0, 2, 1, 3).astype(BF16)
            kmean = kmean.reshape(bsz, nblk, KV_W)
    y_prompt = h.reshape(bsz, seq, D_MODEL)

    n_pages = page_table.shape[1]
    page = cache_k.shape[1]
    past = n_pages * page
    ck = cache_k.reshape(cache_k.shape[0], page, KV_W)
    cv = cache_v.reshape(cache_v.shape[0], page, KV_W)
    cmk = cache_mem_k.reshape(DEPTH, n_s, tokens, MEM_W)
    cmv = cache_mem_v.reshape(DEPTH, n_s, tokens, MEM_W)
    kv_of_lane = (jnp.arange(MIX_W) // HEAD_DIM) // Q_PER_KV * HEAD_DIM + jnp.arange(MIX_W) % HEAD_DIM
    place = (kv_of_lane[:, None] == jnp.arange(KV_W)[None, :]).astype(BF16)
    ind = (jnp.arange(past)[None, :] // MOBA_BLOCK == jnp.arange(past // MOBA_BLOCK)[:, None]).astype(BF16)

    h = x_sample.reshape(n_s, D_MODEL)
    s_fin_re, s_fin_im = [], []
    s_k = s_v = None
    for l in range(DEPTH):
        proj = _norm_matmul(h, g_mix[l][None], w_in_b[l], tm=n_s)
        if l < N_A_LAYERS:
            mix, x_re, x_im = _s5_step(proj, state_ssm_re[l].reshape(n_s, N_STATE),
                                       state_ssm_im[l].reshape(n_s, N_STATE), *s5[l])
            s_fin_re.append(x_re.reshape(n_s, SSM_GROUPS, SSM_STATE))
            s_fin_im.append(x_im.reshape(n_s, SSM_GROUPS, SSM_STATE))
        else:
            mix = _moba_decode(proj.reshape(n_s, 1, D_MODEL), s_k.reshape(n_s, 1, KV_W),
                               s_v.reshape(n_s, 1, KV_W), ck, cv, page_table.reshape(-1), seg_mix,
                               gq_rows[l - N_A_LAYERS][None], place, place.T, ind).reshape(n_s, MIX_W)
        mem = _mem_attn_rows(proj, cmk, cmv, l, seg_kv, gmq_rows[l][None])
        h = _out_proj(h, mix, mem, w_out_b[l, :MIX_W], w_out_b[l, MIX_W:], tm=n_s)
        h = _ffn(h, g_ffn[l][None], w_gu_b[l], w_down_b[l], tm=n_s, fc=D_FF // 2)
        if l == N_A_LAYERS - 1:
            s_k, s_v = _kv_proj(h, g_kv_rows, w_kv_b, seg_kv, g_k_rows, tm=n_s)
    y_sample = h.reshape(n_s, 1, D_MODEL)

    return (y_prompt, y_sample,
            jnp.stack(p_fin_re), jnp.stack(p_fin_im), p_k, p_v, p_mem_k, p_mem_v,
            jnp.stack(s_fin_re), jnp.stack(s_fin_im),
            s_k.reshape(n_s, 1, MOBA_KV_HEADS, HEAD_DIM), s_v.reshape(n_s, 1, MOBA_KV_HEADS, HEAD_DIM))
```

```python
import functools
import math

import jax
import jax.numpy as jnp
from jax import lax
from jax.experimental import pallas as pl
from jax.experimental.pallas import tpu as pltpu

F32 = jnp.float32
BF16 = jnp.bfloat16

D_MODEL = 1024
DEPTH = 4
N_A_LAYERS = DEPTH // 2
HEAD_DIM = 64
MIX_W = 3 * D_MODEL // 4
MEM_W = D_MODEL // 4
MEM_HEADS = MEM_W // HEAD_DIM
SSM_GROUP = 16
SSM_GROUPS = MIX_W // SSM_GROUP
SSM_STATE = 64
N_STATE = SSM_GROUPS * SSM_STATE
MOBA_HEADS = MIX_W // HEAD_DIM
MOBA_KV_HEADS = 4
Q_PER_KV = MOBA_HEADS // MOBA_KV_HEADS
KV_W = MOBA_KV_HEADS * HEAD_DIM
MOBA_BLOCK = 256
MOBA_TOPK = 3
D_FF = ((-(-8 * D_MODEL // 3) + 255) // 256) * 256
EPS = 1e-6
ATTN_SCALE = HEAD_DIM ** -0.5
MASK_VALUE = -1e30
KAUG_W = 128

SCAN_SUB = 8
SCAN_STEPS = 32
SCAN_T = SCAN_SUB * SCAN_STEPS
SSM_KB = 256
N_KB = MIX_W // SSM_KB
KB_STATES = (SSM_KB // SSM_GROUP) * SSM_STATE
SCAN_CW = 512

VMEM_LIMIT_BYTES = 56 * 1024 * 1024


def _cparams(*sem):
    return pltpu.CompilerParams(dimension_semantics=sem, vmem_limit_bytes=VMEM_LIMIT_BYTES)


def _rms(x, g_row):
    ms = jnp.mean(x * x, axis=-1, keepdims=True)
    return x * lax.rsqrt(ms + EPS) * g_row


def _seg_rms(x, seg_mat, g_row):
    sq = x * x
    hi = sq.astype(BF16)
    lo = (sq - hi.astype(F32)).astype(BF16)
    ms = (jnp.dot(hi, seg_mat, preferred_element_type=F32)
          + jnp.dot(lo, seg_mat, preferred_element_type=F32))
    return x * lax.rsqrt(ms + EPS) * g_row


def _dot_nt(a, b, precision=None):
    return lax.dot_general(a, b, (((1,), (1,)), ((), ())), precision=precision,
                           preferred_element_type=F32)


def _topk_keep(gate, n_valid, axis):
    n = gate.shape[axis]
    idx = lax.broadcasted_iota(jnp.int32, gate.shape, axis)
    rank = jnp.zeros(gate.shape, jnp.int32)
    for j in range(n):
        gj = lax.slice_in_dim(gate, j, j + 1, axis=axis)
        beats = (gj > gate) | ((gj == gate) & (j < idx))
        rank = rank + jnp.where(beats, jnp.where(j < n_valid, 1, 0), 0)
    return (rank < MOBA_TOPK) & (idx < n_valid)


def _norm_matmul_kernel(x_ref, g_ref, w_ref, o_ref):
    a = _rms(x_ref[...], g_ref[...]).astype(BF16)
    o_ref[...] = jnp.dot(a, w_ref[...], preferred_element_type=F32)


def _norm_matmul(x, g_row, w, tm):
    n, d = x.shape
    nout = w.shape[1]
    return pl.pallas_call(
        _norm_matmul_kernel,
        grid=(n // tm,),
        in_specs=[pl.BlockSpec((tm, d), lambda i: (i, 0)),
                  pl.BlockSpec((1, d), lambda i: (0, 0)),
                  pl.BlockSpec((d, nout), lambda i: (0, 0))],
        out_specs=pl.BlockSpec((tm, nout), lambda i: (i, 0)),
        out_shape=jax.ShapeDtypeStruct((n, nout), F32),
        compiler_params=_cparams("parallel"),
        name="norm_matmul",
    )(x, g_row, w)


def _kv_kernel(x_ref, g_ref, w_ref, seg_ref, gk_ref, k_ref, v_ref, *extra, half, blocks_per_seq):
    a = _rms(x_ref[...], g_ref[0]).astype(BF16)
    kv = jnp.dot(a, w_ref[0], preferred_element_type=F32)
    k = _seg_rms(kv[:, :half], seg_ref[...], gk_ref[0])
    v = kv[:, half:]
    k_ref[0] = k
    v_ref[0] = v
    if blocks_per_seq:
        kaug_ref, vt_ref, kmean_ref = extra
        lane = lax.broadcasted_iota(jnp.int32, (k.shape[0], KAUG_W), 1)
        tail = jnp.where(lane == HEAD_DIM + pl.program_id(1) % blocks_per_seq, 1.0, 0.0)
        for g in range(MOBA_KV_HEADS):
            col = k[:, (g // 2) * KAUG_W:(g // 2 + 1) * KAUG_W]
            if g % 2:
                col = pltpu.roll(col, HEAD_DIM, axis=1)
            kaug_ref[g] = jnp.where(lane < HEAD_DIM, col, tail).astype(BF16)
        vt_ref[...] = v.T.astype(BF16)
        kmean_ref[0] = jnp.mean(k, axis=0, keepdims=True)


def _kv_proj(x, g_rows, w, seg_mat, gk_rows, tm, blocks_per_seq=0):
    n, d = x.shape
    layers, _, two_half = w.shape
    half = two_half // 2
    out_shape = [jax.ShapeDtypeStruct((layers, n, half), F32)] * 2
    out_specs = [pl.BlockSpec((1, tm, half), lambda l, i: (l, i, 0))] * 2
    if blocks_per_seq:
        assert layers == 1 and tm == MOBA_BLOCK and half == KV_W
        out_shape += [jax.ShapeDtypeStruct((MOBA_KV_HEADS, n, KAUG_W), BF16),
                      jax.ShapeDtypeStruct((half, n), BF16),
                      jax.ShapeDtypeStruct((n // tm, 1, half), F32)]
        out_specs += [pl.BlockSpec((MOBA_KV_HEADS, tm, KAUG_W), lambda l, i: (0, i, 0)),
                      pl.BlockSpec((half, tm), lambda l, i: (0, i)),
                      pl.BlockSpec((1, 1, half), lambda l, i: (i, 0, 0))]
    return pl.pallas_call(
        functools.partial(_kv_kernel, half=half, blocks_per_seq=blocks_per_seq),
        grid=(layers, n // tm),
        in_specs=[pl.BlockSpec((tm, d), lambda l, i: (i, 0)),
                  pl.BlockSpec((1, 1, d), lambda l, i: (l, 0, 0)),
                  pl.BlockSpec((1, d, two_half), lambda l, i: (l, 0, 0)),
                  pl.BlockSpec((half, half), lambda l, i: (0, 0)),
                  pl.BlockSpec((1, 1, half), lambda l, i: (l, 0, 0))],
        out_specs=out_specs,
        out_shape=out_shape,
        compiler_params=_cparams("parallel", "parallel"),
        name="kv_proj",
    )(x, g_rows, w, seg_mat, gk_rows)


def _out_proj_kernel(h_ref, mix_ref, mem_ref, w1_ref, w2_ref, o_ref):
    o_ref[...] = (h_ref[...]
                  + jnp.dot(mix_ref[...], w1_ref[...], preferred_element_type=F32)
                  + jnp.dot(mem_ref[...], w2_ref[...], preferred_element_type=F32))


def _out_proj(h, mix, mem, w_mix, w_mem, tm):
    n, d = h.shape
    return pl.pallas_call(
        _out_proj_kernel,
        grid=(n // tm,),
        in_specs=[pl.BlockSpec((tm, d), lambda i: (i, 0)),
                  pl.BlockSpec((tm, MIX_W), lambda i: (i, 0)),
                  pl.BlockSpec((tm, MEM_W), lambda i: (i, 0)),
                  pl.BlockSpec((MIX_W, d), lambda i: (0, 0)),
                  pl.BlockSpec((MEM_W, d), lambda i: (0, 0))],
        out_specs=pl.BlockSpec((tm, d), lambda i: (i, 0)),
        out_shape=jax.ShapeDtypeStruct((n, d), F32),
        compiler_params=_cparams("parallel"),
        name="out_proj",
    )(h, mix, mem, w_mix, w_mem)


def _ffn_kernel(h_ref, g_ref, wg_ref, wu_ref, wd_ref, o_ref, a_scr, acc_scr):
    j = pl.program_id(1)

    @pl.when(j == 0)
    def _():
        a_scr[...] = _rms(h_ref[...], g_ref[...]).astype(BF16)
        acc_scr[...] = jnp.zeros_like(acc_scr)

    a = a_scr[...]
    gate = jnp.dot(a, wg_ref[...], preferred_element_type=F32)
    up = jnp.dot(a, wu_ref[...], preferred_element_type=F32)
    act = (gate * jax.nn.sigmoid(gate) * up).astype(BF16)
    acc_scr[...] += jnp.dot(act, wd_ref[...], preferred_element_type=F32)

    @pl.when(j == pl.num_programs(1) - 1)
    def _():
        o_ref[...] = h_ref[...] + acc_scr[...]


def _ffn(h, g_row, w_gu, w_down, tm, fc):
    n, d = h.shape
    nchunk = D_FF // fc
    return pl.pallas_call(
        _ffn_kernel,
        grid=(n // tm, nchunk),
        in_specs=[pl.BlockSpec((tm, d), lambda i, j: (i, 0)),
                  pl.BlockSpec((1, d), lambda i, j: (0, 0)),
                  pl.BlockSpec((d, fc), lambda i, j: (0, j)),
                  pl.BlockSpec((d, fc), lambda i, j: (0, nchunk + j)),
                  pl.BlockSpec((fc, d), lambda i, j: (j, 0))],
        out_specs=pl.BlockSpec((tm, d), lambda i, j: (i, 0)),
        out_shape=jax.ShapeDtypeStruct((n, d), F32),
        scratch_shapes=[pltpu.VMEM((tm, d), BF16), pltpu.VMEM((tm, d), F32)],
        compiler_params=_cparams("parallel", "arbitrary"),
        name="ffn",
    )(h, g_row, w_gu, w_gu, w_down)


def _ssm_prep_kernel(are_ref, aim_ref, ldt_ref, bre_ref, bim_ref, abr_ref, abi_ref, bbr_ref, bbi_ref):
    dt = jnp.exp(ldt_ref[0])
    lam_re = jnp.minimum(are_ref[0], -1e-4)
    lam_im = aim_ref[0]
    mag = jnp.exp(dt * lam_re)
    ang = dt * lam_im
    ab_re = mag * jnp.cos(ang)
    ab_im = mag * jnp.sin(ang)
    den = lam_re * lam_re + lam_im * lam_im
    num_re = ab_re - 1.0
    f_re = (num_re * lam_re + ab_im * lam_im) / den
    f_im = (ab_im * lam_re - num_re * lam_im) / den
    b_re = bre_ref[0]
    b_im = bim_ref[0]
    abr_ref[0] = ab_re
    abi_ref[0] = ab_im
    bbr_ref[0] = f_re * b_re - f_im * b_im
    bbi_ref[0] = f_re * b_im + f_im * b_re


def _ssm_prep(a_re, a_im, log_dt, b_re, b_im):
    layers = a_re.shape[0]
    big = pl.BlockSpec((1, MIX_W, SSM_STATE), lambda l: (l, 0, 0))
    return pl.pallas_call(
        _ssm_prep_kernel,
        grid=(layers,),
        in_specs=[big, big, pl.BlockSpec((1, MIX_W, 1), lambda l: (l, 0, 0)), big, big],
        out_specs=[big] * 4,
        out_shape=[jax.ShapeDtypeStruct((layers, MIX_W, SSM_STATE), F32)] * 4,
        compiler_params=_cparams("parallel"),
        name="ssm_prep",
    )(a_re, a_im, log_dt, b_re, b_im)


def _re_cols(n0):
    kb, off = divmod(n0, KB_STATES)
    return kb * 2 * KB_STATES + off


def _s5_output(y_state, u, d_ref, wglu_ref):
    y = jax.nn.gelu(y_state + d_ref[...] * u)
    z = jnp.dot(y.astype(BF16), wglu_ref[...], preferred_element_type=F32)
    return (y * jax.nn.sigmoid(z)).astype(BF16)


def _s5_scan_kernel(u_ref, perm_ref, permt_ref, bmat_ref, are_ref, aim_ref, cmat_ref, d_ref, wglu_ref,
                    y_ref, fre_ref, fim_ref, x_scr, apr_scr, api_scr, str_scr, sti_scr):
    c = pl.program_id(1)

    @pl.when(c == 0)
    def _():
        str_scr[...] = jnp.zeros_like(str_scr)
        sti_scr[...] = jnp.zeros_like(sti_scr)
        ar, ai = are_ref[...], aim_ref[...]
        pr, pi = ar, ai
        for i in range(SCAN_STEPS):
            apr_scr[i:i + 1, :] = pr
            api_scr[i:i + 1, :] = pi
            pr, pi = pr * ar - pi * ai, pr * ai + pi * ar

    u = u_ref[...]
    up = jnp.dot(perm_ref[...], u.astype(BF16), preferred_element_type=F32).astype(BF16)
    for kb in range(N_KB):
        x_scr[:, kb * 2 * KB_STATES:(kb + 1) * 2 * KB_STATES] = jnp.dot(
            up[:, kb * SSM_KB:(kb + 1) * SSM_KB], bmat_ref[kb], preferred_element_type=F32)

    sub = lax.broadcasted_iota(jnp.int32, (SCAN_SUB, SCAN_CW), 0)
    for n0 in range(0, N_STATE, SCAN_CW):
        rc = _re_cols(n0)
        ic = rc + KB_STATES
        ar = jnp.broadcast_to(are_ref[:, n0:n0 + SCAN_CW], (SCAN_SUB, SCAN_CW))
        ai = jnp.broadcast_to(aim_ref[:, n0:n0 + SCAN_CW], (SCAN_SUB, SCAN_CW))

        def local_step(i, carry, rc=rc, ic=ic, ar=ar, ai=ai):
            xr, xi = carry
            r0 = pl.multiple_of(i * SCAN_SUB, SCAN_SUB)
            nxr = ar * xr - ai * xi + x_scr[pl.ds(r0, SCAN_SUB), rc:rc + SCAN_CW]
            nxi = ar * xi + ai * xr + x_scr[pl.ds(r0, SCAN_SUB), ic:ic + SCAN_CW]
            x_scr[pl.ds(r0, SCAN_SUB), rc:rc + SCAN_CW] = nxr
            x_scr[pl.ds(r0, SCAN_SUB), ic:ic + SCAN_CW] = nxi
            return nxr, nxi

        zero = jnp.zeros((SCAN_SUB, SCAN_CW), F32)
        fin_r, fin_i = lax.fori_loop(0, SCAN_STEPS, local_step, (zero, zero), unroll=4)

        a_t_r = apr_scr[SCAN_STEPS - 1:SCAN_STEPS, n0:n0 + SCAN_CW]
        a_t_i = api_scr[SCAN_STEPS - 1:SCAN_STEPS, n0:n0 + SCAN_CW]
        cr = str_scr[:, n0:n0 + SCAN_CW]
        ci = sti_scr[:, n0:n0 + SCAN_CW]
        car_r = jnp.zeros((SCAN_SUB, SCAN_CW), F32)
        car_i = jnp.zeros((SCAN_SUB, SCAN_CW), F32)
        for j in range(SCAN_SUB):
            car_r = jnp.where(sub == j, cr, car_r)
            car_i = jnp.where(sub == j, ci, car_i)
            cr, ci = (fin_r[j:j + 1] + a_t_r * cr - a_t_i * ci,
                      fin_i[j:j + 1] + a_t_r * ci + a_t_i * cr)
        str_scr[:, n0:n0 + SCAN_CW] = cr
        sti_scr[:, n0:n0 + SCAN_CW] = ci

        def fix_step(i, _, rc=rc, ic=ic, n0=n0, car_r=car_r, car_i=car_i):
            r0 = pl.multiple_of(i * SCAN_SUB, SCAN_SUB)
            pr = apr_scr[pl.ds(i, 1), n0:n0 + SCAN_CW]
            pi = api_scr[pl.ds(i, 1), n0:n0 + SCAN_CW]
            xr = x_scr[pl.ds(r0, SCAN_SUB), rc:rc + SCAN_CW]
            xi = x_scr[pl.ds(r0, SCAN_SUB), ic:ic + SCAN_CW]
            x_scr[pl.ds(r0, SCAN_SUB), rc:rc + SCAN_CW] = xr + (pr * car_r - pi * car_i)
            x_scr[pl.ds(r0, SCAN_SUB), ic:ic + SCAN_CW] = xi + (pr * car_i + pi * car_r)
            return 0

        lax.fori_loop(0, SCAN_STEPS, fix_step, 0, unroll=4)

    ys = [jnp.dot(x_scr[:, kb * 2 * KB_STATES:(kb + 1) * 2 * KB_STATES].astype(BF16), cmat_ref[kb],
                  preferred_element_type=F32) for kb in range(N_KB)]
    yp = jnp.concatenate(ys, axis=1)
    hi = yp.astype(BF16)
    lo = (yp - hi.astype(F32)).astype(BF16)
    y_state = (jnp.dot(permt_ref[...], hi, preferred_element_type=F32)
               + jnp.dot(permt_ref[...], lo, preferred_element_type=F32))
    y_ref[...] = _s5_output(y_state, u, d_ref, wglu_ref)
    fre_ref[0] = str_scr[...]
    fim_ref[0] = sti_scr[...]


def _s5_scan(proj, bsz, seq, perm, perm_t, bmat, a_re, a_im, cmat, d_row, w_glu):
    nchunk = seq // SCAN_T
    const2 = lambda b, c: (0, 0)
    const3 = lambda b, c: (0, 0, 0)
    state_spec = pl.BlockSpec((1, 1, N_STATE), lambda b, c: (b, 0, 0))
    return pl.pallas_call(
        _s5_scan_kernel,
        grid=(bsz, nchunk),
        in_specs=[pl.BlockSpec((SCAN_T, MIX_W), lambda b, c: (b * nchunk + c, 0)),
                  pl.BlockSpec((SCAN_T, SCAN_T), const2),
                  pl.BlockSpec((SCAN_T, SCAN_T), const2),
                  pl.BlockSpec((N_KB, SSM_KB, 2 * KB_STATES), const3),
                  pl.BlockSpec((1, N_STATE), const2),
                  pl.BlockSpec((1, N_STATE), const2),
                  pl.BlockSpec((N_KB, 2 * KB_STATES, SSM_KB), const3),
                  pl.BlockSpec((1, MIX_W), const2),
                  pl.BlockSpec((MIX_W, MIX_W), const2)],
        out_specs=[pl.BlockSpec((SCAN_T, MIX_W), lambda b, c: (b * nchunk + c, 0)), state_spec, state_spec],
        out_shape=[jax.ShapeDtypeStruct((bsz * seq, MIX_W), BF16),
                   jax.ShapeDtypeStruct((bsz, 1, N_STATE), F32),
                   jax.ShapeDtypeStruct((bsz, 1, N_STATE), F32)],
        scratch_shapes=[pltpu.VMEM((SCAN_T, 2 * N_STATE), F32),
                        pltpu.VMEM((SCAN_STEPS, N_STATE), F32),
                        pltpu.VMEM((SCAN_STEPS, N_STATE), F32),
                        pltpu.VMEM((1, N_STATE), F32),
                        pltpu.VMEM((1, N_STATE), F32)],
        compiler_params=_cparams("parallel", "arbitrary"),
        name="s5_scan",
    )(proj, perm, perm_t, bmat, a_re, a_im, cmat, d_row, w_glu)


def _s5_step_kernel(u_ref, hre_ref, him_ref, bmat_ref, are_ref, aim_ref, cmat_ref, d_ref, wglu_ref,
                    y_ref, xre_ref, xim_ref):
    u = u_ref[...]
    ub = u.astype(BF16)
    ys = []
    for kb in range(N_KB):
        bu = jnp.dot(ub[:, kb * SSM_KB:(kb + 1) * SSM_KB], bmat_ref[kb], preferred_element_type=F32)
        n0 = kb * KB_STATES
        ar, ai = are_ref[:, n0:n0 + KB_STATES], aim_ref[:, n0:n0 + KB_STATES]
        hr, hi = hre_ref[:, n0:n0 + KB_STATES], him_ref[:, n0:n0 + KB_STATES]
        xr = ar * hr - ai * hi + bu[:, :KB_STATES]
        xi = ar * hi + ai * hr + bu[:, KB_STATES:]
        xre_ref[:, n0:n0 + KB_STATES] = xr
        xim_ref[:, n0:n0 + KB_STATES] = xi
        x = jnp.concatenate([xr, xi], axis=1).astype(BF16)
        ys.append(jnp.dot(x, cmat_ref[kb], preferred_element_type=F32))
    y_ref[...] = _s5_output(jnp.concatenate(ys, axis=1), u, d_ref, wglu_ref)


def _s5_step(proj, h_re, h_im, bmat, a_re, a_im, cmat, d_row, w_glu):
    n = proj.shape[0]
    const2 = lambda i: (0, 0)
    const3 = lambda i: (0, 0, 0)
    return pl.pallas_call(
        _s5_step_kernel,
        grid=(1,),
        in_specs=[pl.BlockSpec((n, MIX_W), const2),
                  pl.BlockSpec((n, N_STATE), const2),
                  pl.BlockSpec((n, N_STATE), const2),
                  pl.BlockSpec((N_KB, SSM_KB, 2 * KB_STATES), const3),
                  pl.BlockSpec((1, N_STATE), const2),
                  pl.BlockSpec((1, N_STATE), const2),
                  pl.BlockSpec((N_KB, 2 * KB_STATES, SSM_KB), const3),
                  pl.BlockSpec((1, MIX_W), const2),
                  pl.BlockSpec((MIX_W, MIX_W), const2)],
        out_specs=[pl.BlockSpec((n, MIX_W), const2),
                   pl.BlockSpec((n, N_STATE), const2),
                   pl.BlockSpec((n, N_STATE), const2)],
        out_shape=[jax.ShapeDtypeStruct((n, MIX_W), BF16),
                   jax.ShapeDtypeStruct((n, N_STATE), F32),
                   jax.ShapeDtypeStruct((n, N_STATE), F32)],
        compiler_params=_cparams("arbitrary"),
        name="s5_step",
    )(proj, h_re, h_im, bmat, a_re, a_im, cmat, d_row, w_glu)


def _mem_attn_kernel(q_ref, k_ref, v_ref, seg_ref, g_ref, o_ref):
    q = _seg_rms(q_ref[...], seg_ref[...], g_ref[...]) * ATTN_SCALE
    k = k_ref[0]
    v = v_ref[0]
    lane_head = lax.broadcasted_iota(jnp.int32, q.shape, 1) // HEAD_DIM
    out = jnp.zeros(q.shape, F32)
    for h in range(MEM_HEADS):
        qh = jnp.where(lane_head == h, q, 0.0).astype(BF16)
        s = _dot_nt(qh, k)
        p = jnp.exp(s - jnp.max(s, axis=-1, keepdims=True))
        o = jnp.dot(p.astype(BF16), v, preferred_element_type=F32) / jnp.sum(p, axis=-1, keepdims=True)
        out = jnp.where(lane_head == h, o, out)
    o_ref[...] = out.astype(BF16)


def _mem_attn(proj, mem_k, mem_v, seg_mat, g_row, bsz, seq, tq):
    nq = seq // tq
    tokens = mem_k.shape[1]
    qcol = MIX_W // MEM_W
    kv_spec = pl.BlockSpec((1, tokens, MEM_W), lambda b, i: (b, 0, 0))
    return pl.pallas_call(
        _mem_attn_kernel,
        grid=(bsz, nq),
        in_specs=[pl.BlockSpec((tq, MEM_W), lambda b, i: (b * nq + i, qcol)),
                  kv_spec, kv_spec,
                  pl.BlockSpec((MEM_W, MEM_W), lambda b, i: (0, 0)),
                  pl.BlockSpec((1, MEM_W), lambda b, i: (0, 0))],
        out_specs=pl.BlockSpec((tq, MEM_W), lambda b, i: (b * nq + i, 0)),
        out_shape=jax.ShapeDtypeStruct((bsz * seq, MEM_W), BF16),
        compiler_params=_cparams("parallel", "parallel"),
        name="mem_attn",
    )(proj, mem_k, mem_v, seg_mat, g_row)


def _mem_attn_rows_kernel(q_ref, k_ref, v_ref, seg_ref, g_ref, o_ref, *, rows):
    q = _seg_rms(q_ref[...], seg_ref[...], g_ref[...]) * ATTN_SCALE
    hshape = (8, MEM_W)
    own = lax.broadcasted_iota(jnp.int32, hshape, 0) == lax.broadcasted_iota(jnp.int32, hshape, 1) // HEAD_DIM
    outs = []
    for r in range(rows):
        qh = jnp.where(own, jnp.broadcast_to(q[r:r + 1], hshape), 0.0).astype(BF16)
        s = _dot_nt(qh, k_ref[0, r].astype(BF16))
        p = jnp.exp(s - jnp.max(s, axis=-1, keepdims=True))
        o = (jnp.dot(p.astype(BF16), v_ref[0, r].astype(BF16), preferred_element_type=F32)
             / jnp.sum(p, axis=-1, keepdims=True))
        outs.append(jnp.sum(jnp.where(own, o, 0.0), axis=0, keepdims=True))
    o_ref[...] = jnp.concatenate(outs, axis=0).astype(BF16)


def _mem_attn_rows(proj, mem_k, mem_v, layer, seg_mat, g_row, rows=8):
    n = proj.shape[0]
    tokens = mem_k.shape[2]
    qcol = MIX_W // MEM_W
    kv_spec = pl.BlockSpec((1, rows, tokens, MEM_W), lambda i: (layer, i, 0, 0))
    return pl.pallas_call(
        functools.partial(_mem_attn_rows_kernel, rows=rows),
        grid=(n // rows,),
        in_specs=[pl.BlockSpec((rows, MEM_W), lambda i: (i, qcol)),
                  kv_spec, kv_spec,
                  pl.BlockSpec((MEM_W, MEM_W), lambda i: (0, 0)),
                  pl.BlockSpec((1, MEM_W), lambda i: (0, 0))],
        out_specs=pl.BlockSpec((rows, MEM_W), lambda i: (i, 0)),
        out_shape=jax.ShapeDtypeStruct((n, MEM_W), BF16),
        compiler_params=_cparams("parallel"),
        name="mem_attn_rows",
    )(proj, mem_k, mem_v, seg_mat, g_row)


def _moba_prefill_kernel(q_ref, k_ref, vt_ref, kmean_ref, seg_ref, g_ref, o_ref,
                         qa_scr, m_scr, l_scr, acc_scr, *, nblk):
    i = pl.program_id(1)
    blk = MOBA_BLOCK
    qn_t = _seg_rms(q_ref[...], seg_ref[...], g_ref[...]).T
    qs_t = (qn_t * ATTN_SCALE).astype(BF16)
    kmean = kmean_ref[0]
    cols = Q_PER_KV * blk
    key_idx = lax.broadcasted_iota(jnp.int32, (blk, cols), 0)
    row_idx = lax.broadcasted_iota(jnp.int32, (blk, cols), 1) % blk
    causal = key_idx <= row_idx
    pad_t = jnp.zeros((KAUG_W - HEAD_DIM - nblk, blk), BF16)
    i0 = pl.multiple_of(i * blk, blk)

    for g in range(MOBA_KV_HEADS):
        km = kmean[:, g * HEAD_DIM:(g + 1) * HEAD_DIM]
        parts = []
        for hh in range(Q_PER_KV):
            h = g * Q_PER_KV + hh
            gate_t = jnp.dot(km, qn_t[h * HEAD_DIM:(h + 1) * HEAD_DIM], precision=lax.Precision.HIGHEST,
                             preferred_element_type=F32)
            keep_t = _topk_keep(gate_t, i, axis=0)
            past_t = lax.broadcasted_iota(jnp.int32, gate_t.shape, 0) < i
            bias_t = jnp.where(past_t & ~keep_t, MASK_VALUE, 0.0).astype(BF16)
            parts.append(jnp.concatenate([qs_t[h * HEAD_DIM:(h + 1) * HEAD_DIM], bias_t, pad_t], axis=0))
        qa_t = jnp.concatenate(parts, axis=1)
        qa_scr[g] = qa_t

        s = jnp.dot(k_ref[g, pl.ds(i0, blk), :], qa_t, preferred_element_type=F32)
        s = jnp.where(causal, s, MASK_VALUE)
        m = jnp.max(s, axis=0, keepdims=True)
        p = jnp.exp(s - m)
        m_scr[g] = m
        l_scr[g] = jnp.sum(p, axis=0, keepdims=True)
        acc_scr[g] = jnp.dot(vt_ref[g * HEAD_DIM:(g + 1) * HEAD_DIM, pl.ds(i0, blk)], p.astype(BF16),
                             preferred_element_type=F32)

    def kv_step(j, _):
        j0 = pl.multiple_of(j * blk, blk)
        for g in range(MOBA_KV_HEADS):
            s = jnp.dot(k_ref[g, pl.ds(j0, blk), :], qa_scr[g], preferred_element_type=F32)
            m_old = m_scr[g]
            m_new = jnp.maximum(m_old, jnp.max(s, axis=0, keepdims=True))
            alpha = jnp.exp(m_old - m_new)
            p = jnp.exp(s - m_new)
            l_scr[g] = alpha * l_scr[g] + jnp.sum(p, axis=0, keepdims=True)
            acc_scr[g] = alpha * acc_scr[g] + jnp.dot(
                vt_ref[g * HEAD_DIM:(g + 1) * HEAD_DIM, pl.ds(j0, blk)], p.astype(BF16),
                preferred_element_type=F32)
            m_scr[g] = m_new
        return 0

    lax.fori_loop(0, i, kv_step, 0)
    outs = []
    for g in range(MOBA_KV_HEADS):
        out_t = acc_scr[g] / l_scr[g]
        outs += [out_t[:, hh * blk:(hh + 1) * blk] for hh in range(Q_PER_KV)]
    o_ref[...] = jnp.concatenate(outs, axis=0).T.astype(BF16)


def _moba_prefill(proj, k_aug, v_t, kmean, seg_mat, g_row, bsz, seq):
    nblk = seq // MOBA_BLOCK
    cols = Q_PER_KV * MOBA_BLOCK
    return pl.pallas_call(
        functools.partial(_moba_prefill_kernel, nblk=nblk),
        grid=(bsz, nblk),
        in_specs=[pl.BlockSpec((MOBA_BLOCK, MIX_W), lambda b, i: (b * nblk + i, 0)),
                  pl.BlockSpec((MOBA_KV_HEADS, seq, KAUG_W), lambda b, i: (0, b, 0)),
                  pl.BlockSpec((KV_W, seq), lambda b, i: (0, b)),
                  pl.BlockSpec((1, nblk, KV_W), lambda b, i: (b, 0, 0)),
                  pl.BlockSpec((MIX_W, MIX_W), lambda b, i: (0, 0)),
                  pl.BlockSpec((1, MIX_W), lambda b, i: (0, 0))],
        out_specs=pl.BlockSpec((MOBA_BLOCK, MIX_W), lambda b, i: (b * nblk + i, 0)),
        out_shape=jax.ShapeDtypeStruct((bsz * seq, MIX_W), BF16),
        scratch_shapes=[pltpu.VMEM((MOBA_KV_HEADS, KAUG_W, cols), BF16),
                        pltpu.VMEM((MOBA_KV_HEADS, 1, cols), F32),
                        pltpu.VMEM((MOBA_KV_HEADS, 1, cols), F32),
                        pltpu.VMEM((MOBA_KV_HEADS, HEAD_DIM, cols), F32)],
        compiler_params=_cparams("parallel", "arbitrary"),
        name="moba_prefill",
    )(proj, k_aug, v_t, kmean, seg_mat, g_row)


def _moba_decode_kernel(pt_ref, q_ref, knew_ref, vnew_ref, *rest, n_pages, page):
    k_pages = rest[:n_pages]
    v_pages = rest[n_pages:2 * n_pages]
    seg_ref, g_ref, place_ref, unplace_ref, ind_ref, o_ref = rest[2 * n_pages:]
    del pt_ref
    past = n_pages * page
    nblk = past // MOBA_BLOCK
    pages_per_blk = MOBA_BLOCK // page
    hrows = 16

    k = jnp.concatenate([r[...] for r in k_pages], axis=0)
    v = jnp.concatenate([r[...] for r in v_pages], axis=0).astype(BF16)
    kmean = jnp.concatenate(
        [jnp.mean(k[j * MOBA_BLOCK:(j + 1) * MOBA_BLOCK], axis=0, keepdims=True) for j in range(nblk)], axis=0)
    del pages_per_blk

    qn = _seg_rms(q_ref[:, :MIX_W], seg_ref[...], g_ref[...])
    rshape = (hrows, MIX_W)
    own = lax.broadcasted_iota(jnp.int32, rshape, 0) == lax.broadcasted_iota(jnp.int32, rshape, 1) // HEAD_DIM
    qrep = jnp.where(own, jnp.broadcast_to(qn, rshape), 0.0)
    qexp = jnp.dot(qrep, place_ref[...].astype(F32), precision=lax.Precision.HIGHEST,
                   preferred_element_type=F32)

    gate = _dot_nt(qexp, kmean, precision=lax.Precision.HIGHEST)
    keep = _topk_keep(gate, nblk, axis=1)
    bias = jnp.dot(jnp.where(keep, 0.0, MASK_VALUE).astype(BF16), ind_ref[...],
                   preferred_element_type=F32)
    s = _dot_nt(qexp.astype(BF16), k.astype(BF16)) * ATTN_SCALE + bias
    s_own = jnp.sum(qexp * knew_ref[...], axis=-1, keepdims=True) * ATTN_SCALE
    m = jnp.maximum(jnp.max(s, axis=-1, keepdims=True), s_own)
    p = jnp.exp(s - m)
    p_own = jnp.exp(s_own - m)
    denom = jnp.sum(p, axis=-1, keepdims=True) + p_own
    o = (jnp.dot(p.astype(BF16), v, preferred_element_type=F32) + p_own * vnew_ref[...]) / denom
    orep = jnp.dot(o.astype(BF16), unplace_ref[...], preferred_element_type=F32)
    o_ref[...] = jnp.sum(jnp.where(own, orep, 0.0), axis=0, keepdims=True).astype(BF16)


def _moba_decode(proj, k_new, v_new, cache_k, cache_v, page_table, seg_mat, g_row, place, unplace, ind):
    n = proj.shape[0]
    page = cache_k.shape[1]
    n_pages = page_table.shape[0] // n
    past = n_pages * page
    nblk = past // MOBA_BLOCK

    def page_spec(p):
        return pl.BlockSpec((None, page, KV_W), lambda b, pt, p=p: (pt[b * n_pages + p], 0, 0))

    const2 = lambda b, pt: (0, 0)
    grid_spec = pltpu.PrefetchScalarGridSpec(
        num_scalar_prefetch=1,
        grid=(n,),
        in_specs=([pl.BlockSpec((None, 1, D_MODEL), lambda b, pt: (b, 0, 0)),
                   pl.BlockSpec((None, 1, KV_W), lambda b, pt: (b, 0, 0)),
                   pl.BlockSpec((None, 1, KV_W), lambda b, pt: (b, 0, 0))]
                  + [page_spec(p) for p in range(n_pages)] * 1
                  + [page_spec(p) for p in range(n_pages)]
                  + [pl.BlockSpec((MIX_W, MIX_W), const2),
                     pl.BlockSpec((1, MIX_W), const2),
                     pl.BlockSpec((MIX_W, KV_W), const2),
                     pl.BlockSpec((KV_W, MIX_W), const2),
                     pl.BlockSpec((nblk, past), const2)]),
        out_specs=pl.BlockSpec((None, 1, MIX_W), lambda b, pt: (b, 0, 0)),
    )
    return pl.pallas_call(
        functools.partial(_moba_decode_kernel, n_pages=n_pages, page=page),
        grid_spec=grid_spec,
        out_shape=jax.ShapeDtypeStruct((n, 1, MIX_W), BF16),
        compiler_params=_cparams("parallel"),
        name="moba_decode",
    )(page_table, proj, k_new, v_new, *([cache_k] * n_pages), *([cache_v] * n_pages),
      seg_mat, g_row, place, unplace, ind)


def _seg_matrix(width):
    head = jnp.arange(width) // HEAD_DIM
    return (head[:, None] == head[None, :]).astype(BF16) / HEAD_DIM


def _scan_perm():
    r = jnp.arange(SCAN_T)
    t = (r % SCAN_SUB) * SCAN_STEPS + r // SCAN_SUB
    return (t[:, None] == jnp.arange(SCAN_T)[None, :]).astype(BF16)


def _block_diag(x, row_axes):
    eye = jnp.eye(x.shape[1], dtype=x.dtype)
    if row_axes == "gh":
        full = jnp.einsum("kghp,gG->kghGp", x, eye)
    else:
        full = jnp.einsum("kghp,gG->kgpGh", x, eye)
    k, g, a, _, b = full.shape
    return full.reshape(k, g * a, g * b)


def _s5_constants(p, l, ab_re, ab_im, bb_re, bb_im):
    gpb = SSM_KB // SSM_GROUP
    shape4 = (N_KB, gpb, SSM_GROUP, SSM_STATE)
    bmat = jnp.concatenate([_block_diag(bb_re[l].reshape(shape4), "gh"),
                            _block_diag(bb_im[l].reshape(shape4), "gh")], axis=-1).astype(BF16)
    c_re = p["ssm_c_re"][l].reshape(shape4)
    c_im = p["ssm_c_im"][l].reshape(shape4)
    cmat = jnp.concatenate([_block_diag(c_re, "gp"), -_block_diag(c_im, "gp")], axis=1).astype(BF16)
    a_re = ab_re[l, ::SSM_GROUP].reshape(1, N_STATE)
    a_im = ab_im[l, ::SSM_GROUP].reshape(1, N_STATE)
    d_row = p["ssm_d"][l].reshape(1, MIX_W)
    return bmat, a_re, a_im, cmat, d_row, p["w_glu"][l].astype(BF16)


def kernel(x_prompt, x_sample, mem_prompt, state_ssm_re, state_ssm_im, cache_k, cache_v, page_table,
           cache_mem_k, cache_mem_v, g_mix, w_in, w_out, g_ffn, w_gu, w_down, ssm_a_re, ssm_a_im,
           ssm_log_dt, ssm_b_re, ssm_b_im, ssm_c_re, ssm_c_im, ssm_d, w_glu, g_q, g_mq, g_mem,
           w_mem_kv, g_mk, g_kv, w_kv, g_k):
    bsz, seq, _ = x_prompt.shape
    n_p = bsz * seq
    n_s = x_sample.shape[0]
    tokens = mem_prompt.shape[1]
    p = dict(ssm_c_re=ssm_c_re, ssm_c_im=ssm_c_im, ssm_d=ssm_d, w_glu=w_glu)

    w_in_b = w_in.astype(BF16)
    w_out_b = w_out.astype(BF16)
    w_gu_b = w_gu.astype(BF16)
    w_down_b = w_down.astype(BF16)
    seg_kv = _seg_matrix(KV_W)
    seg_mix = _seg_matrix(MIX_W)
    gq_rows = jnp.tile(g_q, (1, MOBA_HEADS))
    gmq_rows = jnp.tile(g_mq, (1, MEM_HEADS))

    rep = lambda a: jnp.repeat(a, SSM_GROUP, axis=1)
    b_rows = lambda b: jnp.swapaxes(b, 2, 3).reshape(N_A_LAYERS, MIX_W, SSM_STATE)
    ab_re, ab_im, bb_re, bb_im = _ssm_prep(rep(ssm_a_re), rep(ssm_a_im), rep(ssm_log_dt)[..., None],
                                           b_rows(ssm_b_re), b_rows(ssm_b_im))
    s5 = [_s5_constants(p, l, ab_re, ab_im, bb_re, bb_im) for l in range(N_A_LAYERS)]
    perm = _scan_perm()
    perm_t = perm.T

    mem_flat = mem_prompt.reshape(bsz * tokens, D_MODEL)
    pm_k, pm_v = _kv_proj(mem_flat, g_mem[:, None, :], w_mem_kv.astype(BF16), seg_kv,
                          jnp.tile(g_mk, (1, MEM_HEADS))[:, None, :], tm=256)
    p_mem_k = pm_k.reshape(DEPTH, bsz, tokens, MEM_HEADS, HEAD_DIM)
    p_mem_v = pm_v.reshape(DEPTH, bsz, tokens, MEM_HEADS, HEAD_DIM)
    pm_k_b = pm_k.reshape(DEPTH, bsz, tokens, MEM_W).astype(BF16)
    pm_v_b = pm_v.reshape(DEPTH, bsz, tokens, MEM_W).astype(BF16)

    g_kv_rows = g_kv.reshape(1, 1, D_MODEL)
    w_kv_b = w_kv.astype(BF16)[None]
    g_k_rows = jnp.tile(g_k, MOBA_KV_HEADS).reshape(1, 1, KV_W)

    h = x_prompt.reshape(n_p, D_MODEL)
    p_fin_re, p_fin_im = [], []
    k_aug = v_t = kmean = p_k = p_v = None
    nblk = seq // MOBA_BLOCK
    for l in range(DEPTH):
        proj = _norm_matmul(h, g_mix[l][None], w_in_b[l], tm=512)
        if l < N_A_LAYERS:
            mix, f_re, f_im = _s5_scan(proj, bsz, seq, perm, perm_t, *s5[l])
            p_fin_re.append(f_re.reshape(bsz, SSM_GROUPS, SSM_STATE))
            p_fin_im.append(f_im.reshape(bsz, SSM_GROUPS, SSM_STATE))
        else:
            mix = _moba_prefill(proj, k_aug, v_t, kmean, seg_mix, gq_rows[l - N_A_LAYERS][None], bsz, seq)
        mem = _mem_attn(proj, pm_k_b[l], pm_v_b[l], seg_kv, gmq_rows[l][None], bsz, seq, tq=512)
        h = _out_proj(h, mix, mem, w_out_b[l, :MIX_W], w_out_b[l, MIX_W:], tm=512)
        h = _ffn(h, g_ffn[l][None], w_gu_b[l], w_down_b[l], tm=512, fc=D_FF // 2)
        if l == N_A_LAYERS - 1:
            k_new, v_new, k_aug, v_t, kmean = _kv_proj(h, g_kv_rows, w_kv_b, seg_kv, g_k_rows, tm=MOBA_BLOCK,
                                                       blocks_per_seq=nblk)
            p_k = k_new.reshape(bsz, seq, MOBA_KV_HEADS, HEAD_DIM)
            p_v = v_new.reshape(bsz, seq, MOBA_KV_HEADS, HEAD_DIM)
            kmean = kmean.reshape(bsz, nblk, KV_W)
    y_prompt = h.reshape(bsz, seq, D_MODEL)

    n_pages = page_table.shape[1]
    page = cache_k.shape[1]
    past = n_pages * page
    ck = cache_k.reshape(cache_k.shape[0], page, KV_W)
    cv = cache_v.reshape(cache_v.shape[0], page, KV_W)
    cmk = cache_mem_k.reshape(DEPTH, n_s, tokens, MEM_W)
    cmv = cache_mem_v.reshape(DEPTH, n_s, tokens, MEM_W)
    kv_of_lane = (jnp.arange(MIX_W) // HEAD_DIM) // Q_PER_KV * HEAD_DIM + jnp.arange(MIX_W) % HEAD_DIM
    place = (kv_of_lane[:, None] == jnp.arange(KV_W)[None, :]).astype(BF16)
    ind = (jnp.arange(past)[None, :] // MOBA_BLOCK == jnp.arange(past // MOBA_BLOCK)[:, None]).astype(BF16)

    h = x_sample.reshape(n_s, D_MODEL)
    s_fin_re, s_fin_im = [], []
    s_k = s_v = None
    for l in range(DEPTH):
        proj = _norm_matmul(h, g_mix[l][None], w_in_b[l], tm=n_s)
        if l < N_A_LAYERS:
            mix, x_re, x_im = _s5_step(proj, state_ssm_re[l].reshape(n_s, N_STATE),
                                       state_ssm_im[l].reshape(n_s, N_STATE), *s5[l])
            s_fin_re.append(x_re.reshape(n_s, SSM_GROUPS, SSM_STATE))
            s_fin_im.append(x_im.reshape(n_s, SSM_GROUPS, SSM_STATE))
        else:
            mix = _moba_decode(proj.reshape(n_s, 1, D_MODEL), s_k.reshape(n_s, 1, KV_W),
                               s_v.reshape(n_s, 1, KV_W), ck, cv, page_table.reshape(-1), seg_mix,
                               gq_rows[l - N_A_LAYERS][None], place, place.T, ind).reshape(n_s, MIX_W)
        mem = _mem_attn_rows(proj, cmk, cmv, l, seg_kv, gmq_rows[l][None])
        h = _out_proj(h, mix, mem, w_out_b[l, :MIX_W], w_out_b[l, MIX_W:], tm=n_s)
        h = _ffn(h, g_ffn[l][None], w_gu_b[l], w_down_b[l], tm=n_s, fc=D_FF // 2)
        if l == N_A_LAYERS - 1:
            s_k, s_v = _kv_proj(h, g_kv_rows, w_kv_b, seg_kv, g_k_rows, tm=n_s)
    y_sample = h.reshape(n_s, 1, D_MODEL)

    return (y_prompt, y_sample,
            jnp.stack(p_fin_re), jnp.stack(p_fin_im), p_k, p_v, p_mem_k, p_mem_v,
            jnp.stack(s_fin_re), jnp.stack(s_fin_im),
            s_k.reshape(n_s, 1, MOBA_KV_HEADS, HEAD_DIM), s_v.reshape(n_s, 1, MOBA_KV_HEADS, HEAD_DIM))
```

```python
import functools
import math

import jax
import jax.numpy as jnp
from jax import lax
from jax.experimental import pallas as pl
from jax.experimental.pallas import tpu as pltpu

F32 = jnp.float32
BF16 = jnp.bfloat16

D_MODEL = 1024
DEPTH = 4
N_A_LAYERS = DEPTH // 2
HEAD_DIM = 64
MIX_W = 3 * D_MODEL // 4
MEM_W = D_MODEL // 4
MEM_HEADS = MEM_W // HEAD_DIM
SSM_GROUP = 16
SSM_GROUPS = MIX_W // SSM_GROUP
SSM_STATE = 64
N_STATE = SSM_GROUPS * SSM_STATE
MOBA_HEADS = MIX_W // HEAD_DIM
MOBA_KV_HEADS = 4
Q_PER_KV = MOBA_HEADS // MOBA_KV_HEADS
KV_W = MOBA_KV_HEADS * HEAD_DIM
MOBA_BLOCK = 256
MOBA_TOPK = 3
D_FF = ((-(-8 * D_MODEL // 3) + 255) // 256) * 256
EPS = 1e-6
ATTN_SCALE = HEAD_DIM ** -0.5
MASK_VALUE = -1e30
KAUG_W = 128
VAUG_H = HEAD_DIM + 16
LOG2_E = math.log2(math.e)

SCAN_SUB = 8
SCAN_STEPS = 32
SCAN_T = SCAN_SUB * SCAN_STEPS
SSM_KB = 256
N_KB = MIX_W // SSM_KB
KB_STATES = (SSM_KB // SSM_GROUP) * SSM_STATE
SCAN_CW = 512

VMEM_LIMIT_BYTES = 56 * 1024 * 1024


def _cparams(*sem):
    return pltpu.CompilerParams(dimension_semantics=sem, vmem_limit_bytes=VMEM_LIMIT_BYTES)


def _rms(x, g_row):
    ms = jnp.mean(x * x, axis=-1, keepdims=True)
    return x * lax.rsqrt(ms + EPS) * g_row


def _seg_rms(x, seg_mat, g_row):
    sq = x * x
    hi = sq.astype(BF16)
    lo = (sq - hi.astype(F32)).astype(BF16)
    ms = (jnp.dot(hi, seg_mat, preferred_element_type=F32)
          + jnp.dot(lo, seg_mat, preferred_element_type=F32))
    return x * lax.rsqrt(ms + EPS) * g_row


def _dot_nt(a, b, precision=None):
    return lax.dot_general(a, b, (((1,), (1,)), ((), ())), precision=precision,
                           preferred_element_type=F32)


def _topk_keep(gate, n_valid, axis):
    n = gate.shape[axis]
    idx = lax.broadcasted_iota(jnp.int32, gate.shape, axis)
    rank = jnp.zeros(gate.shape, jnp.int32)
    for j in range(n):
        gj = lax.slice_in_dim(gate, j, j + 1, axis=axis)
        beats = (gj > gate) | ((gj == gate) & (j < idx))
        rank = rank + jnp.where(beats, jnp.where(j < n_valid, 1, 0), 0)
    return (rank < MOBA_TOPK) & (idx < n_valid)


def _norm_matmul_kernel(x_ref, g_ref, w_ref, o_ref):
    a = _rms(x_ref[...], g_ref[...]).astype(BF16)
    o_ref[...] = jnp.dot(a, w_ref[...], preferred_element_type=F32)


def _norm_matmul(x, g_row, w, tm):
    n, d = x.shape
    nout = w.shape[1]
    return pl.pallas_call(
        _norm_matmul_kernel,
        grid=(n // tm,),
        in_specs=[pl.BlockSpec((tm, d), lambda i: (i, 0)),
                  pl.BlockSpec((1, d), lambda i: (0, 0)),
                  pl.BlockSpec((d, nout), lambda i: (0, 0))],
        out_specs=pl.BlockSpec((tm, nout), lambda i: (i, 0)),
        out_shape=jax.ShapeDtypeStruct((n, nout), F32),
        compiler_params=_cparams("parallel"),
        name="norm_matmul",
    )(x, g_row, w)


def _kv_kernel(x_ref, g_ref, w_ref, seg_ref, gk_ref, k_ref, v_ref, *extra, half, blocks_per_seq):
    a = _rms(x_ref[...], g_ref[0]).astype(BF16)
    kv = jnp.dot(a, w_ref[0], preferred_element_type=F32)
    k = _seg_rms(kv[:, :half], seg_ref[...], gk_ref[0])
    v = kv[:, half:]
    k_ref[0] = k
    v_ref[0] = v
    if blocks_per_seq:
        kaug_ref, vt_ref, kmean_ref = extra
        lane = lax.broadcasted_iota(jnp.int32, (k.shape[0], KAUG_W), 1)
        tail = jnp.where(lane == HEAD_DIM + pl.program_id(1) % blocks_per_seq, 1.0, 0.0)
        for g in range(MOBA_KV_HEADS):
            col = k[:, (g // 2) * KAUG_W:(g // 2 + 1) * KAUG_W]
            if g % 2:
                col = pltpu.roll(col, HEAD_DIM, axis=1)
            kaug_ref[g] = jnp.where(lane < HEAD_DIM, col, tail).astype(BF16)
        v_t = v.T
        ones_row = jnp.where(lax.broadcasted_iota(jnp.int32, (VAUG_H - HEAD_DIM, k.shape[0]), 0) == 0, 1.0, 0.0)
        vt_ref[...] = jnp.concatenate(
            [piece for g in range(MOBA_KV_HEADS) for piece in (v_t[g * HEAD_DIM:(g + 1) * HEAD_DIM], ones_row)],
            axis=0).astype(BF16)
        kmean_ref[0] = jnp.mean(k, axis=0, keepdims=True)


def _kv_proj(x, g_rows, w, seg_mat, gk_rows, tm, blocks_per_seq=0):
    n, d = x.shape
    layers, _, two_half = w.shape
    half = two_half // 2
    out_shape = [jax.ShapeDtypeStruct((layers, n, half), F32)] * 2
    out_specs = [pl.BlockSpec((1, tm, half), lambda l, i: (l, i, 0))] * 2
    if blocks_per_seq:
        assert layers == 1 and tm == MOBA_BLOCK and half == KV_W
        out_shape += [jax.ShapeDtypeStruct((MOBA_KV_HEADS, n, KAUG_W), BF16),
                      jax.ShapeDtypeStruct((MOBA_KV_HEADS * VAUG_H, n), BF16),
                      jax.ShapeDtypeStruct((n // tm, 1, half), F32)]
        out_specs += [pl.BlockSpec((MOBA_KV_HEADS, tm, KAUG_W), lambda l, i: (0, i, 0)),
                      pl.BlockSpec((MOBA_KV_HEADS * VAUG_H, tm), lambda l, i: (0, i)),
                      pl.BlockSpec((1, 1, half), lambda l, i: (i, 0, 0))]
    return pl.pallas_call(
        functools.partial(_kv_kernel, half=half, blocks_per_seq=blocks_per_seq),
        grid=(layers, n // tm),
        in_specs=[pl.BlockSpec((tm, d), lambda l, i: (i, 0)),
                  pl.BlockSpec((1, 1, d), lambda l, i: (l, 0, 0)),
                  pl.BlockSpec((1, d, two_half), lambda l, i: (l, 0, 0)),
                  pl.BlockSpec((half, half), lambda l, i: (0, 0)),
                  pl.BlockSpec((1, 1, half), lambda l, i: (l, 0, 0))],
        out_specs=out_specs,
        out_shape=out_shape,
        compiler_params=_cparams("parallel", "parallel"),
        name="kv_proj",
    )(x, g_rows, w, seg_mat, gk_rows)


def _out_proj_kernel(h_ref, mix_ref, mem_ref, w1_ref, w2_ref, o_ref):
    o_ref[...] = (h_ref[...]
                  + jnp.dot(mix_ref[...], w1_ref[...], preferred_element_type=F32)
                  + jnp.dot(mem_ref[...], w2_ref[...], preferred_element_type=F32))


def _out_proj(h, mix, mem, w_mix, w_mem, tm):
    n, d = h.shape
    return pl.pallas_call(
        _out_proj_kernel,
        grid=(n // tm,),
        in_specs=[pl.BlockSpec((tm, d), lambda i: (i, 0)),
                  pl.BlockSpec((tm, MIX_W), lambda i: (i, 0)),
                  pl.BlockSpec((tm, MEM_W), lambda i: (i, 0)),
                  pl.BlockSpec((MIX_W, d), lambda i: (0, 0)),
                  pl.BlockSpec((MEM_W, d), lambda i: (0, 0))],
        out_specs=pl.BlockSpec((tm, d), lambda i: (i, 0)),
        out_shape=jax.ShapeDtypeStruct((n, d), F32),
        compiler_params=_cparams("parallel"),
        name="out_proj",
    )(h, mix, mem, w_mix, w_mem)


def _ffn_kernel(h_ref, g_ref, wg_ref, wu_ref, wd_ref, o_ref, a_scr, acc_scr):
    j = pl.program_id(1)

    @pl.when(j == 0)
    def _():
        a_scr[...] = _rms(h_ref[...], g_ref[...]).astype(BF16)
        acc_scr[...] = jnp.zeros_like(acc_scr)

    a = a_scr[...]
    gate = jnp.dot(a, wg_ref[...], preferred_element_type=F32)
    up = jnp.dot(a, wu_ref[...], preferred_element_type=F32)
    act = (gate * jax.nn.sigmoid(gate) * up).astype(BF16)
    acc_scr[...] += jnp.dot(act, wd_ref[...], preferred_element_type=F32)

    @pl.when(j == pl.num_programs(1) - 1)
    def _():
        o_ref[...] = h_ref[...] + acc_scr[...]


def _ffn(h, g_row, w_gu, w_down, tm, fc):
    n, d = h.shape
    nchunk = D_FF // fc
    return pl.pallas_call(
        _ffn_kernel,
        grid=(n // tm, nchunk),
        in_specs=[pl.BlockSpec((tm, d), lambda i, j: (i, 0)),
                  pl.BlockSpec((1, d), lambda i, j: (0, 0)),
                  pl.BlockSpec((d, fc), lambda i, j: (0, j)),
                  pl.BlockSpec((d, fc), lambda i, j: (0, nchunk + j)),
                  pl.BlockSpec((fc, d), lambda i, j: (j, 0))],
        out_specs=pl.BlockSpec((tm, d), lambda i, j: (i, 0)),
        out_shape=jax.ShapeDtypeStruct((n, d), F32),
        scratch_shapes=[pltpu.VMEM((tm, d), BF16), pltpu.VMEM((tm, d), F32)],
        compiler_params=_cparams("parallel", "arbitrary"),
        name="ffn",
    )(h, g_row, w_gu, w_gu, w_down)


def _ssm_prep_kernel(are_ref, aim_ref, ldt_ref, bre_ref, bim_ref, abr_ref, abi_ref, bbr_ref, bbi_ref):
    dt = jnp.exp(ldt_ref[0])
    lam_re = jnp.minimum(are_ref[0], -1e-4)
    lam_im = aim_ref[0]
    mag = jnp.exp(dt * lam_re)
    ang = dt * lam_im
    ab_re = mag * jnp.cos(ang)
    ab_im = mag * jnp.sin(ang)
    den = lam_re * lam_re + lam_im * lam_im
    num_re = ab_re - 1.0
    f_re = (num_re * lam_re + ab_im * lam_im) / den
    f_im = (ab_im * lam_re - num_re * lam_im) / den
    b_re = bre_ref[0]
    b_im = bim_ref[0]
    abr_ref[0] = ab_re
    abi_ref[0] = ab_im
    bbr_ref[0] = f_re * b_re - f_im * b_im
    bbi_ref[0] = f_re * b_im + f_im * b_re


def _ssm_prep(a_re, a_im, log_dt, b_re, b_im):
    layers = a_re.shape[0]
    big = pl.BlockSpec((1, MIX_W, SSM_STATE), lambda l: (l, 0, 0))
    return pl.pallas_call(
        _ssm_prep_kernel,
        grid=(layers,),
        in_specs=[big, big, pl.BlockSpec((1, MIX_W, 1), lambda l: (l, 0, 0)), big, big],
        out_specs=[big] * 4,
        out_shape=[jax.ShapeDtypeStruct((layers, MIX_W, SSM_STATE), F32)] * 4,
        compiler_params=_cparams("parallel"),
        name="ssm_prep",
    )(a_re, a_im, log_dt, b_re, b_im)


def _re_cols(n0):
    kb, off = divmod(n0, KB_STATES)
    return kb * 2 * KB_STATES + off


def _s5_output(y_state, u, d_ref, wglu_ref):
    y = jax.nn.gelu(y_state + d_ref[...] * u)
    z = jnp.dot(y.astype(BF16), wglu_ref[...], preferred_element_type=F32)
    return (y * jax.nn.sigmoid(z)).astype(BF16)


def _s5_scan_kernel(u_ref, perm_ref, permt_ref, bmat_ref, are_ref, aim_ref, cmat_ref, d_ref, wglu_ref,
                    y_ref, fre_ref, fim_ref, x_scr, apr_scr, api_scr, str_scr, sti_scr):
    c = pl.program_id(1)

    @pl.when(c == 0)
    def _():
        str_scr[...] = jnp.zeros_like(str_scr)
        sti_scr[...] = jnp.zeros_like(sti_scr)
        ar, ai = are_ref[...], aim_ref[...]
        pr, pi = ar, ai
        for i in range(SCAN_STEPS):
            apr_scr[i:i + 1, :] = pr
            api_scr[i:i + 1, :] = pi
            pr, pi = pr * ar - pi * ai, pr * ai + pi * ar

    u = u_ref[...]
    up = jnp.dot(perm_ref[...], u.astype(BF16), preferred_element_type=F32).astype(BF16)
    for kb in range(N_KB):
        x_scr[:, kb * 2 * KB_STATES:(kb + 1) * 2 * KB_STATES] = jnp.dot(
            up[:, kb * SSM_KB:(kb + 1) * SSM_KB], bmat_ref[kb], preferred_element_type=F32)

    sub = lax.broadcasted_iota(jnp.int32, (SCAN_SUB, SCAN_CW), 0)
    for n0 in range(0, N_STATE, SCAN_CW):
        rc = _re_cols(n0)
        ic = rc + KB_STATES
        ar = jnp.broadcast_to(are_ref[:, n0:n0 + SCAN_CW], (SCAN_SUB, SCAN_CW))
        ai = jnp.broadcast_to(aim_ref[:, n0:n0 + SCAN_CW], (SCAN_SUB, SCAN_CW))

        def local_step(i, carry, rc=rc, ic=ic, ar=ar, ai=ai):
            xr, xi = carry
            r0 = pl.multiple_of(i * SCAN_SUB, SCAN_SUB)
            nxr = ar * xr - ai * xi + x_scr[pl.ds(r0, SCAN_SUB), rc:rc + SCAN_CW]
            nxi = ar * xi + ai * xr + x_scr[pl.ds(r0, SCAN_SUB), ic:ic + SCAN_CW]
            x_scr[pl.ds(r0, SCAN_SUB), rc:rc + SCAN_CW] = nxr
            x_scr[pl.ds(r0, SCAN_SUB), ic:ic + SCAN_CW] = nxi
            return nxr, nxi

        zero = jnp.zeros((SCAN_SUB, SCAN_CW), F32)
        fin_r, fin_i = lax.fori_loop(0, SCAN_STEPS, local_step, (zero, zero), unroll=4)

        a_t_r = apr_scr[SCAN_STEPS - 1:SCAN_STEPS, n0:n0 + SCAN_CW]
        a_t_i = api_scr[SCAN_STEPS - 1:SCAN_STEPS, n0:n0 + SCAN_CW]
        cr = str_scr[:, n0:n0 + SCAN_CW]
        ci = sti_scr[:, n0:n0 + SCAN_CW]
        car_r = jnp.zeros((SCAN_SUB, SCAN_CW), F32)
        car_i = jnp.zeros((SCAN_SUB, SCAN_CW), F32)
        for j in range(SCAN_SUB):
            car_r = jnp.where(sub == j, cr, car_r)
            car_i = jnp.where(sub == j, ci, car_i)
            cr, ci = (fin_r[j:j + 1] + a_t_r * cr - a_t_i * ci,
                      fin_i[j:j + 1] + a_t_r * ci + a_t_i * cr)
        str_scr[:, n0:n0 + SCAN_CW] = cr
        sti_scr[:, n0:n0 + SCAN_CW] = ci

        def fix_step(i, _, rc=rc, ic=ic, n0=n0, car_r=car_r, car_i=car_i):
            r0 = pl.multiple_of(i * SCAN_SUB, SCAN_SUB)
            pr = apr_scr[pl.ds(i, 1), n0:n0 + SCAN_CW]
            pi = api_scr[pl.ds(i, 1), n0:n0 + SCAN_CW]
            xr = x_scr[pl.ds(r0, SCAN_SUB), rc:rc + SCAN_CW]
            xi = x_scr[pl.ds(r0, SCAN_SUB), ic:ic + SCAN_CW]
            x_scr[pl.ds(r0, SCAN_SUB), rc:rc + SCAN_CW] = xr + (pr * car_r - pi * car_i)
            x_scr[pl.ds(r0, SCAN_SUB), ic:ic + SCAN_CW] = xi + (pr * car_i + pi * car_r)
            return 0

        lax.fori_loop(0, SCAN_STEPS, fix_step, 0, unroll=4)

    ys = [jnp.dot(x_scr[:, kb * 2 * KB_STATES:(kb + 1) * 2 * KB_STATES].astype(BF16), cmat_ref[kb],
                  preferred_element_type=F32) for kb in range(N_KB)]
    yp = jnp.concatenate(ys, axis=1)
    hi = yp.astype(BF16)
    lo = (yp - hi.astype(F32)).astype(BF16)
    y_state = (jnp.dot(permt_ref[...], hi, preferred_element_type=F32)
               + jnp.dot(permt_ref[...], lo, preferred_element_type=F32))
    y_ref[...] = _s5_output(y_state, u, d_ref, wglu_ref)
    fre_ref[0] = str_scr[...]
    fim_ref[0] = sti_scr[...]


def _s5_scan(proj, bsz, seq, perm, perm_t, bmat, a_re, a_im, cmat, d_row, w_glu):
    nchunk = seq // SCAN_T
    const2 = lambda b, c: (0, 0)
    const3 = lambda b, c: (0, 0, 0)
    state_spec = pl.BlockSpec((1, 1, N_STATE), lambda b, c: (b, 0, 0))
    return pl.pallas_call(
        _s5_scan_kernel,
        grid=(bsz, nchunk),
        in_specs=[pl.BlockSpec((SCAN_T, MIX_W), lambda b, c: (b * nchunk + c, 0)),
                  pl.BlockSpec((SCAN_T, SCAN_T), const2),
                  pl.BlockSpec((SCAN_T, SCAN_T), const2),
                  pl.BlockSpec((N_KB, SSM_KB, 2 * KB_STATES), const3),
                  pl.BlockSpec((1, N_STATE), const2),
                  pl.BlockSpec((1, N_STATE), const2),
                  pl.BlockSpec((N_KB, 2 * KB_STATES, SSM_KB), const3),
                  pl.BlockSpec((1, MIX_W), const2),
                  pl.BlockSpec((MIX_W, MIX_W), const2)],
        out_specs=[pl.BlockSpec((SCAN_T, MIX_W), lambda b, c: (b * nchunk + c, 0)), state_spec, state_spec],
        out_shape=[jax.ShapeDtypeStruct((bsz * seq, MIX_W), BF16),
                   jax.ShapeDtypeStruct((bsz, 1, N_STATE), F32),
                   jax.ShapeDtypeStruct((bsz, 1, N_STATE), F32)],
        scratch_shapes=[pltpu.VMEM((SCAN_T, 2 * N_STATE), F32),
                        pltpu.VMEM((SCAN_STEPS, N_STATE), F32),
                        pltpu.VMEM((SCAN_STEPS, N_STATE), F32),
                        pltpu.VMEM((1, N_STATE), F32),
                        pltpu.VMEM((1, N_STATE), F32)],
        compiler_params=_cparams("parallel", "arbitrary"),
        name="s5_scan",
    )(proj, perm, perm_t, bmat, a_re, a_im, cmat, d_row, w_glu)


def _s5_step_kernel(u_ref, hre_ref, him_ref, bmat_ref, are_ref, aim_ref, cmat_ref, d_ref, wglu_ref,
                    y_ref, xre_ref, xim_ref):
    u = u_ref[...]
    ub = u.astype(BF16)
    ys = []
    for kb in range(N_KB):
        bu = jnp.dot(ub[:, kb * SSM_KB:(kb + 1) * SSM_KB], bmat_ref[kb], preferred_element_type=F32)
        n0 = kb * KB_STATES
        ar, ai = are_ref[:, n0:n0 + KB_STATES], aim_ref[:, n0:n0 + KB_STATES]
        hr, hi = hre_ref[:, n0:n0 + KB_STATES], him_ref[:, n0:n0 + KB_STATES]
        xr = ar * hr - ai * hi + bu[:, :KB_STATES]
        xi = ar * hi + ai * hr + bu[:, KB_STATES:]
        xre_ref[:, n0:n0 + KB_STATES] = xr
        xim_ref[:, n0:n0 + KB_STATES] = xi
        x = jnp.concatenate([xr, xi], axis=1).astype(BF16)
        ys.append(jnp.dot(x, cmat_ref[kb], preferred_element_type=F32))
    y_ref[...] = _s5_output(jnp.concatenate(ys, axis=1), u, d_ref, wglu_ref)


def _s5_step(proj, h_re, h_im, bmat, a_re, a_im, cmat, d_row, w_glu):
    n = proj.shape[0]
    const2 = lambda i: (0, 0)
    const3 = lambda i: (0, 0, 0)
    return pl.pallas_call(
        _s5_step_kernel,
        grid=(1,),
        in_specs=[pl.BlockSpec((n, MIX_W), const2),
                  pl.BlockSpec((n, N_STATE), const2),
                  pl.BlockSpec((n, N_STATE), const2),
                  pl.BlockSpec((N_KB, SSM_KB, 2 * KB_STATES), const3),
                  pl.BlockSpec((1, N_STATE), const2),
                  pl.BlockSpec((1, N_STATE), const2),
                  pl.BlockSpec((N_KB, 2 * KB_STATES, SSM_KB), const3),
                  pl.BlockSpec((1, MIX_W), const2),
                  pl.BlockSpec((MIX_W, MIX_W), const2)],
        out_specs=[pl.BlockSpec((n, MIX_W), const2),
                   pl.BlockSpec((n, N_STATE), const2),
                   pl.BlockSpec((n, N_STATE), const2)],
        out_shape=[jax.ShapeDtypeStruct((n, MIX_W), BF16),
                   jax.ShapeDtypeStruct((n, N_STATE), F32),
                   jax.ShapeDtypeStruct((n, N_STATE), F32)],
        compiler_params=_cparams("arbitrary"),
        name="s5_step",
    )(proj, h_re, h_im, bmat, a_re, a_im, cmat, d_row, w_glu)


def _mem_attn_kernel(q_ref, k_ref, v_ref, seg_ref, g_ref, o_ref):
    q = _seg_rms(q_ref[...], seg_ref[...], g_ref[...]) * ATTN_SCALE
    k = k_ref[0]
    v = v_ref[0]
    lane_head = lax.broadcasted_iota(jnp.int32, q.shape, 1) // HEAD_DIM
    out = jnp.zeros(q.shape, F32)
    for h in range(MEM_HEADS):
        qh = jnp.where(lane_head == h, q, 0.0).astype(BF16)
        s = _dot_nt(qh, k)
        p = jnp.exp(s - jnp.max(s, axis=-1, keepdims=True))
        o = jnp.dot(p.astype(BF16), v, preferred_element_type=F32) / jnp.sum(p, axis=-1, keepdims=True)
        out = jnp.where(lane_head == h, o, out)
    o_ref[...] = out.astype(BF16)


def _mem_attn(proj, mem_k, mem_v, seg_mat, g_row, bsz, seq, tq):
    nq = seq // tq
    tokens = mem_k.shape[1]
    qcol = MIX_W // MEM_W
    kv_spec = pl.BlockSpec((1, tokens, MEM_W), lambda b, i: (b, 0, 0))
    return pl.pallas_call(
        _mem_attn_kernel,
        grid=(bsz, nq),
        in_specs=[pl.BlockSpec((tq, MEM_W), lambda b, i: (b * nq + i, qcol)),
                  kv_spec, kv_spec,
                  pl.BlockSpec((MEM_W, MEM_W), lambda b, i: (0, 0)),
                  pl.BlockSpec((1, MEM_W), lambda b, i: (0, 0))],
        out_specs=pl.BlockSpec((tq, MEM_W), lambda b, i: (b * nq + i, 0)),
        out_shape=jax.ShapeDtypeStruct((bsz * seq, MEM_W), BF16),
        compiler_params=_cparams("parallel", "parallel"),
        name="mem_attn",
    )(proj, mem_k, mem_v, seg_mat, g_row)


def _decode_prep_kernel(proj_ref, segk_ref, gmq_ref, *rest, with_moba):
    x = proj_ref[...]
    if with_moba:
        segm_ref, gq_ref, knew_ref, vnew_ref, mqt_ref, qt_ref, knt_ref, vnt_ref = rest
        qt_ref[...] = _seg_rms(x[:, :MIX_W], segm_ref[...], gq_ref[...]).T
        knt_ref[...] = knew_ref[...].T
        vnt_ref[...] = vnew_ref[...].T
    else:
        mqt_ref, = rest
    mqt_ref[...] = (_seg_rms(x[:, MIX_W:], segk_ref[...], gmq_ref[...]) * ATTN_SCALE).T


def _decode_prep(proj, seg_kv, gmq_row, moba=None):
    n, d = proj.shape
    const2 = lambda i: (0, 0)
    args = [proj, seg_kv, gmq_row]
    in_specs = [pl.BlockSpec((n, d), const2), pl.BlockSpec((MEM_W, MEM_W), const2), pl.BlockSpec((1, MEM_W), const2)]
    out_shape = [jax.ShapeDtypeStruct((MEM_W, n), F32)]
    if moba is not None:
        args += list(moba)
        in_specs += [pl.BlockSpec((MIX_W, MIX_W), const2), pl.BlockSpec((1, MIX_W), const2),
                     pl.BlockSpec((n, KV_W), const2), pl.BlockSpec((n, KV_W), const2)]
        out_shape += [jax.ShapeDtypeStruct((MIX_W, n), F32), jax.ShapeDtypeStruct((KV_W, n), F32),
                      jax.ShapeDtypeStruct((KV_W, n), F32)]
    return pl.pallas_call(
        functools.partial(_decode_prep_kernel, with_moba=moba is not None),
        grid=(1,),
        in_specs=in_specs,
        out_specs=[pl.BlockSpec(s.shape, const2) for s in out_shape],
        out_shape=out_shape,
        compiler_params=_cparams("arbitrary"),
        name="decode_prep",
    )(*args)


def _mem_decode_kernel(qt_ref, kt_ref, vt_ref, ot_ref, s_scr, p_scr, *, rows):
    i = pl.program_id(0)
    tokens = kt_ref.shape[-1]
    lanes = 128
    nchunk = tokens // lanes

    @pl.when(i == 0)
    def _():
        ot_ref[...] = jnp.zeros_like(ot_ref)

    for r in range(rows):
        qrep = _column_lanes(qt_ref, i * rows + r, lanes)
        for h in range(MEM_HEADS):
            qh = qrep[h * HEAD_DIM:(h + 1) * HEAD_DIM]
            for c in range(nchunk):
                s_scr[r * MEM_HEADS + h:r * MEM_HEADS + h + 1, c * lanes:(c + 1) * lanes] = jnp.sum(
                    qh * kt_ref[r, h, :, c * lanes:(c + 1) * lanes], axis=0, keepdims=True)
    s = s_scr[...]
    pr = jnp.exp(s - jnp.max(s, axis=1, keepdims=True))
    p_scr[...] = pr / jnp.sum(pr, axis=1, keepdims=True)
    for r in range(rows):
        accs = [p_scr[r * MEM_HEADS + h:r * MEM_HEADS + h + 1, :] * vt_ref[r, h] for h in range(MEM_HEADS)]
        _add_lane_sum_column(ot_ref, jnp.concatenate(accs, axis=0), i * rows + r)


def _mem_decode(mq_t, mem_kt, mem_vt, layer, rows=8):
    n = mq_t.shape[1]
    tokens = mem_kt.shape[-1]
    kv_spec = pl.BlockSpec((None, rows, MEM_HEADS, HEAD_DIM, tokens), lambda i: (layer, i, 0, 0, 0))
    return pl.pallas_call(
        functools.partial(_mem_decode_kernel, rows=rows),
        grid=(n // rows,),
        in_specs=[pl.BlockSpec((MEM_W, n), lambda i: (0, 0)), kv_spec, kv_spec],
        out_specs=pl.BlockSpec((MEM_W, n), lambda i: (0, 0)),
        out_shape=jax.ShapeDtypeStruct((MEM_W, n), F32),
        scratch_shapes=[pltpu.VMEM((rows * MEM_HEADS, tokens), F32), pltpu.VMEM((rows * MEM_HEADS, tokens), F32)],
        compiler_params=_cparams("arbitrary"),
        name="mem_decode",
    )(mq_t, mem_kt, mem_vt)


def _moba_prefill_kernel(q_ref, k_ref, vt_ref, kmean_ref, seg_ref, g_ref, o_ref,
                         qa_scr, s_scr, mcur_scr, alpha_scr, m_scr, acc_scr, *, nblk):
    i = pl.program_id(1)
    blk = MOBA_BLOCK
    qn_t = _seg_rms(q_ref[...], seg_ref[...], g_ref[...]).T
    qs_t = (qn_t * (ATTN_SCALE * LOG2_E)).astype(BF16)
    kmean = kmean_ref[0]
    cols = Q_PER_KV * blk
    key_idx = lax.broadcasted_iota(jnp.int32, (blk, cols), 0)
    row_idx = lax.broadcasted_iota(jnp.int32, (blk, cols), 1) % blk
    causal = key_idx <= row_idx
    pad_t = jnp.zeros((KAUG_W - HEAD_DIM - nblk, blk), BF16)

    for g in range(MOBA_KV_HEADS):
        km = kmean[:, g * HEAD_DIM:(g + 1) * HEAD_DIM]
        parts = []
        for hh in range(Q_PER_KV):
            h = g * Q_PER_KV + hh
            gate_t = jnp.dot(km, qn_t[h * HEAD_DIM:(h + 1) * HEAD_DIM], precision=lax.Precision.HIGHEST,
                             preferred_element_type=F32)
            keep_t = _topk_keep(gate_t, i, axis=0)
            past_t = lax.broadcasted_iota(jnp.int32, gate_t.shape, 0) < i
            bias_t = jnp.where(past_t & ~keep_t, MASK_VALUE, 0.0).astype(BF16)
            parts.append(jnp.concatenate([qs_t[h * HEAD_DIM:(h + 1) * HEAD_DIM], bias_t, pad_t], axis=0))
        qa_scr[g] = jnp.concatenate(parts, axis=1)
    m_scr[...] = jnp.full(m_scr.shape, MASK_VALUE, F32)
    acc_scr[...] = jnp.zeros_like(acc_scr)

    def start_of(n):
        return pl.multiple_of(jnp.where(n == 0, i, n - 1) * blk, blk)

    def score_phase(n, slot, diagonal=False):
        start = start_of(n)
        for g in range(MOBA_KV_HEADS):
            s = jnp.dot(k_ref[g, pl.ds(start, blk), :], qa_scr[g], preferred_element_type=F32)
            if diagonal:
                s = jnp.where(causal, s, MASK_VALUE)
            s_scr[slot, g] = s
            m_old = m_scr[g]
            m_new = jnp.maximum(m_old, jnp.max(s, axis=0, keepdims=True))
            alpha_scr[slot, g] = jnp.exp2(m_old - m_new)
            mcur_scr[slot, g] = m_new
            m_scr[g] = m_new

    def value_phase(n, slot):
        start = start_of(n)
        for g in range(MOBA_KV_HEADS):
            p = jnp.exp2(s_scr[slot, g] - mcur_scr[slot, g]).astype(BF16)
            acc_scr[g] = alpha_scr[slot, g] * acc_scr[g] + jnp.dot(
                vt_ref[g * VAUG_H:(g + 1) * VAUG_H, pl.ds(start, blk)], p,
                preferred_element_type=F32)

    score_phase(0, 0, diagonal=True)

    def pair_step(t, _):
        value_phase(2 * t, 0)
        score_phase(2 * t + 1, 1)
        value_phase(2 * t + 1, 1)
        score_phase(2 * t + 2, 0)
        return 0

    lax.fori_loop(0, i // 2, pair_step, 0)

    @pl.when(i % 2 == 1)
    def _():
        value_phase(i - 1, 0)
        score_phase(i, 1)
        value_phase(i, 1)

    @pl.when(i % 2 == 0)
    def _():
        value_phase(i, 0)

    outs = []
    for g in range(MOBA_KV_HEADS):
        out_t = acc_scr[g, :HEAD_DIM] / acc_scr[g, HEAD_DIM:HEAD_DIM + 1]
        outs += [out_t[:, hh * blk:(hh + 1) * blk] for hh in range(Q_PER_KV)]
    o_ref[...] = jnp.concatenate(outs, axis=0).T.astype(BF16)


def _moba_prefill(proj, k_aug, v_t, kmean, seg_mat, g_row, bsz, seq):
    nblk = seq // MOBA_BLOCK
    cols = Q_PER_KV * MOBA_BLOCK
    return pl.pallas_call(
        functools.partial(_moba_prefill_kernel, nblk=nblk),
        grid=(bsz, nblk),
        in_specs=[pl.BlockSpec((MOBA_BLOCK, MIX_W), lambda b, i: (b * nblk + i, 0)),
                  pl.BlockSpec((MOBA_KV_HEADS, seq, KAUG_W), lambda b, i: (0, b, 0)),
                  pl.BlockSpec((MOBA_KV_HEADS * VAUG_H, seq), lambda b, i: (0, b)),
                  pl.BlockSpec((1, nblk, KV_W), lambda b, i: (b, 0, 0)),
                  pl.BlockSpec((MIX_W, MIX_W), lambda b, i: (0, 0)),
                  pl.BlockSpec((1, MIX_W), lambda b, i: (0, 0))],
        out_specs=pl.BlockSpec((MOBA_BLOCK, MIX_W), lambda b, i: (b * nblk + i, 0)),
        out_shape=jax.ShapeDtypeStruct((bsz * seq, MIX_W), BF16),
        scratch_shapes=[pltpu.VMEM((MOBA_KV_HEADS, KAUG_W, cols), BF16),
                        pltpu.VMEM((2, MOBA_KV_HEADS, MOBA_BLOCK, cols), F32),
                        pltpu.VMEM((2, MOBA_KV_HEADS, 1, cols), F32),
                        pltpu.VMEM((2, MOBA_KV_HEADS, 1, cols), F32),
                        pltpu.VMEM((MOBA_KV_HEADS, 1, cols), F32),
                        pltpu.VMEM((MOBA_KV_HEADS, VAUG_H, cols), F32)],
        compiler_params=_cparams("parallel", "arbitrary"),
        name="moba_prefill",
    )(proj, k_aug, v_t, kmean, seg_mat, g_row)


def _moba_decode_kernel(pt_ref, qt_ref, knt_ref, vnt_ref, *rest, n_pages, page):
    k_pages = rest[:n_pages]
    v_pages = rest[n_pages:2 * n_pages]
    ot_ref, s_scr, p_scr, own_scr = rest[2 * n_pages:]
    del pt_ref
    b = pl.program_id(0)
    n = qt_ref.shape[1]
    nblk = n_pages * page // MOBA_BLOCK

    @pl.when(b == 0)
    def _():
        ot_ref[...] = jnp.zeros_like(ot_ref)

    qrep = _column_lanes(qt_ref, b, page)
    knrep = _column_lanes(knt_ref, b, page)
    vnrep = _column_lanes(vnt_ref, b, page)

    s_scr[MOBA_HEADS:, :] = jnp.zeros((s_scr.shape[0] - MOBA_HEADS, s_scr.shape[1]), F32)
    own_scr[MOBA_HEADS:, :] = jnp.zeros((own_scr.shape[0] - MOBA_HEADS, page), F32)
    for h in range(MOBA_HEADS):
        g = h // Q_PER_KV
        qh = qrep[h * HEAD_DIM:(h + 1) * HEAD_DIM]
        for p in range(n_pages):
            s_scr[h:h + 1, p * page:(p + 1) * page] = jnp.sum(qh * k_pages[p][g], axis=0, keepdims=True)
        own_scr[h:h + 1, :] = jnp.sum(qh * knrep[g * HEAD_DIM:(g + 1) * HEAD_DIM], axis=0, keepdims=True)
    s = s_scr[...]

    gates = [jnp.sum(s[:, j * MOBA_BLOCK:(j + 1) * MOBA_BLOCK], axis=1, keepdims=True) for j in range(nblk)]
    parts = []
    for j in range(nblk):
        rank = jnp.zeros(gates[j].shape, jnp.int32)
        for jj in range(nblk):
            if jj != j:
                beats = (gates[jj] >= gates[j]) if jj < j else (gates[jj] > gates[j])
                rank = rank + jnp.where(beats, 1, 0)
        bias = jnp.where(rank < MOBA_TOPK, 0.0, MASK_VALUE)
        parts.append(s[:, j * MOBA_BLOCK:(j + 1) * MOBA_BLOCK] * ATTN_SCALE + bias)
    s = jnp.concatenate(parts, axis=1)
    s_own = own_scr[:, :1] * ATTN_SCALE
    m = jnp.maximum(jnp.max(s, axis=1, keepdims=True), s_own)
    pr = jnp.exp(s - m)
    p_own = jnp.exp(s_own - m)
    inv = 1.0 / (jnp.sum(pr, axis=1, keepdims=True) + p_own)
    p_scr[...] = pr * inv
    own_scr[...] = jnp.broadcast_to(p_own * inv * (1.0 / page), own_scr.shape)

    accs = []
    for h in range(MOBA_HEADS):
        g = h // Q_PER_KV
        acc = own_scr[h:h + 1, :] * vnrep[g * HEAD_DIM:(g + 1) * HEAD_DIM]
        for p in range(n_pages):
            acc = acc + p_scr[h:h + 1, p * page:(p + 1) * page] * v_pages[p][g]
        accs.append(acc)
    _add_lane_sum_column(ot_ref, jnp.concatenate(accs, axis=0), b)


def _column_lanes(xt_ref, col, lanes):
    n = xt_ref.shape[1]
    sel = (lax.broadcasted_iota(jnp.int32, (n, lanes), 0) == col).astype(F32)
    return jnp.dot(xt_ref[...], sel, precision=lax.Precision.HIGHEST, preferred_element_type=F32)


def _add_lane_sum_column(ot_ref, acc, col):
    place = (lax.broadcasted_iota(jnp.int32, (acc.shape[1], ot_ref.shape[1]), 1) == col).astype(BF16)
    hi = acc.astype(BF16)
    lo = (acc - hi.astype(F32)).astype(BF16)
    ot_ref[...] += (jnp.dot(hi, place, preferred_element_type=F32)
                    + jnp.dot(lo, place, preferred_element_type=F32))


def _moba_decode(q_t, knew_t, vnew_t, cache_kt, cache_vt, page_table):
    n = q_t.shape[1]
    page = cache_kt.shape[3]
    n_pages = page_table.shape[0] // n
    past = n_pages * page
    hrows = -(-MOBA_HEADS // 8) * 8

    def page_spec(p):
        return pl.BlockSpec((None, MOBA_KV_HEADS, HEAD_DIM, page),
                            lambda b, pt, p=p: (pt[b * n_pages + p], 0, 0, 0))

    const2 = lambda b, pt: (0, 0)
    grid_spec = pltpu.PrefetchScalarGridSpec(
        num_scalar_prefetch=1,
        grid=(n,),
        in_specs=([pl.BlockSpec((MIX_W, n), const2),
                   pl.BlockSpec((KV_W, n), const2),
                   pl.BlockSpec((KV_W, n), const2)]
                  + [page_spec(p) for p in range(n_pages)]
                  + [page_spec(p) for p in range(n_pages)]),
        out_specs=pl.BlockSpec((MIX_W, n), const2),
        scratch_shapes=[pltpu.VMEM((hrows, past), F32), pltpu.VMEM((hrows, past), F32),
                        pltpu.VMEM((hrows, page), F32)],
    )
    return pl.pallas_call(
        functools.partial(_moba_decode_kernel, n_pages=n_pages, page=page),
        grid_spec=grid_spec,
        out_shape=jax.ShapeDtypeStruct((MIX_W, n), F32),
        compiler_params=_cparams("arbitrary"),
        name="moba_decode",
    )(page_table, q_t, knew_t, vnew_t, *([cache_kt] * n_pages), *([cache_vt] * n_pages))


def _seg_matrix(width):
    head = jnp.arange(width) // HEAD_DIM
    return (head[:, None] == head[None, :]).astype(BF16) / HEAD_DIM


def _scan_perm():
    r = jnp.arange(SCAN_T)
    t = (r % SCAN_SUB) * SCAN_STEPS + r // SCAN_SUB
    return (t[:, None] == jnp.arange(SCAN_T)[None, :]).astype(BF16)


def _block_diag(x, row_axes):
    eye = jnp.eye(x.shape[1], dtype=x.dtype)
    if row_axes == "gh":
        full = jnp.einsum("kghp,gG->kghGp", x, eye)
    else:
        full = jnp.einsum("kghp,gG->kgpGh", x, eye)
    k, g, a, _, b = full.shape
    return full.reshape(k, g * a, g * b)


def _s5_constants(p, l, ab_re, ab_im, bb_re, bb_im):
    gpb = SSM_KB // SSM_GROUP
    shape4 = (N_KB, gpb, SSM_GROUP, SSM_STATE)
    bmat = jnp.concatenate([_block_diag(bb_re[l].reshape(shape4), "gh"),
                            _block_diag(bb_im[l].reshape(shape4), "gh")], axis=-1).astype(BF16)
    c_re = p["ssm_c_re"][l].reshape(shape4)
    c_im = p["ssm_c_im"][l].reshape(shape4)
    cmat = jnp.concatenate([_block_diag(c_re, "gp"), -_block_diag(c_im, "gp")], axis=1).astype(BF16)
    a_re = ab_re[l, ::SSM_GROUP].reshape(1, N_STATE)
    a_im = ab_im[l, ::SSM_GROUP].reshape(1, N_STATE)
    d_row = p["ssm_d"][l].reshape(1, MIX_W)
    return bmat, a_re, a_im, cmat, d_row, p["w_glu"][l].astype(BF16)


def kernel(x_prompt, x_sample, mem_prompt, state_ssm_re, state_ssm_im, cache_k, cache_v, page_table,
           cache_mem_k, cache_mem_v, g_mix, w_in, w_out, g_ffn, w_gu, w_down, ssm_a_re, ssm_a_im,
           ssm_log_dt, ssm_b_re, ssm_b_im, ssm_c_re, ssm_c_im, ssm_d, w_glu, g_q, g_mq, g_mem,
           w_mem_kv, g_mk, g_kv, w_kv, g_k):
    bsz, seq, _ = x_prompt.shape
    n_p = bsz * seq
    n_s = x_sample.shape[0]
    tokens = mem_prompt.shape[1]
    p = dict(ssm_c_re=ssm_c_re, ssm_c_im=ssm_c_im, ssm_d=ssm_d, w_glu=w_glu)

    w_in_b = w_in.astype(BF16)
    w_out_b = w_out.astype(BF16)
    w_gu_b = w_gu.astype(BF16)
    w_down_b = w_down.astype(BF16)
    seg_kv = _seg_matrix(KV_W)
    seg_mix = _seg_matrix(MIX_W)
    gq_rows = jnp.tile(g_q, (1, MOBA_HEADS))
    gmq_rows = jnp.tile(g_mq, (1, MEM_HEADS))

    rep = lambda a: jnp.repeat(a, SSM_GROUP, axis=1)
    b_rows = lambda b: jnp.swapaxes(b, 2, 3).reshape(N_A_LAYERS, MIX_W, SSM_STATE)
    ab_re, ab_im, bb_re, bb_im = _ssm_prep(rep(ssm_a_re), rep(ssm_a_im), rep(ssm_log_dt)[..., None],
                                           b_rows(ssm_b_re), b_rows(ssm_b_im))
    s5 = [_s5_constants(p, l, ab_re, ab_im, bb_re, bb_im) for l in range(N_A_LAYERS)]
    perm = _scan_perm()
    perm_t = perm.T

    mem_flat = mem_prompt.reshape(bsz * tokens, D_MODEL)
    pm_k, pm_v = _kv_proj(mem_flat, g_mem[:, None, :], w_mem_kv.astype(BF16), seg_kv,
                          jnp.tile(g_mk, (1, MEM_HEADS))[:, None, :], tm=256)
    p_mem_k = pm_k.reshape(DEPTH, bsz, tokens, MEM_HEADS, HEAD_DIM)
    p_mem_v = pm_v.reshape(DEPTH, bsz, tokens, MEM_HEADS, HEAD_DIM)
    pm_k_b = pm_k.reshape(DEPTH, bsz, tokens, MEM_W).astype(BF16)
    pm_v_b = pm_v.reshape(DEPTH, bsz, tokens, MEM_W).astype(BF16)

    g_kv_rows = g_kv.reshape(1, 1, D_MODEL)
    w_kv_b = w_kv.astype(BF16)[None]
    g_k_rows = jnp.tile(g_k, MOBA_KV_HEADS).reshape(1, 1, KV_W)

    h = x_prompt.reshape(n_p, D_MODEL)
    p_fin_re, p_fin_im = [], []
    k_aug = v_t = kmean = p_k = p_v = None
    nblk = seq // MOBA_BLOCK
    for l in range(DEPTH):
        proj = _norm_matmul(h, g_mix[l][None], w_in_b[l], tm=512)
        if l < N_A_LAYERS:
            mix, f_re, f_im = _s5_scan(proj, bsz, seq, perm, perm_t, *s5[l])
            p_fin_re.append(f_re.reshape(bsz, SSM_GROUPS, SSM_STATE))
            p_fin_im.append(f_im.reshape(bsz, SSM_GROUPS, SSM_STATE))
        else:
            mix = _moba_prefill(proj, k_aug, v_t, kmean, seg_mix, gq_rows[l - N_A_LAYERS][None], bsz, seq)
        mem = _mem_attn(proj, pm_k_b[l], pm_v_b[l], seg_kv, gmq_rows[l][None], bsz, seq, tq=512)
        h = _out_proj(h, mix, mem, w_out_b[l, :MIX_W], w_out_b[l, MIX_W:], tm=512)
        h = _ffn(h, g_ffn[l][None], w_gu_b[l], w_down_b[l], tm=512, fc=D_FF // 2)
        if l == N_A_LAYERS - 1:
            k_new, v_new, k_aug, v_t, kmean = _kv_proj(h, g_kv_rows, w_kv_b, seg_kv, g_k_rows, tm=MOBA_BLOCK,
                                                       blocks_per_seq=nblk)
            p_k = k_new.reshape(bsz, seq, MOBA_KV_HEADS, HEAD_DIM)
            p_v = v_new.reshape(bsz, seq, MOBA_KV_HEADS, HEAD_DIM)
            kmean = kmean.reshape(bsz, nblk, KV_W)
    y_prompt = h.reshape(bsz, seq, D_MODEL)

    ck_t = cache_k.transpose(0, 2, 3, 1)
    cv_t = cache_v.transpose(0, 2, 3, 1)
    cmk_t = cache_mem_k.transpose(0, 1, 3, 4, 2)
    cmv_t = cache_mem_v.transpose(0, 1, 3, 4, 2)

    h = x_sample.reshape(n_s, D_MODEL)
    s_fin_re, s_fin_im = [], []
    s_k = s_v = None
    for l in range(DEPTH):
        proj = _norm_matmul(h, g_mix[l][None], w_in_b[l], tm=n_s)
        if l < N_A_LAYERS:
            mix, x_re, x_im = _s5_step(proj, state_ssm_re[l].reshape(n_s, N_STATE),
                                       state_ssm_im[l].reshape(n_s, N_STATE), *s5[l])
            s_fin_re.append(x_re.reshape(n_s, SSM_GROUPS, SSM_STATE))
            s_fin_im.append(x_im.reshape(n_s, SSM_GROUPS, SSM_STATE))
            mq_t, = _decode_prep(proj, seg_kv, gmq_rows[l][None])
        else:
            mq_t, q_t, knew_t, vnew_t = _decode_prep(
                proj, seg_kv, gmq_rows[l][None],
                moba=(seg_mix, gq_rows[l - N_A_LAYERS][None], s_k.reshape(n_s, KV_W), s_v.reshape(n_s, KV_W)))
            mix = _moba_decode(q_t, knew_t, vnew_t, ck_t, cv_t, page_table.reshape(-1)).T.astype(BF16)
        mem = _mem_decode(mq_t, cmk_t, cmv_t, l).T.astype(BF16)
        h = _out_proj(h, mix, mem, w_out_b[l, :MIX_W], w_out_b[l, MIX_W:], tm=n_s)
        h = _ffn(h, g_ffn[l][None], w_gu_b[l], w_down_b[l], tm=n_s, fc=D_FF // 2)
        if l == N_A_LAYERS - 1:
            s_k, s_v = _kv_proj(h, g_kv_rows, w_kv_b, seg_kv, g_k_rows, tm=n_s)
    y_sample = h.reshape(n_s, 1, D_MODEL)

    return (y_prompt, y_sample,
            jnp.stack(p_fin_re), jnp.stack(p_fin_im), p_k, p_v, p_mem_k, p_mem_v,
            jnp.stack(s_fin_re), jnp.stack(s_fin_im),
            s_k.reshape(n_s, 1, MOBA_KV_HEADS, HEAD_DIM), s_v.reshape(n_s, 1, MOBA_KV_HEADS, HEAD_DIM))
```

```python
import functools
import math

import jax
import jax.numpy as jnp
from jax import lax
from jax.experimental import pallas as pl
from jax.experimental.pallas import tpu as pltpu

F32 = jnp.float32
BF16 = jnp.bfloat16

D_MODEL = 1024
DEPTH = 4
N_A_LAYERS = DEPTH // 2
HEAD_DIM = 64
MIX_W = 3 * D_MODEL // 4
MEM_W = D_MODEL // 4
MEM_HEADS = MEM_W // HEAD_DIM
SSM_GROUP = 16
SSM_GROUPS = MIX_W // SSM_GROUP
SSM_STATE = 64
N_STATE = SSM_GROUPS * SSM_STATE
MOBA_HEADS = MIX_W // HEAD_DIM
MOBA_KV_HEADS = 4
Q_PER_KV = MOBA_HEADS // MOBA_KV_HEADS
KV_W = MOBA_KV_HEADS * HEAD_DIM
MOBA_BLOCK = 256
MOBA_TOPK = 3
D_FF = ((-(-8 * D_MODEL // 3) + 255) // 256) * 256
EPS = 1e-6
ATTN_SCALE = HEAD_DIM ** -0.5
MASK_VALUE = -1e30
KAUG_W = 128
VAUG_H = HEAD_DIM + 16
LOG2_E = math.log2(math.e)

SCAN_SUB = 8
SCAN_STEPS = 32
SCAN_T = SCAN_SUB * SCAN_STEPS
SSM_KB = 256
N_KB = MIX_W // SSM_KB
KB_STATES = (SSM_KB // SSM_GROUP) * SSM_STATE
SCAN_CW = 512

VMEM_LIMIT_BYTES = 56 * 1024 * 1024


def _cparams(*sem):
    return pltpu.CompilerParams(dimension_semantics=sem, vmem_limit_bytes=VMEM_LIMIT_BYTES)


def _rms(x, g_row):
    ms = jnp.mean(x * x, axis=-1, keepdims=True)
    return x * lax.rsqrt(ms + EPS) * g_row


def _seg_rms(x, seg_mat, g_row):
    sq = x * x
    hi = sq.astype(BF16)
    lo = (sq - hi.astype(F32)).astype(BF16)
    ms = (jnp.dot(hi, seg_mat, preferred_element_type=F32)
          + jnp.dot(lo, seg_mat, preferred_element_type=F32))
    return x * lax.rsqrt(ms + EPS) * g_row


def _dot_nt(a, b, precision=None):
    return lax.dot_general(a, b, (((1,), (1,)), ((), ())), precision=precision,
                           preferred_element_type=F32)


def _topk_keep(gate, n_valid, axis):
    n = gate.shape[axis]
    idx = lax.broadcasted_iota(jnp.int32, gate.shape, axis)
    rank = jnp.zeros(gate.shape, jnp.int32)
    for j in range(n):
        gj = lax.slice_in_dim(gate, j, j + 1, axis=axis)
        beats = (gj > gate) | ((gj == gate) & (j < idx))
        rank = rank + jnp.where(beats, jnp.where(j < n_valid, 1, 0), 0)
    return (rank < MOBA_TOPK) & (idx < n_valid)


def _norm_matmul_kernel(x_ref, g_ref, w_ref, o_ref):
    a = _rms(x_ref[...], g_ref[...]).astype(BF16)
    o_ref[...] = jnp.dot(a, w_ref[...], preferred_element_type=F32)


def _norm_matmul(x, g_row, w, tm):
    n, d = x.shape
    nout = w.shape[1]
    return pl.pallas_call(
        _norm_matmul_kernel,
        grid=(n // tm,),
        in_specs=[pl.BlockSpec((tm, d), lambda i: (i, 0)),
                  pl.BlockSpec((1, d), lambda i: (0, 0)),
                  pl.BlockSpec((d, nout), lambda i: (0, 0))],
        out_specs=pl.BlockSpec((tm, nout), lambda i: (i, 0)),
        out_shape=jax.ShapeDtypeStruct((n, nout), F32),
        compiler_params=_cparams("parallel"),
        name="norm_matmul",
    )(x, g_row, w)


def _kv_kernel(x_ref, g_ref, w_ref, seg_ref, gk_ref, k_ref, v_ref, *extra, half, blocks_per_seq):
    a = _rms(x_ref[...], g_ref[0]).astype(BF16)
    kv = jnp.dot(a, w_ref[0], preferred_element_type=F32)
    k = _seg_rms(kv[:, :half], seg_ref[...], gk_ref[0])
    v = kv[:, half:]
    k_ref[0] = k
    v_ref[0] = v
    if blocks_per_seq:
        kaug_ref, vt_ref, kmean_ref = extra
        lane = lax.broadcasted_iota(jnp.int32, (k.shape[0], KAUG_W), 1)
        tail = jnp.where(lane == HEAD_DIM + pl.program_id(1) % blocks_per_seq, 1.0, 0.0)
        for g in range(MOBA_KV_HEADS):
            col = k[:, (g // 2) * KAUG_W:(g // 2 + 1) * KAUG_W]
            if g % 2:
                col = pltpu.roll(col, HEAD_DIM, axis=1)
            kaug_ref[g] = jnp.where(lane < HEAD_DIM, col, tail).astype(BF16)
        v_t = v.T
        ones_row = jnp.where(lax.broadcasted_iota(jnp.int32, (VAUG_H - HEAD_DIM, k.shape[0]), 0) == 0, 1.0, 0.0)
        vt_ref[...] = jnp.concatenate(
            [piece for g in range(MOBA_KV_HEADS) for piece in (v_t[g * HEAD_DIM:(g + 1) * HEAD_DIM], ones_row)],
            axis=0).astype(BF16)
        kmean_ref[0] = jnp.mean(k, axis=0, keepdims=True)


def _kv_proj(x, g_rows, w, seg_mat, gk_rows, tm, blocks_per_seq=0):
    n, d = x.shape
    layers, _, two_half = w.shape
    half = two_half // 2
    out_shape = [jax.ShapeDtypeStruct((layers, n, half), F32)] * 2
    out_specs = [pl.BlockSpec((1, tm, half), lambda l, i: (l, i, 0))] * 2
    if blocks_per_seq:
        assert layers == 1 and tm == MOBA_BLOCK and half == KV_W
        out_shape += [jax.ShapeDtypeStruct((MOBA_KV_HEADS, n, KAUG_W), BF16),
                      jax.ShapeDtypeStruct((MOBA_KV_HEADS * VAUG_H, n), BF16),
                      jax.ShapeDtypeStruct((n // tm, 1, half), F32)]
        out_specs += [pl.BlockSpec((MOBA_KV_HEADS, tm, KAUG_W), lambda l, i: (0, i, 0)),
                      pl.BlockSpec((MOBA_KV_HEADS * VAUG_H, tm), lambda l, i: (0, i)),
                      pl.BlockSpec((1, 1, half), lambda l, i: (i, 0, 0))]
    return pl.pallas_call(
        functools.partial(_kv_kernel, half=half, blocks_per_seq=blocks_per_seq),
        grid=(layers, n // tm),
        in_specs=[pl.BlockSpec((tm, d), lambda l, i: (i, 0)),
                  pl.BlockSpec((1, 1, d), lambda l, i: (l, 0, 0)),
                  pl.BlockSpec((1, d, two_half), lambda l, i: (l, 0, 0)),
                  pl.BlockSpec((half, half), lambda l, i: (0, 0)),
                  pl.BlockSpec((1, 1, half), lambda l, i: (l, 0, 0))],
        out_specs=out_specs,
        out_shape=out_shape,
        compiler_params=_cparams("parallel", "parallel"),
        name="kv_proj",
    )(x, g_rows, w, seg_mat, gk_rows)


def _layer_tail_kernel(h_ref, mix_ref, mem_ref, wmix_ref, wmem_ref, g_ref, wgu_ref, wd_ref, *rest):
    h1 = (h_ref[...]
          + jnp.dot(mix_ref[...], wmix_ref[...], preferred_element_type=F32)
          + jnp.dot(mem_ref[...], wmem_ref[...], preferred_element_type=F32))
    a = _rms(h1, g_ref[...]).astype(BF16)
    gate = jnp.dot(a, wgu_ref[:, :D_FF], preferred_element_type=F32)
    up = jnp.dot(a, wgu_ref[:, D_FF:], preferred_element_type=F32)
    act = (gate * jax.nn.sigmoid(gate) * up).astype(BF16)
    h2 = h1 + jnp.dot(act, wd_ref[...], preferred_element_type=F32)
    if len(rest) == 1:
        rest[0][...] = h2
    else:
        gn_ref, win_ref, o_ref, proj_ref = rest
        o_ref[...] = h2
        proj_ref[...] = jnp.dot(_rms(h2, gn_ref[...]).astype(BF16), win_ref[...], preferred_element_type=F32)


def _layer_tail(h, mix, mem, w_mix, w_mem, g_row, w_gu, w_down, tm, next_in=None):
    n, d = h.shape
    row = lambda w: pl.BlockSpec((tm, w), lambda i: (i, 0))
    resident = lambda a: pl.BlockSpec(a.shape, lambda i: (0,) * a.ndim, pipeline_mode=pl.Buffered(1))
    args = [h, mix, mem, w_mix, w_mem, g_row, w_gu, w_down]
    in_specs = [row(d), row(MIX_W), row(MEM_W)] + [resident(a) for a in args[3:]]
    out_shape = [jax.ShapeDtypeStruct((n, d), F32)]
    if next_in is not None:
        args += list(next_in)
        in_specs += [resident(a) for a in next_in]
        out_shape.append(jax.ShapeDtypeStruct((n, next_in[1].shape[1]), F32))
    out = pl.pallas_call(
        _layer_tail_kernel,
        grid=(n // tm,),
        in_specs=in_specs,
        out_specs=[row(s.shape[1]) for s in out_shape],
        out_shape=out_shape,
        compiler_params=_cparams("parallel"),
        name="layer_tail",
    )(*args)
    return out if next_in is not None else (out[0], None)


def _ssm_prep_kernel(are_ref, aim_ref, ldt_ref, bre_ref, bim_ref, abr_ref, abi_ref, bbr_ref, bbi_ref):
    dt = jnp.exp(ldt_ref[0])
    lam_re = jnp.minimum(are_ref[0], -1e-4)
    lam_im = aim_ref[0]
    mag = jnp.exp(dt * lam_re)
    ang = dt * lam_im
    ab_re = mag * jnp.cos(ang)
    ab_im = mag * jnp.sin(ang)
    den = lam_re * lam_re + lam_im * lam_im
    num_re = ab_re - 1.0
    f_re = (num_re * lam_re + ab_im * lam_im) / den
    f_im = (ab_im * lam_re - num_re * lam_im) / den
    b_re = bre_ref[0]
    b_im = bim_ref[0]
    abr_ref[0] = ab_re
    abi_ref[0] = ab_im
    bbr_ref[0] = f_re * b_re - f_im * b_im
    bbi_ref[0] = f_re * b_im + f_im * b_re


def _ssm_prep(a_re, a_im, log_dt, b_re, b_im):
    layers = a_re.shape[0]
    big = pl.BlockSpec((1, MIX_W, SSM_STATE), lambda l: (l, 0, 0))
    return pl.pallas_call(
        _ssm_prep_kernel,
        grid=(layers,),
        in_specs=[big, big, pl.BlockSpec((1, MIX_W, 1), lambda l: (l, 0, 0)), big, big],
        out_specs=[big] * 4,
        out_shape=[jax.ShapeDtypeStruct((layers, MIX_W, SSM_STATE), F32)] * 4,
        compiler_params=_cparams("parallel"),
        name="ssm_prep",
    )(a_re, a_im, log_dt, b_re, b_im)


def _re_cols(n0):
    kb, off = divmod(n0, KB_STATES)
    return kb * 2 * KB_STATES + off


def _s5_output(y_state, u, d_ref, wglu_ref):
    y = jax.nn.gelu(y_state + d_ref[...] * u)
    z = jnp.dot(y.astype(BF16), wglu_ref[...], preferred_element_type=F32)
    return (y * jax.nn.sigmoid(z)).astype(BF16)


def _s5_scan_kernel(u_ref, perm_ref, permt_ref, bmat_ref, are_ref, aim_ref, cmat_ref, d_ref, wglu_ref,
                    y_ref, fre_ref, fim_ref, x_scr, apr_scr, api_scr, str_scr, sti_scr):
    c = pl.program_id(1)

    @pl.when(c == 0)
    def _():
        str_scr[...] = jnp.zeros_like(str_scr)
        sti_scr[...] = jnp.zeros_like(sti_scr)
        ar, ai = are_ref[...], aim_ref[...]
        pr, pi = ar, ai
        for i in range(SCAN_STEPS):
            apr_scr[i:i + 1, :] = pr
            api_scr[i:i + 1, :] = pi
            pr, pi = pr * ar - pi * ai, pr * ai + pi * ar

    u = u_ref[...]
    up = jnp.dot(perm_ref[...], u.astype(BF16), preferred_element_type=F32).astype(BF16)
    for kb in range(N_KB):
        x_scr[:, kb * 2 * KB_STATES:(kb + 1) * 2 * KB_STATES] = jnp.dot(
            up[:, kb * SSM_KB:(kb + 1) * SSM_KB], bmat_ref[kb], preferred_element_type=F32)

    sub = lax.broadcasted_iota(jnp.int32, (SCAN_SUB, SCAN_CW), 0)
    for n0 in range(0, N_STATE, SCAN_CW):
        rc = _re_cols(n0)
        ic = rc + KB_STATES
        ar = jnp.broadcast_to(are_ref[:, n0:n0 + SCAN_CW], (SCAN_SUB, SCAN_CW))
        ai = jnp.broadcast_to(aim_ref[:, n0:n0 + SCAN_CW], (SCAN_SUB, SCAN_CW))

        def local_step(i, carry, rc=rc, ic=ic, ar=ar, ai=ai):
            xr, xi = carry
            r0 = pl.multiple_of(i * SCAN_SUB, SCAN_SUB)
            nxr = ar * xr - ai * xi + x_scr[pl.ds(r0, SCAN_SUB), rc:rc + SCAN_CW]
            nxi = ar * xi + ai * xr + x_scr[pl.ds(r0, SCAN_SUB), ic:ic + SCAN_CW]
            x_scr[pl.ds(r0, SCAN_SUB), rc:rc + SCAN_CW] = nxr
            x_scr[pl.ds(r0, SCAN_SUB), ic:ic + SCAN_CW] = nxi
            return nxr, nxi

        zero = jnp.zeros((SCAN_SUB, SCAN_CW), F32)
        fin_r, fin_i = lax.fori_loop(0, SCAN_STEPS, local_step, (zero, zero), unroll=4)

        a_t_r = apr_scr[SCAN_STEPS - 1:SCAN_STEPS, n0:n0 + SCAN_CW]
        a_t_i = api_scr[SCAN_STEPS - 1:SCAN_STEPS, n0:n0 + SCAN_CW]
        cr = str_scr[:, n0:n0 + SCAN_CW]
        ci = sti_scr[:, n0:n0 + SCAN_CW]
        car_r = jnp.zeros((SCAN_SUB, SCAN_CW), F32)
        car_i = jnp.zeros((SCAN_SUB, SCAN_CW), F32)
        for j in range(SCAN_SUB):
            car_r = jnp.where(sub == j, cr, car_r)
            car_i = jnp.where(sub == j, ci, car_i)
            cr, ci = (fin_r[j:j + 1] + a_t_r * cr - a_t_i * ci,
                      fin_i[j:j + 1] + a_t_r * ci + a_t_i * cr)
        str_scr[:, n0:n0 + SCAN_CW] = cr
        sti_scr[:, n0:n0 + SCAN_CW] = ci

        def fix_step(i, _, rc=rc, ic=ic, n0=n0, car_r=car_r, car_i=car_i):
            r0 = pl.multiple_of(i * SCAN_SUB, SCAN_SUB)
            pr = apr_scr[pl.ds(i, 1), n0:n0 + SCAN_CW]
            pi = api_scr[pl.ds(i, 1), n0:n0 + SCAN_CW]
            xr = x_scr[pl.ds(r0, SCAN_SUB), rc:rc + SCAN_CW]
            xi = x_scr[pl.ds(r0, SCAN_SUB), ic:ic + SCAN_CW]
            x_scr[pl.ds(r0, SCAN_SUB), rc:rc + SCAN_CW] = xr + (pr * car_r - pi * car_i)
            x_scr[pl.ds(r0, SCAN_SUB), ic:ic + SCAN_CW] = xi + (pr * car_i + pi * car_r)
            return 0

        lax.fori_loop(0, SCAN_STEPS, fix_step, 0, unroll=4)

    ys = [jnp.dot(x_scr[:, kb * 2 * KB_STATES:(kb + 1) * 2 * KB_STATES].astype(BF16), cmat_ref[kb],
                  preferred_element_type=F32) for kb in range(N_KB)]
    yp = jnp.concatenate(ys, axis=1)
    hi = yp.astype(BF16)
    lo = (yp - hi.astype(F32)).astype(BF16)
    y_state = (jnp.dot(permt_ref[...], hi, preferred_element_type=F32)
               + jnp.dot(permt_ref[...], lo, preferred_element_type=F32))
    y_ref[...] = _s5_output(y_state, u, d_ref, wglu_ref)
    fre_ref[0] = str_scr[...]
    fim_ref[0] = sti_scr[...]


def _s5_scan(proj, bsz, seq, perm, perm_t, bmat, a_re, a_im, cmat, d_row, w_glu):
    nchunk = seq // SCAN_T
    const2 = lambda b, c: (0, 0)
    const3 = lambda b, c: (0, 0, 0)
    state_spec = pl.BlockSpec((1, 1, N_STATE), lambda b, c: (b, 0, 0))
    return pl.pallas_call(
        _s5_scan_kernel,
        grid=(bsz, nchunk),
        in_specs=[pl.BlockSpec((SCAN_T, MIX_W), lambda b, c: (b * nchunk + c, 0)),
                  pl.BlockSpec((SCAN_T, SCAN_T), const2),
                  pl.BlockSpec((SCAN_T, SCAN_T), const2),
                  pl.BlockSpec((N_KB, SSM_KB, 2 * KB_STATES), const3),
                  pl.BlockSpec((1, N_STATE), const2),
                  pl.BlockSpec((1, N_STATE), const2),
                  pl.BlockSpec((N_KB, 2 * KB_STATES, SSM_KB), const3),
                  pl.BlockSpec((1, MIX_W), const2),
                  pl.BlockSpec((MIX_W, MIX_W), const2)],
        out_specs=[pl.BlockSpec((SCAN_T, MIX_W), lambda b, c: (b * nchunk + c, 0)), state_spec, state_spec],
        out_shape=[jax.ShapeDtypeStruct((bsz * seq, MIX_W), BF16),
                   jax.ShapeDtypeStruct((bsz, 1, N_STATE), F32),
                   jax.ShapeDtypeStruct((bsz, 1, N_STATE), F32)],
        scratch_shapes=[pltpu.VMEM((SCAN_T, 2 * N_STATE), F32),
                        pltpu.VMEM((SCAN_STEPS, N_STATE), F32),
                        pltpu.VMEM((SCAN_STEPS, N_STATE), F32),
                        pltpu.VMEM((1, N_STATE), F32),
                        pltpu.VMEM((1, N_STATE), F32)],
        compiler_params=_cparams("parallel", "arbitrary"),
        name="s5_scan",
    )(proj, perm, perm_t, bmat, a_re, a_im, cmat, d_row, w_glu)


def _s5_step_kernel(u_ref, hre_ref, him_ref, bmat_ref, are_ref, aim_ref, cmat_ref, d_ref, wglu_ref,
                    y_ref, xre_ref, xim_ref):
    u = u_ref[...]
    ub = u.astype(BF16)
    ys = []
    for kb in range(N_KB):
        bu = jnp.dot(ub[:, kb * SSM_KB:(kb + 1) * SSM_KB], bmat_ref[kb], preferred_element_type=F32)
        n0 = kb * KB_STATES
        ar, ai = are_ref[:, n0:n0 + KB_STATES], aim_ref[:, n0:n0 + KB_STATES]
        hr, hi = hre_ref[:, n0:n0 + KB_STATES], him_ref[:, n0:n0 + KB_STATES]
        xr = ar * hr - ai * hi + bu[:, :KB_STATES]
        xi = ar * hi + ai * hr + bu[:, KB_STATES:]
        xre_ref[:, n0:n0 + KB_STATES] = xr
        xim_ref[:, n0:n0 + KB_STATES] = xi
        x = jnp.concatenate([xr, xi], axis=1).astype(BF16)
        ys.append(jnp.dot(x, cmat_ref[kb], preferred_element_type=F32))
    y_ref[...] = _s5_output(jnp.concatenate(ys, axis=1), u, d_ref, wglu_ref)


def _s5_step(proj, h_re, h_im, bmat, a_re, a_im, cmat, d_row, w_glu):
    n = proj.shape[0]
    const2 = lambda i: (0, 0)
    const3 = lambda i: (0, 0, 0)
    return pl.pallas_call(
        _s5_step_kernel,
        grid=(1,),
        in_specs=[pl.BlockSpec((n, MIX_W), const2),
                  pl.BlockSpec((n, N_STATE), const2),
                  pl.BlockSpec((n, N_STATE), const2),
                  pl.BlockSpec((N_KB, SSM_KB, 2 * KB_STATES), const3),
                  pl.BlockSpec((1, N_STATE), const2),
                  pl.BlockSpec((1, N_STATE), const2),
                  pl.BlockSpec((N_KB, 2 * KB_STATES, SSM_KB), const3),
                  pl.BlockSpec((1, MIX_W), const2),
                  pl.BlockSpec((MIX_W, MIX_W), const2)],
        out_specs=[pl.BlockSpec((n, MIX_W), const2),
                   pl.BlockSpec((n, N_STATE), const2),
                   pl.BlockSpec((n, N_STATE), const2)],
        out_shape=[jax.ShapeDtypeStruct((n, MIX_W), BF16),
                   jax.ShapeDtypeStruct((n, N_STATE), F32),
                   jax.ShapeDtypeStruct((n, N_STATE), F32)],
        compiler_params=_cparams("arbitrary"),
        name="s5_step",
    )(proj, h_re, h_im, bmat, a_re, a_im, cmat, d_row, w_glu)


def _mem_attn_kernel(q_ref, k_ref, v_ref, seg_ref, g_ref, o_ref):
    q = _seg_rms(q_ref[...], seg_ref[...], g_ref[...]) * ATTN_SCALE
    k = k_ref[0]
    v = v_ref[0]
    lane_head = lax.broadcasted_iota(jnp.int32, q.shape, 1) // HEAD_DIM
    out = jnp.zeros(q.shape, F32)
    for h in range(MEM_HEADS):
        qh = jnp.where(lane_head == h, q, 0.0).astype(BF16)
        s = _dot_nt(qh, k)
        p = jnp.exp(s - jnp.max(s, axis=-1, keepdims=True))
        o = jnp.dot(p.astype(BF16), v, preferred_element_type=F32) / jnp.sum(p, axis=-1, keepdims=True)
        out = jnp.where(lane_head == h, o, out)
    o_ref[...] = out.astype(BF16)


def _mem_attn(proj, mem_k, mem_v, seg_mat, g_row, bsz, seq, tq):
    nq = seq // tq
    tokens = mem_k.shape[1]
    qcol = MIX_W // MEM_W
    kv_spec = pl.BlockSpec((1, tokens, MEM_W), lambda b, i: (b, 0, 0))
    return pl.pallas_call(
        _mem_attn_kernel,
        grid=(bsz, nq),
        in_specs=[pl.BlockSpec((tq, MEM_W), lambda b, i: (b * nq + i, qcol)),
                  kv_spec, kv_spec,
                  pl.BlockSpec((MEM_W, MEM_W), lambda b, i: (0, 0)),
                  pl.BlockSpec((1, MEM_W), lambda b, i: (0, 0))],
        out_specs=pl.BlockSpec((tq, MEM_W), lambda b, i: (b * nq + i, 0)),
        out_shape=jax.ShapeDtypeStruct((bsz * seq, MEM_W), BF16),
        compiler_params=_cparams("parallel", "parallel"),
        name="mem_attn",
    )(proj, mem_k, mem_v, seg_mat, g_row)


def _decode_prep_kernel(proj_ref, segk_ref, gmq_ref, *rest, with_moba):
    x = proj_ref[...]
    if with_moba:
        segm_ref, gq_ref, knew_ref, vnew_ref, mqt_ref, qt_ref, knt_ref, vnt_ref = rest
        qt_ref[...] = _seg_rms(x[:, :MIX_W], segm_ref[...], gq_ref[...]).T
        knt_ref[...] = knew_ref[...].T
        vnt_ref[...] = vnew_ref[...].T
    else:
        mqt_ref, = rest
    mqt_ref[...] = (_seg_rms(x[:, MIX_W:], segk_ref[...], gmq_ref[...]) * ATTN_SCALE).T


def _decode_prep(proj, seg_kv, gmq_row, moba=None):
    n, d = proj.shape
    const2 = lambda i: (0, 0)
    args = [proj, seg_kv, gmq_row]
    in_specs = [pl.BlockSpec((n, d), const2), pl.BlockSpec((MEM_W, MEM_W), const2), pl.BlockSpec((1, MEM_W), const2)]
    out_shape = [jax.ShapeDtypeStruct((MEM_W, n), F32)]
    if moba is not None:
        args += list(moba)
        in_specs += [pl.BlockSpec((MIX_W, MIX_W), const2), pl.BlockSpec((1, MIX_W), const2),
                     pl.BlockSpec((n, KV_W), const2), pl.BlockSpec((n, KV_W), const2)]
        out_shape += [jax.ShapeDtypeStruct((MIX_W, n), F32), jax.ShapeDtypeStruct((KV_W, n), F32),
                      jax.ShapeDtypeStruct((KV_W, n), F32)]
    return pl.pallas_call(
        functools.partial(_decode_prep_kernel, with_moba=moba is not None),
        grid=(1,),
        in_specs=in_specs,
        out_specs=[pl.BlockSpec(s.shape, const2) for s in out_shape],
        out_shape=out_shape,
        compiler_params=_cparams("arbitrary"),
        name="decode_prep",
    )(*args)


def _mem_decode_kernel(qt_ref, kt_ref, vt_ref, ot_ref, s_scr, p_scr, *, rows):
    i = pl.program_id(0)
    tokens = kt_ref.shape[-1]
    lanes = 128
    nchunk = tokens // lanes

    @pl.when(i == 0)
    def _():
        ot_ref[...] = jnp.zeros_like(ot_ref)

    for r in range(rows):
        qrep = _column_lanes(qt_ref, i * rows + r, lanes)
        for h in range(MEM_HEADS):
            qh = qrep[h * HEAD_DIM:(h + 1) * HEAD_DIM]
            for c in range(nchunk):
                s_scr[r * MEM_HEADS + h:r * MEM_HEADS + h + 1, c * lanes:(c + 1) * lanes] = jnp.sum(
                    qh * kt_ref[r, h, :, c * lanes:(c + 1) * lanes], axis=0, keepdims=True)
    s = s_scr[...]
    pr = jnp.exp(s - jnp.max(s, axis=1, keepdims=True))
    p_scr[...] = pr / jnp.sum(pr, axis=1, keepdims=True)
    for r in range(rows):
        accs = [p_scr[r * MEM_HEADS + h:r * MEM_HEADS + h + 1, :] * vt_ref[r, h] for h in range(MEM_HEADS)]
        _add_lane_sum_column(ot_ref, jnp.concatenate(accs, axis=0), i * rows + r)


def _mem_decode(mq_t, mem_kt, mem_vt, layer, rows=8):
    n = mq_t.shape[1]
    tokens = mem_kt.shape[-1]
    kv_spec = pl.BlockSpec((None, rows, MEM_HEADS, HEAD_DIM, tokens), lambda i: (layer, i, 0, 0, 0))
    return pl.pallas_call(
        functools.partial(_mem_decode_kernel, rows=rows),
        grid=(n // rows,),
        in_specs=[pl.BlockSpec((MEM_W, n), lambda i: (0, 0)), kv_spec, kv_spec],
        out_specs=pl.BlockSpec((MEM_W, n), lambda i: (0, 0)),
        out_shape=jax.ShapeDtypeStruct((MEM_W, n), F32),
        scratch_shapes=[pltpu.VMEM((rows * MEM_HEADS, tokens), F32), pltpu.VMEM((rows * MEM_HEADS, tokens), F32)],
        compiler_params=_cparams("arbitrary"),
        name="mem_decode",
    )(mq_t, mem_kt, mem_vt)


def _moba_prefill_kernel(q_ref, k_ref, vt_ref, kmean_ref, seg_ref, g_ref, o_ref,
                         qa_scr, s_scr, mcur_scr, alpha_scr, m_scr, acc_scr, *, nblk):
    i = pl.program_id(1)
    blk = MOBA_BLOCK
    qn_t = _seg_rms(q_ref[...], seg_ref[...], g_ref[...]).T
    qs_t = (qn_t * (ATTN_SCALE * LOG2_E)).astype(BF16)
    kmean = kmean_ref[0]
    cols = Q_PER_KV * blk
    key_idx = lax.broadcasted_iota(jnp.int32, (blk, cols), 0)
    row_idx = lax.broadcasted_iota(jnp.int32, (blk, cols), 1) % blk
    causal = key_idx <= row_idx
    pad_t = jnp.zeros((KAUG_W - HEAD_DIM - nblk, blk), BF16)

    for g in range(MOBA_KV_HEADS):
        km = kmean[:, g * HEAD_DIM:(g + 1) * HEAD_DIM]
        parts = []
        for hh in range(Q_PER_KV):
            h = g * Q_PER_KV + hh
            gate_t = jnp.dot(km, qn_t[h * HEAD_DIM:(h + 1) * HEAD_DIM], precision=lax.Precision.HIGHEST,
                             preferred_element_type=F32)
            keep_t = _topk_keep(gate_t, i, axis=0)
            past_t = lax.broadcasted_iota(jnp.int32, gate_t.shape, 0) < i
            bias_t = jnp.where(past_t & ~keep_t, MASK_VALUE, 0.0).astype(BF16)
            parts.append(jnp.concatenate([qs_t[h * HEAD_DIM:(h + 1) * HEAD_DIM], bias_t, pad_t], axis=0))
        qa_scr[g] = jnp.concatenate(parts, axis=1)
    m_scr[...] = jnp.full(m_scr.shape, MASK_VALUE, F32)
    acc_scr[...] = jnp.zeros_like(acc_scr)

    def start_of(n):
        return pl.multiple_of(jnp.where(n == 0, i, n - 1) * blk, blk)

    def score_phase(n, slot, diagonal=False):
        start = start_of(n)
        for g in range(MOBA_KV_HEADS):
            s = jnp.dot(k_ref[g, pl.ds(start, blk), :], qa_scr[g], preferred_element_type=F32)
            if diagonal:
                s = jnp.where(causal, s, MASK_VALUE)
            s_scr[slot, g] = s
            m_old = m_scr[g]
            m_new = jnp.maximum(m_old, jnp.max(s, axis=0, keepdims=True))
            alpha_scr[slot, g] = jnp.exp2(m_old - m_new)
            mcur_scr[slot, g] = m_new
            m_scr[g] = m_new

    def value_phase(n, slot):
        start = start_of(n)
        for g in range(MOBA_KV_HEADS):
            p = jnp.exp2(s_scr[slot, g] - mcur_scr[slot, g]).astype(BF16)
            acc_scr[g] = alpha_scr[slot, g] * acc_scr[g] + jnp.dot(
                vt_ref[g * VAUG_H:(g + 1) * VAUG_H, pl.ds(start, blk)], p,
                preferred_element_type=F32)

    score_phase(0, 0, diagonal=True)

    def pair_step(t, _):
        value_phase(2 * t, 0)
        score_phase(2 * t + 1, 1)
        value_phase(2 * t + 1, 1)
        score_phase(2 * t + 2, 0)
        return 0

    lax.fori_loop(0, i // 2, pair_step, 0)

    @pl.when(i % 2 == 1)
    def _():
        value_phase(i - 1, 0)
        score_phase(i, 1)
        value_phase(i, 1)

    @pl.when(i % 2 == 0)
    def _():
        value_phase(i, 0)

    outs = []
    for g in range(MOBA_KV_HEADS):
        out_t = acc_scr[g, :HEAD_DIM] / acc_scr[g, HEAD_DIM:HEAD_DIM + 1]
        outs += [out_t[:, hh * blk:(hh + 1) * blk] for hh in range(Q_PER_KV)]
    o_ref[...] = jnp.concatenate(outs, axis=0).T.astype(BF16)


def _moba_prefill(proj, k_aug, v_t, kmean, seg_mat, g_row, bsz, seq):
    nblk = seq // MOBA_BLOCK
    cols = Q_PER_KV * MOBA_BLOCK
    return pl.pallas_call(
        functools.partial(_moba_prefill_kernel, nblk=nblk),
        grid=(bsz, nblk),
        in_specs=[pl.BlockSpec((MOBA_BLOCK, MIX_W), lambda b, i: (b * nblk + i, 0)),
                  pl.BlockSpec((MOBA_KV_HEADS, seq, KAUG_W), lambda b, i: (0, b, 0)),
                  pl.BlockSpec((MOBA_KV_HEADS * VAUG_H, seq), lambda b, i: (0, b)),
                  pl.BlockSpec((1, nblk, KV_W), lambda b, i: (b, 0, 0)),
                  pl.BlockSpec((MIX_W, MIX_W), lambda b, i: (0, 0)),
                  pl.BlockSpec((1, MIX_W), lambda b, i: (0, 0))],
        out_specs=pl.BlockSpec((MOBA_BLOCK, MIX_W), lambda b, i: (b * nblk + i, 0)),
        out_shape=jax.ShapeDtypeStruct((bsz * seq, MIX_W), BF16),
        scratch_shapes=[pltpu.VMEM((MOBA_KV_HEADS, KAUG_W, cols), BF16),
                        pltpu.VMEM((2, MOBA_KV_HEADS, MOBA_BLOCK, cols), F32),
                        pltpu.VMEM((2, MOBA_KV_HEADS, 1, cols), F32),
                        pltpu.VMEM((2, MOBA_KV_HEADS, 1, cols), F32),
                        pltpu.VMEM((MOBA_KV_HEADS, 1, cols), F32),
                        pltpu.VMEM((MOBA_KV_HEADS, VAUG_H, cols), F32)],
        compiler_params=_cparams("parallel", "arbitrary"),
        name="moba_prefill",
    )(proj, k_aug, v_t, kmean, seg_mat, g_row)


def _moba_decode_kernel(pt_ref, qt_ref, knt_ref, vnt_ref, *rest, n_pages, page):
    k_pages = rest[:n_pages]
    v_pages = rest[n_pages:2 * n_pages]
    ot_ref, s_scr, p_scr, own_scr = rest[2 * n_pages:]
    del pt_ref
    b = pl.program_id(0)
    n = qt_ref.shape[1]
    nblk = n_pages * page // MOBA_BLOCK

    @pl.when(b == 0)
    def _():
        ot_ref[...] = jnp.zeros_like(ot_ref)

    qrep = _column_lanes(qt_ref, b, page)
    knrep = _column_lanes(knt_ref, b, page)
    vnrep = _column_lanes(vnt_ref, b, page)

    s_scr[MOBA_HEADS:, :] = jnp.zeros((s_scr.shape[0] - MOBA_HEADS, s_scr.shape[1]), F32)
    own_scr[MOBA_HEADS:, :] = jnp.zeros((own_scr.shape[0] - MOBA_HEADS, page), F32)
    for h in range(MOBA_HEADS):
        g = h // Q_PER_KV
        qh = qrep[h * HEAD_DIM:(h + 1) * HEAD_DIM]
        for p in range(n_pages):
            s_scr[h:h + 1, p * page:(p + 1) * page] = jnp.sum(qh * k_pages[p][g], axis=0, keepdims=True)
        own_scr[h:h + 1, :] = jnp.sum(qh * knrep[g * HEAD_DIM:(g + 1) * HEAD_DIM], axis=0, keepdims=True)
    s = s_scr[...]

    gates = [jnp.sum(s[:, j * MOBA_BLOCK:(j + 1) * MOBA_BLOCK], axis=1, keepdims=True) for j in range(nblk)]
    parts = []
    for j in range(nblk):
        rank = jnp.zeros(gates[j].shape, jnp.int32)
        for jj in range(nblk):
            if jj != j:
                beats = (gates[jj] >= gates[j]) if jj < j else (gates[jj] > gates[j])
                rank = rank + jnp.where(beats, 1, 0)
        bias = jnp.where(rank < MOBA_TOPK, 0.0, MASK_VALUE)
        parts.append(s[:, j * MOBA_BLOCK:(j + 1) * MOBA_BLOCK] * ATTN_SCALE + bias)
    s = jnp.concatenate(parts, axis=1)
    s_own = own_scr[:, :1] * ATTN_SCALE
    m = jnp.maximum(jnp.max(s, axis=1, keepdims=True), s_own)
    pr = jnp.exp(s - m)
    p_own = jnp.exp(s_own - m)
    inv = 1.0 / (jnp.sum(pr, axis=1, keepdims=True) + p_own)
    p_scr[...] = pr * inv
    own_scr[...] = jnp.broadcast_to(p_own * inv * (1.0 / page), own_scr.shape)

    accs = []
    for h in range(MOBA_HEADS):
        g = h // Q_PER_KV
        acc = own_scr[h:h + 1, :] * vnrep[g * HEAD_DIM:(g + 1) * HEAD_DIM]
        for p in range(n_pages):
            acc = acc + p_scr[h:h + 1, p * page:(p + 1) * page] * v_pages[p][g]
        accs.append(acc)
    _add_lane_sum_column(ot_ref, jnp.concatenate(accs, axis=0), b)


def _column_lanes(xt_ref, col, lanes):
    n = xt_ref.shape[1]
    sel = (lax.broadcasted_iota(jnp.int32, (n, lanes), 0) == col).astype(F32)
    return jnp.dot(xt_ref[...], sel, precision=lax.Precision.HIGHEST, preferred_element_type=F32)


def _add_lane_sum_column(ot_ref, acc, col):
    place = (lax.broadcasted_iota(jnp.int32, (acc.shape[1], ot_ref.shape[1]), 1) == col).astype(BF16)
    hi = acc.astype(BF16)
    lo = (acc - hi.astype(F32)).astype(BF16)
    ot_ref[...] += (jnp.dot(hi, place, preferred_element_type=F32)
                    + jnp.dot(lo, place, preferred_element_type=F32))


def _moba_decode(q_t, knew_t, vnew_t, cache_kt, cache_vt, page_table):
    n = q_t.shape[1]
    page = cache_kt.shape[3]
    n_pages = page_table.shape[0] // n
    past = n_pages * page
    hrows = -(-MOBA_HEADS // 8) * 8

    def page_spec(p):
        return pl.BlockSpec((None, MOBA_KV_HEADS, HEAD_DIM, page),
                            lambda b, pt, p=p: (pt[b * n_pages + p], 0, 0, 0))

    const2 = lambda b, pt: (0, 0)
    grid_spec = pltpu.PrefetchScalarGridSpec(
        num_scalar_prefetch=1,
        grid=(n,),
        in_specs=([pl.BlockSpec((MIX_W, n), const2),
                   pl.BlockSpec((KV_W, n), const2),
                   pl.BlockSpec((KV_W, n), const2)]
                  + [page_spec(p) for p in range(n_pages)]
                  + [page_spec(p) for p in range(n_pages)]),
        out_specs=pl.BlockSpec((MIX_W, n), const2),
        scratch_shapes=[pltpu.VMEM((hrows, past), F32), pltpu.VMEM((hrows, past), F32),
                        pltpu.VMEM((hrows, page), F32)],
    )
    return pl.pallas_call(
        functools.partial(_moba_decode_kernel, n_pages=n_pages, page=page),
        grid_spec=grid_spec,
        out_shape=jax.ShapeDtypeStruct((MIX_W, n), F32),
        compiler_params=_cparams("arbitrary"),
        name="moba_decode",
    )(page_table, q_t, knew_t, vnew_t, *([cache_kt] * n_pages), *([cache_vt] * n_pages))


def _seg_matrix(width):
    head = jnp.arange(width) // HEAD_DIM
    return (head[:, None] == head[None, :]).astype(BF16) / HEAD_DIM


def _scan_perm():
    r = jnp.arange(SCAN_T)
    t = (r % SCAN_SUB) * SCAN_STEPS + r // SCAN_SUB
    return (t[:, None] == jnp.arange(SCAN_T)[None, :]).astype(BF16)


def _block_diag(x, row_axes):
    eye = jnp.eye(x.shape[1], dtype=x.dtype)
    if row_axes == "gh":
        full = jnp.einsum("kghp,gG->kghGp", x, eye)
    else:
        full = jnp.einsum("kghp,gG->kgpGh", x, eye)
    k, g, a, _, b = full.shape
    return full.reshape(k, g * a, g * b)


def _s5_constants(p, l, ab_re, ab_im, bb_re, bb_im):
    gpb = SSM_KB // SSM_GROUP
    shape4 = (N_KB, gpb, SSM_GROUP, SSM_STATE)
    bmat = jnp.concatenate([_block_diag(bb_re[l].reshape(shape4), "gh"),
                            _block_diag(bb_im[l].reshape(shape4), "gh")], axis=-1).astype(BF16)
    c_re = p["ssm_c_re"][l].reshape(shape4)
    c_im = p["ssm_c_im"][l].reshape(shape4)
    cmat = jnp.concatenate([_block_diag(c_re, "gp"), -_block_diag(c_im, "gp")], axis=1).astype(BF16)
    a_re = ab_re[l, ::SSM_GROUP].reshape(1, N_STATE)
    a_im = ab_im[l, ::SSM_GROUP].reshape(1, N_STATE)
    d_row = p["ssm_d"][l].reshape(1, MIX_W)
    return bmat, a_re, a_im, cmat, d_row, p["w_glu"][l].astype(BF16)


def kernel(x_prompt, x_sample, mem_prompt, state_ssm_re, state_ssm_im, cache_k, cache_v, page_table,
           cache_mem_k, cache_mem_v, g_mix, w_in, w_out, g_ffn, w_gu, w_down, ssm_a_re, ssm_a_im,
           ssm_log_dt, ssm_b_re, ssm_b_im, ssm_c_re, ssm_c_im, ssm_d, w_glu, g_q, g_mq, g_mem,
           w_mem_kv, g_mk, g_kv, w_kv, g_k):
    bsz, seq, _ = x_prompt.shape
    n_p = bsz * seq
    n_s = x_sample.shape[0]
    tokens = mem_prompt.shape[1]
    p = dict(ssm_c_re=ssm_c_re, ssm_c_im=ssm_c_im, ssm_d=ssm_d, w_glu=w_glu)

    w_in_b = w_in.astype(BF16)
    w_out_b = w_out.astype(BF16)
    w_gu_b = w_gu.astype(BF16)
    w_down_b = w_down.astype(BF16)
    seg_kv = _seg_matrix(KV_W)
    seg_mix = _seg_matrix(MIX_W)
    gq_rows = jnp.tile(g_q, (1, MOBA_HEADS))
    gmq_rows = jnp.tile(g_mq, (1, MEM_HEADS))

    rep = lambda a: jnp.repeat(a, SSM_GROUP, axis=1)
    b_rows = lambda b: jnp.swapaxes(b, 2, 3).reshape(N_A_LAYERS, MIX_W, SSM_STATE)
    ab_re, ab_im, bb_re, bb_im = _ssm_prep(rep(ssm_a_re), rep(ssm_a_im), rep(ssm_log_dt)[..., None],
                                           b_rows(ssm_b_re), b_rows(ssm_b_im))
    s5 = [_s5_constants(p, l, ab_re, ab_im, bb_re, bb_im) for l in range(N_A_LAYERS)]
    perm = _scan_perm()
    perm_t = perm.T

    mem_flat = mem_prompt.reshape(bsz * tokens, D_MODEL)
    pm_k, pm_v = _kv_proj(mem_flat, g_mem[:, None, :], w_mem_kv.astype(BF16), seg_kv,
                          jnp.tile(g_mk, (1, MEM_HEADS))[:, None, :], tm=256)
    p_mem_k = pm_k.reshape(DEPTH, bsz, tokens, MEM_HEADS, HEAD_DIM)
    p_mem_v = pm_v.reshape(DEPTH, bsz, tokens, MEM_HEADS, HEAD_DIM)
    pm_k_b = pm_k.reshape(DEPTH, bsz, tokens, MEM_W).astype(BF16)
    pm_v_b = pm_v.reshape(DEPTH, bsz, tokens, MEM_W).astype(BF16)

    g_kv_rows = g_kv.reshape(1, 1, D_MODEL)
    w_kv_b = w_kv.astype(BF16)[None]
    g_k_rows = jnp.tile(g_k, MOBA_KV_HEADS).reshape(1, 1, KV_W)

    h = x_prompt.reshape(n_p, D_MODEL)
    p_fin_re, p_fin_im = [], []
    k_aug = v_t = kmean = p_k = p_v = None
    nblk = seq // MOBA_BLOCK
    next_in = lambda l: (g_mix[l + 1][None], w_in_b[l + 1]) if l + 1 < DEPTH else None
    tail_weights = lambda l: (w_out_b[l, :MIX_W], w_out_b[l, MIX_W:], g_ffn[l][None], w_gu_b[l], w_down_b[l])
    proj = _norm_matmul(h, g_mix[0][None], w_in_b[0], tm=512)
    for l in range(DEPTH):
        if l < N_A_LAYERS:
            mix, f_re, f_im = _s5_scan(proj, bsz, seq, perm, perm_t, *s5[l])
            p_fin_re.append(f_re.reshape(bsz, SSM_GROUPS, SSM_STATE))
            p_fin_im.append(f_im.reshape(bsz, SSM_GROUPS, SSM_STATE))
        else:
            mix = _moba_prefill(proj, k_aug, v_t, kmean, seg_mix, gq_rows[l - N_A_LAYERS][None], bsz, seq)
        mem = _mem_attn(proj, pm_k_b[l], pm_v_b[l], seg_kv, gmq_rows[l][None], bsz, seq, tq=512)
        h, proj = _layer_tail(h, mix, mem, *tail_weights(l), tm=512, next_in=next_in(l))
        if l == N_A_LAYERS - 1:
            k_new, v_new, k_aug, v_t, kmean = _kv_proj(h, g_kv_rows, w_kv_b, seg_kv, g_k_rows, tm=MOBA_BLOCK,
                                                       blocks_per_seq=nblk)
            p_k = k_new.reshape(bsz, seq, MOBA_KV_HEADS, HEAD_DIM)
            p_v = v_new.reshape(bsz, seq, MOBA_KV_HEADS, HEAD_DIM)
            kmean = kmean.reshape(bsz, nblk, KV_W)
    y_prompt = h.reshape(bsz, seq, D_MODEL)

    ck_t = cache_k.transpose(0, 2, 3, 1)
    cv_t = cache_v.transpose(0, 2, 3, 1)
    cmk_t = cache_mem_k.transpose(0, 1, 3, 4, 2)
    cmv_t = cache_mem_v.transpose(0, 1, 3, 4, 2)

    h = x_sample.reshape(n_s, D_MODEL)
    s_fin_re, s_fin_im = [], []
    s_k = s_v = None
    proj = _norm_matmul(h, g_mix[0][None], w_in_b[0], tm=n_s)
    for l in range(DEPTH):
        if l < N_A_LAYERS:
            mix, x_re, x_im = _s5_step(proj, state_ssm_re[l].reshape(n_s, N_STATE),
                                       state_ssm_im[l].reshape(n_s, N_STATE), *s5[l])
            s_fin_re.append(x_re.reshape(n_s, SSM_GROUPS, SSM_STATE))
            s_fin_im.append(x_im.reshape(n_s, SSM_GROUPS, SSM_STATE))
            mq_t, = _decode_prep(proj, seg_kv, gmq_rows[l][None])
        else:
            mq_t, q_t, knew_t, vnew_t = _decode_prep(
                proj, seg_kv, gmq_rows[l][None],
                moba=(seg_mix, gq_rows[l - N_A_LAYERS][None], s_k.reshape(n_s, KV_W), s_v.reshape(n_s, KV_W)))
            mix = _moba_decode(q_t, knew_t, vnew_t, ck_t, cv_t, page_table.reshape(-1)).T.astype(BF16)
        mem = _mem_decode(mq_t, cmk_t, cmv_t, l).T.astype(BF16)
        h, proj = _layer_tail(h, mix, mem, *tail_weights(l), tm=n_s, next_in=next_in(l))
        if l == N_A_LAYERS - 1:
            s_k, s_v = _kv_proj(h, g_kv_rows, w_kv_b, seg_kv, g_k_rows, tm=n_s)
    y_sample = h.reshape(n_s, 1, D_MODEL)

    return (y_prompt, y_sample,
            jnp.stack(p_fin_re), jnp.stack(p_fin_im), p_k, p_v, p_mem_k, p_mem_v,
            jnp.stack(s_fin_re), jnp.stack(s_fin_im),
            s_k.reshape(n_s, 1, MOBA_KV_HEADS, HEAD_DIM), s_v.reshape(n_s, 1, MOBA_KV_HEADS, HEAD_DIM))
```

```python
import functools
import math

import jax
import jax.numpy as jnp
from jax import lax
from jax.experimental import pallas as pl
from jax.experimental.pallas import tpu as pltpu

F32 = jnp.float32
BF16 = jnp.bfloat16

D_MODEL = 1024
DEPTH = 4
N_A_LAYERS = DEPTH // 2
HEAD_DIM = 64
MIX_W = 3 * D_MODEL // 4
MEM_W = D_MODEL // 4
MEM_HEADS = MEM_W // HEAD_DIM
SSM_GROUP = 16
SSM_GROUPS = MIX_W // SSM_GROUP
SSM_STATE = 64
N_STATE = SSM_GROUPS * SSM_STATE
MOBA_HEADS = MIX_W // HEAD_DIM
MOBA_KV_HEADS = 4
Q_PER_KV = MOBA_HEADS // MOBA_KV_HEADS
KV_W = MOBA_KV_HEADS * HEAD_DIM
MOBA_BLOCK = 256
MOBA_TOPK = 3
D_FF = ((-(-8 * D_MODEL // 3) + 255) // 256) * 256
EPS = 1e-6
ATTN_SCALE = HEAD_DIM ** -0.5
MASK_VALUE = -1e30
KAUG_W = 128
VAUG_H = HEAD_DIM + 16
LOG2_E = math.log2(math.e)

SCAN_SUB = 8
SCAN_STEPS = 32
SCAN_T = SCAN_SUB * SCAN_STEPS
SSM_KB = 256
N_KB = MIX_W // SSM_KB
KB_STATES = (SSM_KB // SSM_GROUP) * SSM_STATE
SCAN_CW = 512

VMEM_LIMIT_BYTES = 56 * 1024 * 1024


def _cparams(*sem):
    return pltpu.CompilerParams(dimension_semantics=sem, vmem_limit_bytes=VMEM_LIMIT_BYTES)


def _rms(x, g_row):
    ms = jnp.mean(x * x, axis=-1, keepdims=True)
    return x * lax.rsqrt(ms + EPS) * g_row


def _seg_rms(x, seg_mat, g_row):
    sq = x * x
    hi = sq.astype(BF16)
    lo = (sq - hi.astype(F32)).astype(BF16)
    ms = (jnp.dot(hi, seg_mat, preferred_element_type=F32)
          + jnp.dot(lo, seg_mat, preferred_element_type=F32))
    return x * lax.rsqrt(ms + EPS) * g_row


def _dot_nt(a, b, precision=None):
    return lax.dot_general(a, b, (((1,), (1,)), ((), ())), precision=precision,
                           preferred_element_type=F32)


def _topk_keep(gate, n_valid, axis):
    n = gate.shape[axis]
    idx = lax.broadcasted_iota(jnp.int32, gate.shape, axis)
    rank = jnp.zeros(gate.shape, jnp.int32)
    for j in range(n):
        gj = lax.slice_in_dim(gate, j, j + 1, axis=axis)
        beats = (gj > gate) | ((gj == gate) & (j < idx))
        rank = rank + jnp.where(beats, jnp.where(j < n_valid, 1, 0), 0)
    return (rank < MOBA_TOPK) & (idx < n_valid)


def _norm_matmul_kernel(x_ref, g_ref, w_ref, o_ref):
    a = _rms(x_ref[...], g_ref[...]).astype(BF16)
    o_ref[...] = jnp.dot(a, w_ref[...], preferred_element_type=F32)


def _norm_matmul(x, g_row, w, tm):
    n, d = x.shape
    nout = w.shape[1]
    return pl.pallas_call(
        _norm_matmul_kernel,
        grid=(n // tm,),
        in_specs=[pl.BlockSpec((tm, d), lambda i: (i, 0)),
                  pl.BlockSpec((1, d), lambda i: (0, 0)),
                  pl.BlockSpec((d, nout), lambda i: (0, 0))],
        out_specs=pl.BlockSpec((tm, nout), lambda i: (i, 0)),
        out_shape=jax.ShapeDtypeStruct((n, nout), F32),
        compiler_params=_cparams("parallel"),
        name="norm_matmul",
    )(x, g_row, w)


def _kv_kernel(x_ref, g_ref, w_ref, seg_ref, gk_ref, k_ref, v_ref, *extra, half, blocks_per_seq):
    a = _rms(x_ref[...], g_ref[0]).astype(BF16)
    kv = jnp.dot(a, w_ref[0], preferred_element_type=F32)
    k = _seg_rms(kv[:, :half], seg_ref[...], gk_ref[0])
    v = kv[:, half:]
    k_ref[0] = k
    v_ref[0] = v
    if blocks_per_seq:
        kaug_ref, vt_ref, kmean_ref = extra
        lane = lax.broadcasted_iota(jnp.int32, (k.shape[0], KAUG_W), 1)
        tail = jnp.where(lane == HEAD_DIM + pl.program_id(1) % blocks_per_seq, 1.0, 0.0)
        for g in range(MOBA_KV_HEADS):
            col = k[:, (g // 2) * KAUG_W:(g // 2 + 1) * KAUG_W]
            if g % 2:
                col = pltpu.roll(col, HEAD_DIM, axis=1)
            kaug_ref[g] = jnp.where(lane < HEAD_DIM, col, tail).astype(BF16)
        v_t = v.T
        ones_row = jnp.where(lax.broadcasted_iota(jnp.int32, (VAUG_H - HEAD_DIM, k.shape[0]), 0) == 0, 1.0, 0.0)
        vt_ref[...] = jnp.concatenate(
            [piece for g in range(MOBA_KV_HEADS) for piece in (v_t[g * HEAD_DIM:(g + 1) * HEAD_DIM], ones_row)],
            axis=0).astype(BF16)
        kmean_ref[0] = jnp.mean(k, axis=0, keepdims=True)


def _kv_proj(x, g_rows, w, seg_mat, gk_rows, tm, blocks_per_seq=0):
    n, d = x.shape
    layers, _, two_half = w.shape
    half = two_half // 2
    out_shape = [jax.ShapeDtypeStruct((layers, n, half), F32)] * 2
    out_specs = [pl.BlockSpec((1, tm, half), lambda l, i: (l, i, 0))] * 2
    if blocks_per_seq:
        assert layers == 1 and tm == MOBA_BLOCK and half == KV_W
        out_shape += [jax.ShapeDtypeStruct((MOBA_KV_HEADS, n, KAUG_W), BF16),
                      jax.ShapeDtypeStruct((MOBA_KV_HEADS * VAUG_H, n), BF16),
                      jax.ShapeDtypeStruct((n // tm, 1, half), F32)]
        out_specs += [pl.BlockSpec((MOBA_KV_HEADS, tm, KAUG_W), lambda l, i: (0, i, 0)),
                      pl.BlockSpec((MOBA_KV_HEADS * VAUG_H, tm), lambda l, i: (0, i)),
                      pl.BlockSpec((1, 1, half), lambda l, i: (i, 0, 0))]
    return pl.pallas_call(
        functools.partial(_kv_kernel, half=half, blocks_per_seq=blocks_per_seq),
        grid=(layers, n // tm),
        in_specs=[pl.BlockSpec((tm, d), lambda l, i: (i, 0)),
                  pl.BlockSpec((1, 1, d), lambda l, i: (l, 0, 0)),
                  pl.BlockSpec((1, d, two_half), lambda l, i: (l, 0, 0)),
                  pl.BlockSpec((half, half), lambda l, i: (0, 0)),
                  pl.BlockSpec((1, 1, half), lambda l, i: (l, 0, 0))],
        out_specs=out_specs,
        out_shape=out_shape,
        compiler_params=_cparams("parallel", "parallel"),
        name="kv_proj",
    )(x, g_rows, w, seg_mat, gk_rows)


def _layer_tail_kernel(h_ref, mix_ref, mem_ref, wmix_ref, wmem_ref, g_ref, wgu_ref, wd_ref, *rest):
    h1 = (h_ref[...]
          + jnp.dot(mix_ref[...], wmix_ref[...], preferred_element_type=F32)
          + jnp.dot(mem_ref[...], wmem_ref[...], preferred_element_type=F32))
    a = _rms(h1, g_ref[...]).astype(BF16)
    gate = jnp.dot(a, wgu_ref[:, :D_FF], preferred_element_type=F32)
    up = jnp.dot(a, wgu_ref[:, D_FF:], preferred_element_type=F32)
    act = (gate * jax.nn.sigmoid(gate) * up).astype(BF16)
    h2 = h1 + jnp.dot(act, wd_ref[...], preferred_element_type=F32)
    if len(rest) == 1:
        rest[0][...] = h2
    else:
        gn_ref, win_ref, o_ref, proj_ref = rest
        o_ref[...] = h2
        proj_ref[...] = jnp.dot(_rms(h2, gn_ref[...]).astype(BF16), win_ref[...], preferred_element_type=F32)


def _layer_tail(h, mix, mem, w_mix, w_mem, g_row, w_gu, w_down, tm, next_in=None):
    n, d = h.shape
    row = lambda w: pl.BlockSpec((tm, w), lambda i: (i, 0))
    resident = lambda a: pl.BlockSpec(a.shape, lambda i: (0,) * a.ndim, pipeline_mode=pl.Buffered(1))
    args = [h, mix, mem, w_mix, w_mem, g_row, w_gu, w_down]
    in_specs = [row(d), row(MIX_W), row(MEM_W)] + [resident(a) for a in args[3:]]
    out_shape = [jax.ShapeDtypeStruct((n, d), F32)]
    if next_in is not None:
        args += list(next_in)
        in_specs += [resident(a) for a in next_in]
        out_shape.append(jax.ShapeDtypeStruct((n, next_in[1].shape[1]), F32))
    out = pl.pallas_call(
        _layer_tail_kernel,
        grid=(n // tm,),
        in_specs=in_specs,
        out_specs=[row(s.shape[1]) for s in out_shape],
        out_shape=out_shape,
        compiler_params=_cparams("parallel"),
        name="layer_tail",
    )(*args)
    return out if next_in is not None else (out[0], None)


def _ssm_prep_kernel(are_ref, aim_ref, ldt_ref, bre_ref, bim_ref, abr_ref, abi_ref, bbr_ref, bbi_ref):
    dt = jnp.exp(ldt_ref[0])
    lam_re = jnp.minimum(are_ref[0], -1e-4)
    lam_im = aim_ref[0]
    mag = jnp.exp(dt * lam_re)
    ang = dt * lam_im
    ab_re = mag * jnp.cos(ang)
    ab_im = mag * jnp.sin(ang)
    den = lam_re * lam_re + lam_im * lam_im
    num_re = ab_re - 1.0
    f_re = (num_re * lam_re + ab_im * lam_im) / den
    f_im = (ab_im * lam_re - num_re * lam_im) / den
    b_re = bre_ref[0]
    b_im = bim_ref[0]
    abr_ref[0] = ab_re
    abi_ref[0] = ab_im
    bbr_ref[0] = f_re * b_re - f_im * b_im
    bbi_ref[0] = f_re * b_im + f_im * b_re


def _ssm_prep(a_re, a_im, log_dt, b_re, b_im):
    layers = a_re.shape[0]
    big = pl.BlockSpec((1, MIX_W, SSM_STATE), lambda l: (l, 0, 0))
    return pl.pallas_call(
        _ssm_prep_kernel,
        grid=(layers,),
        in_specs=[big, big, pl.BlockSpec((1, MIX_W, 1), lambda l: (l, 0, 0)), big, big],
        out_specs=[big] * 4,
        out_shape=[jax.ShapeDtypeStruct((layers, MIX_W, SSM_STATE), F32)] * 4,
        compiler_params=_cparams("parallel"),
        name="ssm_prep",
    )(a_re, a_im, log_dt, b_re, b_im)


def _re_cols(n0):
    kb, off = divmod(n0, KB_STATES)
    return kb * 2 * KB_STATES + off


def _s5_output(y_state, u, d_ref, wglu_ref):
    y = jax.nn.gelu(y_state + d_ref[...] * u)
    z = jnp.dot(y.astype(BF16), wglu_ref[...], preferred_element_type=F32)
    return (y * jax.nn.sigmoid(z)).astype(BF16)


def _s5_scan_kernel(u_ref, perm_ref, permt_ref, bmat_ref, are_ref, aim_ref, cmat_ref, d_ref, wglu_ref,
                    y_ref, fre_ref, fim_ref, x_scr, apr_scr, api_scr, str_scr, sti_scr):
    c = pl.program_id(1)

    @pl.when(c == 0)
    def _():
        str_scr[...] = jnp.zeros_like(str_scr)
        sti_scr[...] = jnp.zeros_like(sti_scr)
        ar, ai = are_ref[...], aim_ref[...]
        pr, pi = ar, ai
        for i in range(SCAN_STEPS):
            apr_scr[i:i + 1, :] = pr
            api_scr[i:i + 1, :] = pi
            pr, pi = pr * ar - pi * ai, pr * ai + pi * ar

    u = u_ref[...]
    up = jnp.dot(perm_ref[...], u.astype(BF16), preferred_element_type=F32).astype(BF16)
    sub = lax.broadcasted_iota(jnp.int32, (SCAN_SUB, SCAN_CW), 0)
    ys = []
    for kb in range(N_KB):
        x_scr[:, kb * 2 * KB_STATES:(kb + 1) * 2 * KB_STATES] = jnp.dot(
            up[:, kb * SSM_KB:(kb + 1) * SSM_KB], bmat_ref[kb], preferred_element_type=F32)
        for n0 in range(kb * KB_STATES, (kb + 1) * KB_STATES, SCAN_CW):
            rc = _re_cols(n0)
            ic = rc + KB_STATES
            ar = jnp.broadcast_to(are_ref[:, n0:n0 + SCAN_CW], (SCAN_SUB, SCAN_CW))
            ai = jnp.broadcast_to(aim_ref[:, n0:n0 + SCAN_CW], (SCAN_SUB, SCAN_CW))
            xr = jnp.zeros((SCAN_SUB, SCAN_CW), F32)
            xi = jnp.zeros((SCAN_SUB, SCAN_CW), F32)
            for i in range(SCAN_STEPS):
                r0 = i * SCAN_SUB
                xr, xi = (ar * xr - ai * xi + x_scr[r0:r0 + SCAN_SUB, rc:rc + SCAN_CW],
                          ar * xi + ai * xr + x_scr[r0:r0 + SCAN_SUB, ic:ic + SCAN_CW])
                x_scr[r0:r0 + SCAN_SUB, rc:rc + SCAN_CW] = xr
                x_scr[r0:r0 + SCAN_SUB, ic:ic + SCAN_CW] = xi

            a_t_r = apr_scr[SCAN_STEPS - 1:SCAN_STEPS, n0:n0 + SCAN_CW]
            a_t_i = api_scr[SCAN_STEPS - 1:SCAN_STEPS, n0:n0 + SCAN_CW]
            cr = str_scr[:, n0:n0 + SCAN_CW]
            ci = sti_scr[:, n0:n0 + SCAN_CW]
            car_r = jnp.zeros((SCAN_SUB, SCAN_CW), F32)
            car_i = jnp.zeros((SCAN_SUB, SCAN_CW), F32)
            for j in range(SCAN_SUB):
                car_r = jnp.where(sub == j, cr, car_r)
                car_i = jnp.where(sub == j, ci, car_i)
                cr, ci = (xr[j:j + 1] + a_t_r * cr - a_t_i * ci,
                          xi[j:j + 1] + a_t_r * ci + a_t_i * cr)
            str_scr[:, n0:n0 + SCAN_CW] = cr
            sti_scr[:, n0:n0 + SCAN_CW] = ci

            for i in range(SCAN_STEPS):
                r0 = i * SCAN_SUB
                pr = apr_scr[i:i + 1, n0:n0 + SCAN_CW]
                pi = api_scr[i:i + 1, n0:n0 + SCAN_CW]
                x_scr[r0:r0 + SCAN_SUB, rc:rc + SCAN_CW] = (x_scr[r0:r0 + SCAN_SUB, rc:rc + SCAN_CW]
                                                            + (pr * car_r - pi * car_i))
                x_scr[r0:r0 + SCAN_SUB, ic:ic + SCAN_CW] = (x_scr[r0:r0 + SCAN_SUB, ic:ic + SCAN_CW]
                                                            + (pr * car_i + pi * car_r))
        ys.append(jnp.dot(x_scr[:, kb * 2 * KB_STATES:(kb + 1) * 2 * KB_STATES].astype(BF16), cmat_ref[kb],
                          preferred_element_type=F32))
    yp = jnp.concatenate(ys, axis=1)
    hi = yp.astype(BF16)
    lo = (yp - hi.astype(F32)).astype(BF16)
    y_state = (jnp.dot(permt_ref[...], hi, preferred_element_type=F32)
               + jnp.dot(permt_ref[...], lo, preferred_element_type=F32))
    y_ref[...] = _s5_output(y_state, u, d_ref, wglu_ref)
    fre_ref[0] = str_scr[...]
    fim_ref[0] = sti_scr[...]


def _s5_scan(proj, bsz, seq, perm, perm_t, bmat, a_re, a_im, cmat, d_row, w_glu):
    nchunk = seq // SCAN_T
    const2 = lambda b, c: (0, 0)
    const3 = lambda b, c: (0, 0, 0)
    state_spec = pl.BlockSpec((1, 1, N_STATE), lambda b, c: (b, 0, 0))
    return pl.pallas_call(
        _s5_scan_kernel,
        grid=(bsz, nchunk),
        in_specs=[pl.BlockSpec((SCAN_T, MIX_W), lambda b, c: (b * nchunk + c, 0)),
                  pl.BlockSpec((SCAN_T, SCAN_T), const2),
                  pl.BlockSpec((SCAN_T, SCAN_T), const2),
                  pl.BlockSpec((N_KB, SSM_KB, 2 * KB_STATES), const3),
                  pl.BlockSpec((1, N_STATE), const2),
                  pl.BlockSpec((1, N_STATE), const2),
                  pl.BlockSpec((N_KB, 2 * KB_STATES, SSM_KB), const3),
                  pl.BlockSpec((1, MIX_W), const2),
                  pl.BlockSpec((MIX_W, MIX_W), const2)],
        out_specs=[pl.BlockSpec((SCAN_T, MIX_W), lambda b, c: (b * nchunk + c, 0)), state_spec, state_spec],
        out_shape=[jax.ShapeDtypeStruct((bsz * seq, MIX_W), BF16),
                   jax.ShapeDtypeStruct((bsz, 1, N_STATE), F32),
                   jax.ShapeDtypeStruct((bsz, 1, N_STATE), F32)],
        scratch_shapes=[pltpu.VMEM((SCAN_T, 2 * N_STATE), F32),
                        pltpu.VMEM((SCAN_STEPS, N_STATE), F32),
                        pltpu.VMEM((SCAN_STEPS, N_STATE), F32),
                        pltpu.VMEM((1, N_STATE), F32),
                        pltpu.VMEM((1, N_STATE), F32)],
        compiler_params=_cparams("parallel", "arbitrary"),
        name="s5_scan",
    )(proj, perm, perm_t, bmat, a_re, a_im, cmat, d_row, w_glu)


def _s5_step_kernel(u_ref, hre_ref, him_ref, bmat_ref, are_ref, aim_ref, cmat_ref, d_ref, wglu_ref,
                    y_ref, xre_ref, xim_ref):
    u = u_ref[...]
    ub = u.astype(BF16)
    ys = []
    for kb in range(N_KB):
        bu = jnp.dot(ub[:, kb * SSM_KB:(kb + 1) * SSM_KB], bmat_ref[kb], preferred_element_type=F32)
        n0 = kb * KB_STATES
        ar, ai = are_ref[:, n0:n0 + KB_STATES], aim_ref[:, n0:n0 + KB_STATES]
        hr, hi = hre_ref[:, n0:n0 + KB_STATES], him_ref[:, n0:n0 + KB_STATES]
        xr = ar * hr - ai * hi + bu[:, :KB_STATES]
        xi = ar * hi + ai * hr + bu[:, KB_STATES:]
        xre_ref[:, n0:n0 + KB_STATES] = xr
        xim_ref[:, n0:n0 + KB_STATES] = xi
        x = jnp.concatenate([xr, xi], axis=1).astype(BF16)
        ys.append(jnp.dot(x, cmat_ref[kb], preferred_element_type=F32))
    y_ref[...] = _s5_output(jnp.concatenate(ys, axis=1), u, d_ref, wglu_ref)


def _s5_step(proj, h_re, h_im, bmat, a_re, a_im, cmat, d_row, w_glu):
    n = proj.shape[0]
    const2 = lambda i: (0, 0)
    const3 = lambda i: (0, 0, 0)
    return pl.pallas_call(
        _s5_step_kernel,
        grid=(1,),
        in_specs=[pl.BlockSpec((n, MIX_W), const2),
                  pl.BlockSpec((n, N_STATE), const2),
                  pl.BlockSpec((n, N_STATE), const2),
                  pl.BlockSpec((N_KB, SSM_KB, 2 * KB_STATES), const3),
                  pl.BlockSpec((1, N_STATE), const2),
                  pl.BlockSpec((1, N_STATE), const2),
                  pl.BlockSpec((N_KB, 2 * KB_STATES, SSM_KB), const3),
                  pl.BlockSpec((1, MIX_W), const2),
                  pl.BlockSpec((MIX_W, MIX_W), const2)],
        out_specs=[pl.BlockSpec((n, MIX_W), const2),
                   pl.BlockSpec((n, N_STATE), const2),
                   pl.BlockSpec((n, N_STATE), const2)],
        out_shape=[jax.ShapeDtypeStruct((n, MIX_W), BF16),
                   jax.ShapeDtypeStruct((n, N_STATE), F32),
                   jax.ShapeDtypeStruct((n, N_STATE), F32)],
        compiler_params=_cparams("arbitrary"),
        name="s5_step",
    )(proj, h_re, h_im, bmat, a_re, a_im, cmat, d_row, w_glu)


def _mem_attn_kernel(q_ref, k_ref, v_ref, seg_ref, g_ref, o_ref):
    q = _seg_rms(q_ref[...], seg_ref[...], g_ref[...]) * ATTN_SCALE
    k = k_ref[0]
    v = v_ref[0]
    lane_head = lax.broadcasted_iota(jnp.int32, q.shape, 1) // HEAD_DIM
    out = jnp.zeros(q.shape, F32)
    for h in range(MEM_HEADS):
        qh = jnp.where(lane_head == h, q, 0.0).astype(BF16)
        s = _dot_nt(qh, k)
        p = jnp.exp(s - jnp.max(s, axis=-1, keepdims=True))
        o = jnp.dot(p.astype(BF16), v, preferred_element_type=F32) / jnp.sum(p, axis=-1, keepdims=True)
        out = jnp.where(lane_head == h, o, out)
    o_ref[...] = out.astype(BF16)


def _mem_attn(proj, mem_k, mem_v, seg_mat, g_row, bsz, seq, tq):
    nq = seq // tq
    tokens = mem_k.shape[1]
    qcol = MIX_W // MEM_W
    kv_spec = pl.BlockSpec((1, tokens, MEM_W), lambda b, i: (b, 0, 0))
    return pl.pallas_call(
        _mem_attn_kernel,
        grid=(bsz, nq),
        in_specs=[pl.BlockSpec((tq, MEM_W), lambda b, i: (b * nq + i, qcol)),
                  kv_spec, kv_spec,
                  pl.BlockSpec((MEM_W, MEM_W), lambda b, i: (0, 0)),
                  pl.BlockSpec((1, MEM_W), lambda b, i: (0, 0))],
        out_specs=pl.BlockSpec((tq, MEM_W), lambda b, i: (b * nq + i, 0)),
        out_shape=jax.ShapeDtypeStruct((bsz * seq, MEM_W), BF16),
        compiler_params=_cparams("parallel", "parallel"),
        name="mem_attn",
    )(proj, mem_k, mem_v, seg_mat, g_row)


def _decode_prep_kernel(proj_ref, segk_ref, gmq_ref, *rest, with_moba):
    x = proj_ref[...]
    if with_moba:
        segm_ref, gq_ref, knew_ref, vnew_ref, mqt_ref, qt_ref, knt_ref, vnt_ref = rest
        qt_ref[...] = _seg_rms(x[:, :MIX_W], segm_ref[...], gq_ref[...]).T
        knt_ref[...] = knew_ref[...].T
        vnt_ref[...] = vnew_ref[...].T
    else:
        mqt_ref, = rest
    mqt_ref[...] = (_seg_rms(x[:, MIX_W:], segk_ref[...], gmq_ref[...]) * ATTN_SCALE).T


def _decode_prep(proj, seg_kv, gmq_row, moba=None):
    n, d = proj.shape
    const2 = lambda i: (0, 0)
    args = [proj, seg_kv, gmq_row]
    in_specs = [pl.BlockSpec((n, d), const2), pl.BlockSpec((MEM_W, MEM_W), const2), pl.BlockSpec((1, MEM_W), const2)]
    out_shape = [jax.ShapeDtypeStruct((MEM_W, n), F32)]
    if moba is not None:
        args += list(moba)
        in_specs += [pl.BlockSpec((MIX_W, MIX_W), const2), pl.BlockSpec((1, MIX_W), const2),
                     pl.BlockSpec((n, KV_W), const2), pl.BlockSpec((n, KV_W), const2)]
        out_shape += [jax.ShapeDtypeStruct((MIX_W, n), F32), jax.ShapeDtypeStruct((KV_W, n), F32),
                      jax.ShapeDtypeStruct((KV_W, n), F32)]
    return pl.pallas_call(
        functools.partial(_decode_prep_kernel, with_moba=moba is not None),
        grid=(1,),
        in_specs=in_specs,
        out_specs=[pl.BlockSpec(s.shape, const2) for s in out_shape],
        out_shape=out_shape,
        compiler_params=_cparams("arbitrary"),
        name="decode_prep",
    )(*args)


def _mem_decode_kernel(qt_ref, kt_ref, vt_ref, ot_ref, s_scr, p_scr, *, rows):
    i = pl.program_id(0)
    tokens = kt_ref.shape[-1]
    lanes = 128
    nchunk = tokens // lanes

    @pl.when(i == 0)
    def _():
        ot_ref[...] = jnp.zeros_like(ot_ref)

    for r in range(rows):
        qrep = _column_lanes(qt_ref, i * rows + r, lanes)
        for h in range(MEM_HEADS):
            qh = qrep[h * HEAD_DIM:(h + 1) * HEAD_DIM]
            for c in range(nchunk):
                s_scr[r * MEM_HEADS + h:r * MEM_HEADS + h + 1, c * lanes:(c + 1) * lanes] = jnp.sum(
                    qh * kt_ref[r, h, :, c * lanes:(c + 1) * lanes], axis=0, keepdims=True)
    s = s_scr[...]
    pr = jnp.exp(s - jnp.max(s, axis=1, keepdims=True))
    p_scr[...] = pr / jnp.sum(pr, axis=1, keepdims=True)
    for r in range(rows):
        accs = [p_scr[r * MEM_HEADS + h:r * MEM_HEADS + h + 1, :] * vt_ref[r, h] for h in range(MEM_HEADS)]
        _add_lane_sum_column(ot_ref, jnp.concatenate(accs, axis=0), i * rows + r)


def _mem_decode(mq_t, mem_kt, mem_vt, layer, rows=8):
    n = mq_t.shape[1]
    tokens = mem_kt.shape[-1]
    kv_spec = pl.BlockSpec((None, rows, MEM_HEADS, HEAD_DIM, tokens), lambda i: (layer, i, 0, 0, 0))
    return pl.pallas_call(
        functools.partial(_mem_decode_kernel, rows=rows),
        grid=(n // rows,),
        in_specs=[pl.BlockSpec((MEM_W, n), lambda i: (0, 0)), kv_spec, kv_spec],
        out_specs=pl.BlockSpec((MEM_W, n), lambda i: (0, 0)),
        out_shape=jax.ShapeDtypeStruct((MEM_W, n), F32),
        scratch_shapes=[pltpu.VMEM((rows * MEM_HEADS, tokens), F32), pltpu.VMEM((rows * MEM_HEADS, tokens), F32)],
        compiler_params=_cparams("arbitrary"),
        name="mem_decode",
    )(mq_t, mem_kt, mem_vt)


def _moba_prefill_kernel(q_ref, k_ref, vt_ref, kmean_ref, seg_ref, g_ref, o_ref,
                         qa_scr, s_scr, mcur_scr, alpha_scr, m_scr, acc_scr, *, nblk):
    i = pl.program_id(1)
    blk = MOBA_BLOCK
    qn_t = _seg_rms(q_ref[...], seg_ref[...], g_ref[...]).T
    qs_t = (qn_t * (ATTN_SCALE * LOG2_E)).astype(BF16)
    kmean = kmean_ref[0]
    cols = Q_PER_KV * blk
    key_idx = lax.broadcasted_iota(jnp.int32, (blk, cols), 0)
    row_idx = lax.broadcasted_iota(jnp.int32, (blk, cols), 1) % blk
    causal = key_idx <= row_idx
    pad_t = jnp.zeros((KAUG_W - HEAD_DIM - nblk, blk), BF16)

    def start_of(n):
        return pl.multiple_of(jnp.where(n == 0, i, n - 1) * blk, blk)

    def score_phase(n, slot, g, diagonal=False):
        s = jnp.dot(k_ref[g, pl.ds(start_of(n), blk), :], qa_scr[g], preferred_element_type=F32)
        if diagonal:
            s = jnp.where(causal, s, MASK_VALUE)
        s_scr[slot, g] = s
        m_old = m_scr[g]
        m_new = jnp.maximum(m_old, jnp.max(s, axis=0, keepdims=True))
        alpha_scr[slot, g] = jnp.exp2(m_old - m_new)
        mcur_scr[slot, g] = m_new
        m_scr[g] = m_new

    def value_phase(n, slot, g):
        p = jnp.exp2(s_scr[slot, g] - mcur_scr[slot, g]).astype(BF16)
        acc_scr[g] = alpha_scr[slot, g] * acc_scr[g] + jnp.dot(
            vt_ref[g * VAUG_H:(g + 1) * VAUG_H, pl.ds(start_of(n), blk)], p,
            preferred_element_type=F32)

    m_scr[...] = jnp.full(m_scr.shape, MASK_VALUE, F32)
    acc_scr[...] = jnp.zeros_like(acc_scr)
    for g in range(MOBA_KV_HEADS):
        km = kmean[:, g * HEAD_DIM:(g + 1) * HEAD_DIM]
        parts = []
        for hh in range(Q_PER_KV):
            h = g * Q_PER_KV + hh
            gate_t = jnp.dot(km, qn_t[h * HEAD_DIM:(h + 1) * HEAD_DIM], precision=lax.Precision.HIGHEST,
                             preferred_element_type=F32)
            keep_t = _topk_keep(gate_t, i, axis=0)
            past_t = lax.broadcasted_iota(jnp.int32, gate_t.shape, 0) < i
            bias_t = jnp.where(past_t & ~keep_t, MASK_VALUE, 0.0).astype(BF16)
            parts.append(jnp.concatenate([qs_t[h * HEAD_DIM:(h + 1) * HEAD_DIM], bias_t, pad_t], axis=0))
        qa_scr[g] = jnp.concatenate(parts, axis=1)
        score_phase(0, 0, g, diagonal=True)

    def pair_step(t, _):
        for g in range(MOBA_KV_HEADS):
            value_phase(2 * t, 0, g)
            score_phase(2 * t + 1, 1, g)
        for g in range(MOBA_KV_HEADS):
            value_phase(2 * t + 1, 1, g)
            score_phase(2 * t + 2, 0, g)
        return 0

    lax.fori_loop(0, i // 2, pair_step, 0)

    @pl.when(i % 2 == 1)
    def _():
        for g in range(MOBA_KV_HEADS):
            value_phase(i - 1, 0, g)
            score_phase(i, 1, g)
        for g in range(MOBA_KV_HEADS):
            value_phase(i, 1, g)

    @pl.when(i % 2 == 0)
    def _():
        for g in range(MOBA_KV_HEADS):
            value_phase(i, 0, g)

    outs = []
    for g in range(MOBA_KV_HEADS):
        out_t = acc_scr[g, :HEAD_DIM] / acc_scr[g, HEAD_DIM:HEAD_DIM + 1]
        outs += [out_t[:, hh * blk:(hh + 1) * blk] for hh in range(Q_PER_KV)]
    o_ref[...] = jnp.concatenate(outs, axis=0).T.astype(BF16)


def _moba_prefill(proj, k_aug, v_t, kmean, seg_mat, g_row, bsz, seq):
    nblk = seq // MOBA_BLOCK
    cols = Q_PER_KV * MOBA_BLOCK
    return pl.pallas_call(
        functools.partial(_moba_prefill_kernel, nblk=nblk),
        grid=(bsz, nblk),
        in_specs=[pl.BlockSpec((MOBA_BLOCK, MIX_W), lambda b, i: (b * nblk + i, 0)),
                  pl.BlockSpec((MOBA_KV_HEADS, seq, KAUG_W), lambda b, i: (0, b, 0)),
                  pl.BlockSpec((MOBA_KV_HEADS * VAUG_H, seq), lambda b, i: (0, b)),
                  pl.BlockSpec((1, nblk, KV_W), lambda b, i: (b, 0, 0)),
                  pl.BlockSpec((MIX_W, MIX_W), lambda b, i: (0, 0)),
                  pl.BlockSpec((1, MIX_W), lambda b, i: (0, 0))],
        out_specs=pl.BlockSpec((MOBA_BLOCK, MIX_W), lambda b, i: (b * nblk + i, 0)),
        out_shape=jax.ShapeDtypeStruct((bsz * seq, MIX_W), BF16),
        scratch_shapes=[pltpu.VMEM((MOBA_KV_HEADS, KAUG_W, cols), BF16),
                        pltpu.VMEM((2, MOBA_KV_HEADS, MOBA_BLOCK, cols), F32),
                        pltpu.VMEM((2, MOBA_KV_HEADS, 1, cols), F32),
                        pltpu.VMEM((2, MOBA_KV_HEADS, 1, cols), F32),
                        pltpu.VMEM((MOBA_KV_HEADS, 1, cols), F32),
                        pltpu.VMEM((MOBA_KV_HEADS, VAUG_H, cols), F32)],
        compiler_params=_cparams("parallel", "arbitrary"),
        name="moba_prefill",
    )(proj, k_aug, v_t, kmean, seg_mat, g_row)


def _moba_decode_kernel(pt_ref, qt_ref, knt_ref, vnt_ref, *rest, n_pages, page):
    k_pages = rest[:n_pages]
    v_pages = rest[n_pages:2 * n_pages]
    ot_ref, s_scr, p_scr, own_scr = rest[2 * n_pages:]
    del pt_ref
    b = pl.program_id(0)
    n = qt_ref.shape[1]
    nblk = n_pages * page // MOBA_BLOCK

    @pl.when(b == 0)
    def _():
        ot_ref[...] = jnp.zeros_like(ot_ref)

    qrep = _column_lanes(qt_ref, b, page)
    knrep = _column_lanes(knt_ref, b, page)
    vnrep = _column_lanes(vnt_ref, b, page)

    s_scr[MOBA_HEADS:, :] = jnp.zeros((s_scr.shape[0] - MOBA_HEADS, s_scr.shape[1]), F32)
    own_scr[MOBA_HEADS:, :] = jnp.zeros((own_scr.shape[0] - MOBA_HEADS, page), F32)
    for h in range(MOBA_HEADS):
        g = h // Q_PER_KV
        qh = qrep[h * HEAD_DIM:(h + 1) * HEAD_DIM]
        for p in range(n_pages):
            s_scr[h:h + 1, p * page:(p + 1) * page] = jnp.sum(qh * k_pages[p][g], axis=0, keepdims=True)
        own_scr[h:h + 1, :] = jnp.sum(qh * knrep[g * HEAD_DIM:(g + 1) * HEAD_DIM], axis=0, keepdims=True)
    s = s_scr[...]

    gates = [jnp.sum(s[:, j * MOBA_BLOCK:(j + 1) * MOBA_BLOCK], axis=1, keepdims=True) for j in range(nblk)]
    parts = []
    for j in range(nblk):
        rank = jnp.zeros(gates[j].shape, jnp.int32)
        for jj in range(nblk):
            if jj != j:
                beats = (gates[jj] >= gates[j]) if jj < j else (gates[jj] > gates[j])
                rank = rank + jnp.where(beats, 1, 0)
        bias = jnp.where(rank < MOBA_TOPK, 0.0, MASK_VALUE)
        parts.append(s[:, j * MOBA_BLOCK:(j + 1) * MOBA_BLOCK] * ATTN_SCALE + bias)
    s = jnp.concatenate(parts, axis=1)
    s_own = own_scr[:, :1] * ATTN_SCALE
    m = jnp.maximum(jnp.max(s, axis=1, keepdims=True), s_own)
    pr = jnp.exp(s - m)
    p_own = jnp.exp(s_own - m)
    inv = 1.0 / (jnp.sum(pr, axis=1, keepdims=True) + p_own)
    p_scr[...] = pr * inv
    own_scr[...] = jnp.broadcast_to(p_own * inv * (1.0 / page), own_scr.shape)

    accs = []
    for h in range(MOBA_HEADS):
        g = h // Q_PER_KV
        acc = own_scr[h:h + 1, :] * vnrep[g * HEAD_DIM:(g + 1) * HEAD_DIM]
        for p in range(n_pages):
            acc = acc + p_scr[h:h + 1, p * page:(p + 1) * page] * v_pages[p][g]
        accs.append(acc)
    _add_lane_sum_column(ot_ref, jnp.concatenate(accs, axis=0), b)


def _column_lanes(xt_ref, col, lanes):
    n = xt_ref.shape[1]
    sel = (lax.broadcasted_iota(jnp.int32, (n, lanes), 0) == col).astype(F32)
    return jnp.dot(xt_ref[...], sel, precision=lax.Precision.HIGHEST, preferred_element_type=F32)


def _add_lane_sum_column(ot_ref, acc, col):
    place = (lax.broadcasted_iota(jnp.int32, (acc.shape[1], ot_ref.shape[1]), 1) == col).astype(BF16)
    hi = acc.astype(BF16)
    lo = (acc - hi.astype(F32)).astype(BF16)
    ot_ref[...] += (jnp.dot(hi, place, preferred_element_type=F32)
                    + jnp.dot(lo, place, preferred_element_type=F32))


def _moba_decode(q_t, knew_t, vnew_t, cache_kt, cache_vt, page_table):
    n = q_t.shape[1]
    page = cache_kt.shape[3]
    n_pages = page_table.shape[0] // n
    past = n_pages * page
    hrows = -(-MOBA_HEADS // 8) * 8

    def page_spec(p):
        return pl.BlockSpec((None, MOBA_KV_HEADS, HEAD_DIM, page),
                            lambda b, pt, p=p: (pt[b * n_pages + p], 0, 0, 0))

    const2 = lambda b, pt: (0, 0)
    grid_spec = pltpu.PrefetchScalarGridSpec(
        num_scalar_prefetch=1,
        grid=(n,),
        in_specs=([pl.BlockSpec((MIX_W, n), const2),
                   pl.BlockSpec((KV_W, n), const2),
                   pl.BlockSpec((KV_W, n), const2)]
                  + [page_spec(p) for p in range(n_pages)]
                  + [page_spec(p) for p in range(n_pages)]),
        out_specs=pl.BlockSpec((MIX_W, n), const2),
        scratch_shapes=[pltpu.VMEM((hrows, past), F32), pltpu.VMEM((hrows, past), F32),
                        pltpu.VMEM((hrows, page), F32)],
    )
    return pl.pallas_call(
        functools.partial(_moba_decode_kernel, n_pages=n_pages, page=page),
        grid_spec=grid_spec,
        out_shape=jax.ShapeDtypeStruct((MIX_W, n), F32),
        compiler_params=_cparams("arbitrary"),
        name="moba_decode",
    )(page_table, q_t, knew_t, vnew_t, *([cache_kt] * n_pages), *([cache_vt] * n_pages))


def _seg_matrix(width):
    head = jnp.arange(width) // HEAD_DIM
    return (head[:, None] == head[None, :]).astype(BF16) / HEAD_DIM


def _scan_perm():
    r = jnp.arange(SCAN_T)
    t = (r % SCAN_SUB) * SCAN_STEPS + r // SCAN_SUB
    return (t[:, None] == jnp.arange(SCAN_T)[None, :]).astype(BF16)


def _block_diag(x, row_axes):
    eye = jnp.eye(x.shape[1], dtype=x.dtype)
    if row_axes == "gh":
        full = jnp.einsum("kghp,gG->kghGp", x, eye)
    else:
        full = jnp.einsum("kghp,gG->kgpGh", x, eye)
    k, g, a, _, b = full.shape
    return full.reshape(k, g * a, g * b)


def _s5_constants(p, l, ab_re, ab_im, bb_re, bb_im):
    gpb = SSM_KB // SSM_GROUP
    shape4 = (N_KB, gpb, SSM_GROUP, SSM_STATE)
    bmat = jnp.concatenate([_block_diag(bb_re[l].reshape(shape4), "gh"),
                            _block_diag(bb_im[l].reshape(shape4), "gh")], axis=-1).astype(BF16)
    c_re = p["ssm_c_re"][l].reshape(shape4)
    c_im = p["ssm_c_im"][l].reshape(shape4)
    cmat = jnp.concatenate([_block_diag(c_re, "gp"), -_block_diag(c_im, "gp")], axis=1).astype(BF16)
    a_re = ab_re[l, ::SSM_GROUP].reshape(1, N_STATE)
    a_im = ab_im[l, ::SSM_GROUP].reshape(1, N_STATE)
    d_row = p["ssm_d"][l].reshape(1, MIX_W)
    return bmat, a_re, a_im, cmat, d_row, p["w_glu"][l].astype(BF16)


def kernel(x_prompt, x_sample, mem_prompt, state_ssm_re, state_ssm_im, cache_k, cache_v, page_table,
           cache_mem_k, cache_mem_v, g_mix, w_in, w_out, g_ffn, w_gu, w_down, ssm_a_re, ssm_a_im,
           ssm_log_dt, ssm_b_re, ssm_b_im, ssm_c_re, ssm_c_im, ssm_d, w_glu, g_q, g_mq, g_mem,
           w_mem_kv, g_mk, g_kv, w_kv, g_k):
    bsz, seq, _ = x_prompt.shape
    n_p = bsz * seq
    n_s = x_sample.shape[0]
    tokens = mem_prompt.shape[1]
    p = dict(ssm_c_re=ssm_c_re, ssm_c_im=ssm_c_im, ssm_d=ssm_d, w_glu=w_glu)

    w_in_b = w_in.astype(BF16)
    w_out_b = w_out.astype(BF16)
    w_gu_b = w_gu.astype(BF16)
    w_down_b = w_down.astype(BF16)
    seg_kv = _seg_matrix(KV_W)
    seg_mix = _seg_matrix(MIX_W)
    gq_rows = jnp.tile(g_q, (1, MOBA_HEADS))
    gmq_rows = jnp.tile(g_mq, (1, MEM_HEADS))

    rep = lambda a: jnp.repeat(a, SSM_GROUP, axis=1)
    b_rows = lambda b: jnp.swapaxes(b, 2, 3).reshape(N_A_LAYERS, MIX_W, SSM_STATE)
    ab_re, ab_im, bb_re, bb_im = _ssm_prep(rep(ssm_a_re), rep(ssm_a_im), rep(ssm_log_dt)[..., None],
                                           b_rows(ssm_b_re), b_rows(ssm_b_im))
    s5 = [_s5_constants(p, l, ab_re, ab_im, bb_re, bb_im) for l in range(N_A_LAYERS)]
    perm = _scan_perm()
    perm_t = perm.T

    mem_flat = mem_prompt.reshape(bsz * tokens, D_MODEL)
    pm_k, pm_v = _kv_proj(mem_flat, g_mem[:, None, :], w_mem_kv.astype(BF16), seg_kv,
                          jnp.tile(g_mk, (1, MEM_HEADS))[:, None, :], tm=256)
    p_mem_k = pm_k.reshape(DEPTH, bsz, tokens, MEM_HEADS, HEAD_DIM)
    p_mem_v = pm_v.reshape(DEPTH, bsz, tokens, MEM_HEADS, HEAD_DIM)
    pm_k_b = pm_k.reshape(DEPTH, bsz, tokens, MEM_W).astype(BF16)
    pm_v_b = pm_v.reshape(DEPTH, bsz, tokens, MEM_W).astype(BF16)

    g_kv_rows = g_kv.reshape(1, 1, D_MODEL)
    w_kv_b = w_kv.astype(BF16)[None]
    g_k_rows = jnp.tile(g_k, MOBA_KV_HEADS).reshape(1, 1, KV_W)

    h = x_prompt.reshape(n_p, D_MODEL)
    p_fin_re, p_fin_im = [], []
    k_aug = v_t = kmean = p_k = p_v = None
    nblk = seq // MOBA_BLOCK
    next_in = lambda l: (g_mix[l + 1][None], w_in_b[l + 1]) if l + 1 < DEPTH else None
    tail_weights = lambda l: (w_out_b[l, :MIX_W], w_out_b[l, MIX_W:], g_ffn[l][None], w_gu_b[l], w_down_b[l])
    proj = _norm_matmul(h, g_mix[0][None], w_in_b[0], tm=512)
    for l in range(DEPTH):
        if l < N_A_LAYERS:
            mix, f_re, f_im = _s5_scan(proj, bsz, seq, perm, perm_t, *s5[l])
            p_fin_re.append(f_re.reshape(bsz, SSM_GROUPS, SSM_STATE))
            p_fin_im.append(f_im.reshape(bsz, SSM_GROUPS, SSM_STATE))
        else:
            mix = _moba_prefill(proj, k_aug, v_t, kmean, seg_mix, gq_rows[l - N_A_LAYERS][None], bsz, seq)
        mem = _mem_attn(proj, pm_k_b[l], pm_v_b[l], seg_kv, gmq_rows[l][None], bsz, seq, tq=512)
        h, proj = _layer_tail(h, mix, mem, *tail_weights(l), tm=512, next_in=next_in(l))
        if l == N_A_LAYERS - 1:
            k_new, v_new, k_aug, v_t, kmean = _kv_proj(h, g_kv_rows, w_kv_b, seg_kv, g_k_rows, tm=MOBA_BLOCK,
                                                       blocks_per_seq=nblk)
            p_k = k_new.reshape(bsz, seq, MOBA_KV_HEADS, HEAD_DIM)
            p_v = v_new.reshape(bsz, seq, MOBA_KV_HEADS, HEAD_DIM)
            kmean = kmean.reshape(bsz, nblk, KV_W)
    y_prompt = h.reshape(bsz, seq, D_MODEL)

    ck_t = cache_k.transpose(0, 2, 3, 1)
    cv_t = cache_v.transpose(0, 2, 3, 1)
    cmk_t = cache_mem_k.transpose(0, 1, 3, 4, 2)
    cmv_t = cache_mem_v.transpose(0, 1, 3, 4, 2)

    h = x_sample.reshape(n_s, D_MODEL)
    s_fin_re, s_fin_im = [], []
    s_k = s_v = None
    proj = _norm_matmul(h, g_mix[0][None], w_in_b[0], tm=n_s)
    for l in range(DEPTH):
        if l < N_A_LAYERS:
            mix, x_re, x_im = _s5_step(proj, state_ssm_re[l].reshape(n_s, N_STATE),
                                       state_ssm_im[l].reshape(n_s, N_STATE), *s5[l])
            s_fin_re.append(x_re.reshape(n_s, SSM_GROUPS, SSM_STATE))
            s_fin_im.append(x_im.reshape(n_s, SSM_GROUPS, SSM_STATE))
            mq_t, = _decode_prep(proj, seg_kv, gmq_rows[l][None])
        else:
            mq_t, q_t, knew_t, vnew_t = _decode_prep(
                proj, seg_kv, gmq_rows[l][None],
                moba=(seg_mix, gq_rows[l - N_A_LAYERS][None], s_k.reshape(n_s, KV_W), s_v.reshape(n_s, KV_W)))
            mix = _moba_decode(q_t, knew_t, vnew_t, ck_t, cv_t, page_table.reshape(-1)).T.astype(BF16)
        mem = _mem_decode(mq_t, cmk_t, cmv_t, l).T.astype(BF16)
        h, proj = _layer_tail(h, mix, mem, *tail_weights(l), tm=n_s, next_in=next_in(l))
        if l == N_A_LAYERS - 1:
            s_k, s_v = _kv_proj(h, g_kv_rows, w_kv_b, seg_kv, g_k_rows, tm=n_s)
    y_sample = h.reshape(n_s, 1, D_MODEL)

    return (y_prompt, y_sample,
            jnp.stack(p_fin_re), jnp.stack(p_fin_im), p_k, p_v, p_mem_k, p_mem_v,
            jnp.stack(s_fin_re), jnp.stack(s_fin_im),
            s_k.reshape(n_s, 1, MOBA_KV_HEADS, HEAD_DIM), s_v.reshape(n_s, 1, MOBA_KV_HEADS, HEAD_DIM))
```

```python
import functools
import math

import jax
import jax.numpy as jnp
from jax import lax
from jax.experimental import pallas as pl
from jax.experimental.pallas import tpu as pltpu

F32 = jnp.float32
BF16 = jnp.bfloat16

D_MODEL = 1024
DEPTH = 4
N_A_LAYERS = DEPTH // 2
HEAD_DIM = 64
MIX_W = 3 * D_MODEL // 4
MEM_W = D_MODEL // 4
MEM_HEADS = MEM_W // HEAD_DIM
SSM_GROUP = 16
SSM_GROUPS = MIX_W // SSM_GROUP
SSM_STATE = 64
N_STATE = SSM_GROUPS * SSM_STATE
MOBA_HEADS = MIX_W // HEAD_DIM
MOBA_KV_HEADS = 4
Q_PER_KV = MOBA_HEADS // MOBA_KV_HEADS
KV_W = MOBA_KV_HEADS * HEAD_DIM
MOBA_BLOCK = 256
MOBA_TOPK = 3
D_FF = ((-(-8 * D_MODEL // 3) + 255) // 256) * 256
EPS = 1e-6
ATTN_SCALE = HEAD_DIM ** -0.5
MASK_VALUE = -1e30
KAUG_W = 128
VAUG_H = HEAD_DIM + 16
LOG2_E = math.log2(math.e)

SCAN_SUB = 8
SCAN_STEPS = 32
SCAN_T = SCAN_SUB * SCAN_STEPS
SSM_KB = 256
N_KB = MIX_W // SSM_KB
KB_STATES = (SSM_KB // SSM_GROUP) * SSM_STATE
SCAN_CW = 512

VMEM_LIMIT_BYTES = 56 * 1024 * 1024


def _cparams(*sem):
    return pltpu.CompilerParams(dimension_semantics=sem, vmem_limit_bytes=VMEM_LIMIT_BYTES)


def _rms(x, g_row):
    ms = jnp.mean(x * x, axis=-1, keepdims=True)
    return x * lax.rsqrt(ms + EPS) * g_row


def _seg_rms(x, seg_mat, g_row):
    sq = x * x
    hi = sq.astype(BF16)
    lo = (sq - hi.astype(F32)).astype(BF16)
    ms = (jnp.dot(hi, seg_mat, preferred_element_type=F32)
          + jnp.dot(lo, seg_mat, preferred_element_type=F32))
    return x * lax.rsqrt(ms + EPS) * g_row


def _dot_nt(a, b, precision=None):
    return lax.dot_general(a, b, (((1,), (1,)), ((), ())), precision=precision,
                           preferred_element_type=F32)


def _topk_keep(gate, n_valid, axis):
    n = gate.shape[axis]
    idx = lax.broadcasted_iota(jnp.int32, gate.shape, axis)
    rank = jnp.zeros(gate.shape, jnp.int32)
    for j in range(n):
        gj = lax.slice_in_dim(gate, j, j + 1, axis=axis)
        beats = (gj > gate) | ((gj == gate) & (j < idx))
        rank = rank + jnp.where(beats, jnp.where(j < n_valid, 1, 0), 0)
    return (rank < MOBA_TOPK) & (idx < n_valid)


def _norm_matmul_kernel(x_ref, g_ref, w_ref, o_ref):
    a = _rms(x_ref[...], g_ref[...]).astype(BF16)
    o_ref[...] = jnp.dot(a, w_ref[...], preferred_element_type=F32)


def _norm_matmul(x, g_rows, w, layer, tm):
    n, d = x.shape
    nout = w.shape[2]
    return pl.pallas_call(
        _norm_matmul_kernel,
        grid=(n // tm,),
        in_specs=[pl.BlockSpec((tm, d), lambda i: (i, 0)),
                  pl.BlockSpec((None, 1, d), lambda i: (layer, 0, 0)),
                  pl.BlockSpec((None, d, nout), lambda i: (layer, 0, 0))],
        out_specs=pl.BlockSpec((tm, nout), lambda i: (i, 0)),
        out_shape=jax.ShapeDtypeStruct((n, nout), F32),
        compiler_params=_cparams("parallel"),
        name="norm_matmul",
    )(x, g_rows, w)


def _kv_kernel(x_ref, g_ref, w_ref, seg_ref, gk_ref, k_ref, v_ref, *extra, half, blocks_per_seq):
    a = _rms(x_ref[...], g_ref[0]).astype(BF16)
    kv = jnp.dot(a, w_ref[0], preferred_element_type=F32)
    k = _seg_rms(kv[:, :half], seg_ref[...], gk_ref[0])
    v = kv[:, half:]
    if not blocks_per_seq:
        k_ref[0] = k
        v_ref[0] = v
    else:
        kaug_ref, vt_ref, kmean_ref = extra
        lane = lax.broadcasted_iota(jnp.int32, (k.shape[0], KAUG_W), 1)
        tail = jnp.where(lane == HEAD_DIM + pl.program_id(1) % blocks_per_seq, 1.0, 0.0)
        for g in range(MOBA_KV_HEADS):
            col = k[:, (g // 2) * KAUG_W:(g // 2 + 1) * KAUG_W]
            if g % 2:
                col = pltpu.roll(col, HEAD_DIM, axis=1)
            kaug_ref[g] = jnp.where(lane < HEAD_DIM, col, tail).astype(BF16)
        v_t = v.T
        k_ref[0] = k.T
        v_ref[0] = v_t
        ones_row = jnp.where(lax.broadcasted_iota(jnp.int32, (VAUG_H - HEAD_DIM, k.shape[0]), 0) == 0, 1.0, 0.0)
        vt_ref[...] = jnp.concatenate(
            [piece for g in range(MOBA_KV_HEADS) for piece in (v_t[g * HEAD_DIM:(g + 1) * HEAD_DIM], ones_row)],
            axis=0).astype(BF16)
        kmean_ref[0] = jnp.mean(k, axis=0, keepdims=True)


def _kv_proj(x, g_rows, w, seg_mat, gk_rows, tm, blocks_per_seq=0):
    n, d = x.shape
    layers, _, two_half = w.shape
    half = two_half // 2
    out_shape = [jax.ShapeDtypeStruct((layers, n, half), F32)] * 2
    out_specs = [pl.BlockSpec((1, tm, half), lambda l, i: (l, i, 0))] * 2
    if blocks_per_seq:
        assert layers == 1 and tm == MOBA_BLOCK and half == KV_W
        nseq = n // (tm * blocks_per_seq)
        out_shape = [jax.ShapeDtypeStruct((nseq, half, tm * blocks_per_seq), F32)] * 2
        out_specs = [pl.BlockSpec((1, half, tm), lambda l, i: (i // blocks_per_seq, 0, i % blocks_per_seq))] * 2
        out_shape += [jax.ShapeDtypeStruct((MOBA_KV_HEADS, n, KAUG_W), BF16),
                      jax.ShapeDtypeStruct((MOBA_KV_HEADS * VAUG_H, n), BF16),
                      jax.ShapeDtypeStruct((n // tm, 1, half), F32)]
        out_specs += [pl.BlockSpec((MOBA_KV_HEADS, tm, KAUG_W), lambda l, i: (0, i, 0)),
                      pl.BlockSpec((MOBA_KV_HEADS * VAUG_H, tm), lambda l, i: (0, i)),
                      pl.BlockSpec((1, 1, half), lambda l, i: (i, 0, 0))]
    return pl.pallas_call(
        functools.partial(_kv_kernel, half=half, blocks_per_seq=blocks_per_seq),
        grid=(layers, n // tm),
        in_specs=[pl.BlockSpec((tm, d), lambda l, i: (i, 0)),
                  pl.BlockSpec((1, 1, d), lambda l, i: (l, 0, 0)),
                  pl.BlockSpec((1, d, two_half), lambda l, i: (l, 0, 0)),
                  pl.BlockSpec((half, half), lambda l, i: (0, 0)),
                  pl.BlockSpec((1, 1, half), lambda l, i: (l, 0, 0))],
        out_specs=out_specs,
        out_shape=out_shape,
        compiler_params=_cparams("parallel", "parallel"),
        name="kv_proj",
    )(x, g_rows, w, seg_mat, gk_rows)


def _layer_tail_kernel(h_ref, mix_ref, mem_ref, wmix_ref, wmem_ref, g_ref, wgu_ref, wd_ref, *rest):
    h1 = (h_ref[...]
          + jnp.dot(mix_ref[...], wmix_ref[...], preferred_element_type=F32)
          + jnp.dot(mem_ref[...], wmem_ref[...], preferred_element_type=F32))
    a = _rms(h1, g_ref[...]).astype(BF16)
    gate = jnp.dot(a, wgu_ref[:, :D_FF], preferred_element_type=F32)
    up = jnp.dot(a, wgu_ref[:, D_FF:], preferred_element_type=F32)
    act = (gate * jax.nn.sigmoid(gate) * up).astype(BF16)
    h2 = h1 + jnp.dot(act, wd_ref[...], preferred_element_type=F32)
    if len(rest) == 1:
        rest[0][...] = h2
    else:
        gn_ref, win_ref, o_ref, proj_ref = rest
        o_ref[...] = h2
        proj_ref[...] = jnp.dot(_rms(h2, gn_ref[...]).astype(BF16), win_ref[...], preferred_element_type=F32)


def _layer_tail(h, mix, mem, layer, w_out, g_ffn, w_gu, w_down, tm, next_in=None):
    n, d = h.shape
    row = lambda w: pl.BlockSpec((tm, w), lambda i: (i, 0))

    def resident(a, l, rows=None, row_block=0):
        shape = (None, rows or a.shape[1], a.shape[2])
        return pl.BlockSpec(shape, lambda i: (l, row_block, 0), pipeline_mode=pl.Buffered(1))

    args = [h, mix, mem, w_out, w_out, g_ffn, w_gu, w_down]
    in_specs = [row(d), row(MIX_W), row(MEM_W),
                resident(w_out, layer, MIX_W), resident(w_out, layer, MEM_W, MIX_W // MEM_W),
                resident(g_ffn, layer), resident(w_gu, layer), resident(w_down, layer)]
    out_shape = [jax.ShapeDtypeStruct((n, d), F32)]
    if next_in is not None:
        args += list(next_in)
        in_specs += [resident(a, layer + 1) for a in next_in]
        out_shape.append(jax.ShapeDtypeStruct((n, next_in[1].shape[2]), F32))
    out = pl.pallas_call(
        _layer_tail_kernel,
        grid=(n // tm,),
        in_specs=in_specs,
        out_specs=[row(s.shape[1]) for s in out_shape],
        out_shape=out_shape,
        compiler_params=_cparams("parallel"),
        name="layer_tail",
    )(*args)
    return out if next_in is not None else (out[0], None)


def _ssm_prep_kernel(are_ref, aim_ref, ldt_ref, bre_ref, bim_ref, abr_ref, abi_ref, bbr_ref, bbi_ref):
    dt = jnp.exp(ldt_ref[0])
    lam_re = jnp.minimum(are_ref[0], -1e-4)
    lam_im = aim_ref[0]
    mag = jnp.exp(dt * lam_re)
    ang = dt * lam_im
    ab_re = mag * jnp.cos(ang)
    ab_im = mag * jnp.sin(ang)
    den = lam_re * lam_re + lam_im * lam_im
    num_re = ab_re - 1.0
    f_re = (num_re * lam_re + ab_im * lam_im) / den
    f_im = (ab_im * lam_re - num_re * lam_im) / den
    b_re = bre_ref[0]
    b_im = bim_ref[0]
    abr_ref[0] = ab_re
    abi_ref[0] = ab_im
    bbr_ref[0] = f_re * b_re - f_im * b_im
    bbi_ref[0] = f_re * b_im + f_im * b_re


def _ssm_prep(a_re, a_im, log_dt, b_re, b_im):
    layers = a_re.shape[0]
    big = pl.BlockSpec((1, MIX_W, SSM_STATE), lambda l: (l, 0, 0))
    return pl.pallas_call(
        _ssm_prep_kernel,
        grid=(layers,),
        in_specs=[big, big, pl.BlockSpec((1, MIX_W, 1), lambda l: (l, 0, 0)), big, big],
        out_specs=[big] * 4,
        out_shape=[jax.ShapeDtypeStruct((layers, MIX_W, SSM_STATE), F32)] * 4,
        compiler_params=_cparams("parallel"),
        name="ssm_prep",
    )(a_re, a_im, log_dt, b_re, b_im)


def _re_cols(n0):
    kb, off = divmod(n0, KB_STATES)
    return kb * 2 * KB_STATES + off


def _s5_output(y_state, u, d_ref, wglu_ref):
    y = jax.nn.gelu(y_state + d_ref[...] * u)
    z = jnp.dot(y.astype(BF16), wglu_ref[...], preferred_element_type=F32)
    return (y * jax.nn.sigmoid(z)).astype(BF16)


def _s5_scan_kernel(u_ref, perm_ref, permt_ref, bmat_ref, are_ref, aim_ref, cmat_ref, d_ref, wglu_ref,
                    y_ref, fre_ref, fim_ref, x_scr, apr_scr, api_scr, str_scr, sti_scr):
    c = pl.program_id(1)

    @pl.when(c == 0)
    def _():
        str_scr[...] = jnp.zeros_like(str_scr)
        sti_scr[...] = jnp.zeros_like(sti_scr)
        ar, ai = are_ref[...], aim_ref[...]
        pr, pi = ar, ai
        for i in range(SCAN_STEPS):
            apr_scr[i:i + 1, :] = pr
            api_scr[i:i + 1, :] = pi
            pr, pi = pr * ar - pi * ai, pr * ai + pi * ar

    u = u_ref[...]
    up = jnp.dot(perm_ref[...], u.astype(BF16), preferred_element_type=F32).astype(BF16)
    sub = lax.broadcasted_iota(jnp.int32, (SCAN_SUB, SCAN_CW), 0)
    ys = []
    for kb in range(N_KB):
        x_scr[:, kb * 2 * KB_STATES:(kb + 1) * 2 * KB_STATES] = jnp.dot(
            up[:, kb * SSM_KB:(kb + 1) * SSM_KB], bmat_ref[kb], preferred_element_type=F32)
        for n0 in range(kb * KB_STATES, (kb + 1) * KB_STATES, SCAN_CW):
            rc = _re_cols(n0)
            ic = rc + KB_STATES
            ar = jnp.broadcast_to(are_ref[:, n0:n0 + SCAN_CW], (SCAN_SUB, SCAN_CW))
            ai = jnp.broadcast_to(aim_ref[:, n0:n0 + SCAN_CW], (SCAN_SUB, SCAN_CW))
            xr = jnp.zeros((SCAN_SUB, SCAN_CW), F32)
            xi = jnp.zeros((SCAN_SUB, SCAN_CW), F32)
            for i in range(SCAN_STEPS):
                r0 = i * SCAN_SUB
                xr, xi = (ar * xr - ai * xi + x_scr[r0:r0 + SCAN_SUB, rc:rc + SCAN_CW],
                          ar * xi + ai * xr + x_scr[r0:r0 + SCAN_SUB, ic:ic + SCAN_CW])
                x_scr[r0:r0 + SCAN_SUB, rc:rc + SCAN_CW] = xr
                x_scr[r0:r0 + SCAN_SUB, ic:ic + SCAN_CW] = xi

            a_t_r = apr_scr[SCAN_STEPS - 1:SCAN_STEPS, n0:n0 + SCAN_CW]
            a_t_i = api_scr[SCAN_STEPS - 1:SCAN_STEPS, n0:n0 + SCAN_CW]
            cr = str_scr[:, n0:n0 + SCAN_CW]
            ci = sti_scr[:, n0:n0 + SCAN_CW]
            car_r = jnp.zeros((SCAN_SUB, SCAN_CW), F32)
            car_i = jnp.zeros((SCAN_SUB, SCAN_CW), F32)
            for j in range(SCAN_SUB):
                car_r = jnp.where(sub == j, cr, car_r)
                car_i = jnp.where(sub == j, ci, car_i)
                cr, ci = (xr[j:j + 1] + a_t_r * cr - a_t_i * ci,
                          xi[j:j + 1] + a_t_r * ci + a_t_i * cr)
            str_scr[:, n0:n0 + SCAN_CW] = cr
            sti_scr[:, n0:n0 + SCAN_CW] = ci

            for i in range(SCAN_STEPS):
                r0 = i * SCAN_SUB
                pr = apr_scr[i:i + 1, n0:n0 + SCAN_CW]
                pi = api_scr[i:i + 1, n0:n0 + SCAN_CW]
                x_scr[r0:r0 + SCAN_SUB, rc:rc + SCAN_CW] = (x_scr[r0:r0 + SCAN_SUB, rc:rc + SCAN_CW]
                                                            + (pr * car_r - pi * car_i))
                x_scr[r0:r0 + SCAN_SUB, ic:ic + SCAN_CW] = (x_scr[r0:r0 + SCAN_SUB, ic:ic + SCAN_CW]
                                                            + (pr * car_i + pi * car_r))
        ys.append(jnp.dot(x_scr[:, kb * 2 * KB_STATES:(kb + 1) * 2 * KB_STATES].astype(BF16), cmat_ref[kb],
                          preferred_element_type=F32))
    yp = jnp.concatenate(ys, axis=1)
    hi = yp.astype(BF16)
    lo = (yp - hi.astype(F32)).astype(BF16)
    y_state = (jnp.dot(permt_ref[...], hi, preferred_element_type=F32)
               + jnp.dot(permt_ref[...], lo, preferred_element_type=F32))
    y_ref[...] = _s5_output(y_state, u, d_ref, wglu_ref)
    fre_ref[0] = str_scr[...]
    fim_ref[0] = sti_scr[...]


def _s5_scan(proj, bsz, seq, perm, perm_t, bmat, a_re, a_im, cmat, d_row, w_glu):
    nchunk = seq // SCAN_T
    const2 = lambda b, c: (0, 0)
    const3 = lambda b, c: (0, 0, 0)
    state_spec = pl.BlockSpec((1, 1, N_STATE), lambda b, c: (b, 0, 0))
    return pl.pallas_call(
        _s5_scan_kernel,
        grid=(bsz, nchunk),
        in_specs=[pl.BlockSpec((SCAN_T, MIX_W), lambda b, c: (b * nchunk + c, 0)),
                  pl.BlockSpec((SCAN_T, SCAN_T), const2),
                  pl.BlockSpec((SCAN_T, SCAN_T), const2),
                  pl.BlockSpec((N_KB, SSM_KB, 2 * KB_STATES), const3),
                  pl.BlockSpec((1, N_STATE), const2),
                  pl.BlockSpec((1, N_STATE), const2),
                  pl.BlockSpec((N_KB, 2 * KB_STATES, SSM_KB), const3),
                  pl.BlockSpec((1, MIX_W), const2),
                  pl.BlockSpec((MIX_W, MIX_W), const2)],
        out_specs=[pl.BlockSpec((SCAN_T, MIX_W), lambda b, c: (b * nchunk + c, 0)), state_spec, state_spec],
        out_shape=[jax.ShapeDtypeStruct((bsz * seq, MIX_W), BF16),
                   jax.ShapeDtypeStruct((bsz, 1, N_STATE), F32),
                   jax.ShapeDtypeStruct((bsz, 1, N_STATE), F32)],
        scratch_shapes=[pltpu.VMEM((SCAN_T, 2 * N_STATE), F32),
                        pltpu.VMEM((SCAN_STEPS, N_STATE), F32),
                        pltpu.VMEM((SCAN_STEPS, N_STATE), F32),
                        pltpu.VMEM((1, N_STATE), F32),
                        pltpu.VMEM((1, N_STATE), F32)],
        compiler_params=_cparams("parallel", "arbitrary"),
        name="s5_scan",
    )(proj, perm, perm_t, bmat, a_re, a_im, cmat, d_row, w_glu)


def _s5_step_kernel(u_ref, hre_ref, him_ref, bmat_ref, are_ref, aim_ref, cmat_ref, d_ref, wglu_ref,
                    y_ref, xre_ref, xim_ref):
    u = u_ref[...]
    ub = u.astype(BF16)
    ys = []
    for kb in range(N_KB):
        bu = jnp.dot(ub[:, kb * SSM_KB:(kb + 1) * SSM_KB], bmat_ref[kb], preferred_element_type=F32)
        n0 = kb * KB_STATES
        ar, ai = are_ref[:, n0:n0 + KB_STATES], aim_ref[:, n0:n0 + KB_STATES]
        hr, hi = hre_ref[:, n0:n0 + KB_STATES], him_ref[:, n0:n0 + KB_STATES]
        xr = ar * hr - ai * hi + bu[:, :KB_STATES]
        xi = ar * hi + ai * hr + bu[:, KB_STATES:]
        xre_ref[:, n0:n0 + KB_STATES] = xr
        xim_ref[:, n0:n0 + KB_STATES] = xi
        x = jnp.concatenate([xr, xi], axis=1).astype(BF16)
        ys.append(jnp.dot(x, cmat_ref[kb], preferred_element_type=F32))
    y_ref[...] = _s5_output(jnp.concatenate(ys, axis=1), u, d_ref, wglu_ref)


def _s5_step(proj, h_re, h_im, bmat, a_re, a_im, cmat, d_row, w_glu):
    n = proj.shape[0]
    const2 = lambda i: (0, 0)
    const3 = lambda i: (0, 0, 0)
    return pl.pallas_call(
        _s5_step_kernel,
        grid=(1,),
        in_specs=[pl.BlockSpec((n, MIX_W), const2),
                  pl.BlockSpec((n, N_STATE), const2),
                  pl.BlockSpec((n, N_STATE), const2),
                  pl.BlockSpec((N_KB, SSM_KB, 2 * KB_STATES), const3),
                  pl.BlockSpec((1, N_STATE), const2),
                  pl.BlockSpec((1, N_STATE), const2),
                  pl.BlockSpec((N_KB, 2 * KB_STATES, SSM_KB), const3),
                  pl.BlockSpec((1, MIX_W), const2),
                  pl.BlockSpec((MIX_W, MIX_W), const2)],
        out_specs=[pl.BlockSpec((n, MIX_W), const2),
                   pl.BlockSpec((n, N_STATE), const2),
                   pl.BlockSpec((n, N_STATE), const2)],
        out_shape=[jax.ShapeDtypeStruct((n, MIX_W), BF16),
                   jax.ShapeDtypeStruct((n, N_STATE), F32),
                   jax.ShapeDtypeStruct((n, N_STATE), F32)],
        compiler_params=_cparams("arbitrary"),
        name="s5_step",
    )(proj, h_re, h_im, bmat, a_re, a_im, cmat, d_row, w_glu)


def _mem_attn_kernel(q_ref, k_ref, v_ref, seg_ref, g_ref, o_ref):
    q = _seg_rms(q_ref[...], seg_ref[...], g_ref[...]) * ATTN_SCALE
    k = k_ref[0]
    v = v_ref[0]
    lane_head = lax.broadcasted_iota(jnp.int32, q.shape, 1) // HEAD_DIM
    out = jnp.zeros(q.shape, F32)
    for h in range(MEM_HEADS):
        qh = jnp.where(lane_head == h, q, 0.0).astype(BF16)
        s = _dot_nt(qh, k)
        p = jnp.exp(s - jnp.max(s, axis=-1, keepdims=True))
        o = jnp.dot(p.astype(BF16), v, preferred_element_type=F32) / jnp.sum(p, axis=-1, keepdims=True)
        out = jnp.where(lane_head == h, o, out)
    o_ref[...] = out.astype(BF16)


def _mem_attn(proj, mem_k, mem_v, seg_mat, g_row, bsz, seq, tq):
    nq = seq // tq
    tokens = mem_k.shape[1]
    qcol = MIX_W // MEM_W
    kv_spec = pl.BlockSpec((1, tokens, MEM_W), lambda b, i: (b, 0, 0))
    return pl.pallas_call(
        _mem_attn_kernel,
        grid=(bsz, nq),
        in_specs=[pl.BlockSpec((tq, MEM_W), lambda b, i: (b * nq + i, qcol)),
                  kv_spec, kv_spec,
                  pl.BlockSpec((MEM_W, MEM_W), lambda b, i: (0, 0)),
                  pl.BlockSpec((1, MEM_W), lambda b, i: (0, 0))],
        out_specs=pl.BlockSpec((tq, MEM_W), lambda b, i: (b * nq + i, 0)),
        out_shape=jax.ShapeDtypeStruct((bsz * seq, MEM_W), BF16),
        compiler_params=_cparams("parallel", "parallel"),
        name="mem_attn",
    )(proj, mem_k, mem_v, seg_mat, g_row)


def _decode_prep_kernel(proj_ref, segk_ref, gmq_ref, *rest, with_moba):
    x = proj_ref[...]
    if with_moba:
        segm_ref, gq_ref, knew_ref, vnew_ref, mqt_ref, qt_ref, knt_ref, vnt_ref = rest
        qt_ref[...] = _seg_rms(x[:, :MIX_W], segm_ref[...], gq_ref[...]).T
        knt_ref[...] = knew_ref[...].T
        vnt_ref[...] = vnew_ref[...].T
    else:
        mqt_ref, = rest
    mqt_ref[...] = (_seg_rms(x[:, MIX_W:], segk_ref[...], gmq_ref[...]) * ATTN_SCALE).T


def _decode_prep(proj, seg_kv, gmq_row, moba=None):
    n, d = proj.shape
    const2 = lambda i: (0, 0)
    args = [proj, seg_kv, gmq_row]
    in_specs = [pl.BlockSpec((n, d), const2), pl.BlockSpec((MEM_W, MEM_W), const2), pl.BlockSpec((1, MEM_W), const2)]
    out_shape = [jax.ShapeDtypeStruct((MEM_W, n), F32)]
    if moba is not None:
        args += list(moba)
        in_specs += [pl.BlockSpec((MIX_W, MIX_W), const2), pl.BlockSpec((1, MIX_W), const2),
                     pl.BlockSpec((n, KV_W), const2), pl.BlockSpec((n, KV_W), const2)]
        out_shape += [jax.ShapeDtypeStruct((MIX_W, n), F32), jax.ShapeDtypeStruct((KV_W, n), F32),
                      jax.ShapeDtypeStruct((KV_W, n), F32)]
    return pl.pallas_call(
        functools.partial(_decode_prep_kernel, with_moba=moba is not None),
        grid=(1,),
        in_specs=in_specs,
        out_specs=[pl.BlockSpec(s.shape, const2) for s in out_shape],
        out_shape=out_shape,
        compiler_params=_cparams("arbitrary"),
        name="decode_prep",
    )(*args)


def _mem_decode_kernel(qt_ref, kt_ref, vt_ref, ot_ref, s_scr, p_scr, *, rows):
    i = pl.program_id(0)
    tokens = kt_ref.shape[-1]
    lanes = 128
    nchunk = tokens // lanes

    @pl.when(i == 0)
    def _():
        ot_ref[...] = jnp.zeros_like(ot_ref)

    for r in range(rows):
        qrep = _column_lanes(qt_ref, i * rows + r, lanes)
        for h in range(MEM_HEADS):
            qh = qrep[h * HEAD_DIM:(h + 1) * HEAD_DIM]
            for c in range(nchunk):
                s_scr[r * MEM_HEADS + h:r * MEM_HEADS + h + 1, c * lanes:(c + 1) * lanes] = jnp.sum(
                    qh * kt_ref[r, h, :, c * lanes:(c + 1) * lanes], axis=0, keepdims=True)
    s = s_scr[...]
    pr = jnp.exp(s - jnp.max(s, axis=1, keepdims=True))
    p_scr[...] = pr / jnp.sum(pr, axis=1, keepdims=True)
    for r in range(rows):
        accs = [p_scr[r * MEM_HEADS + h:r * MEM_HEADS + h + 1, :] * vt_ref[r, h] for h in range(MEM_HEADS)]
        _add_lane_sum_column(ot_ref, jnp.concatenate(accs, axis=0), i * rows + r)


def _mem_decode(mq_t, mem_kt, mem_vt, layer, rows=8):
    n = mq_t.shape[1]
    tokens = mem_kt.shape[-1]
    kv_spec = pl.BlockSpec((None, rows, MEM_HEADS, HEAD_DIM, tokens), lambda i: (layer, i, 0, 0, 0))
    return pl.pallas_call(
        functools.partial(_mem_decode_kernel, rows=rows),
        grid=(n // rows,),
        in_specs=[pl.BlockSpec((MEM_W, n), lambda i: (0, 0)), kv_spec, kv_spec],
        out_specs=pl.BlockSpec((MEM_W, n), lambda i: (0, 0)),
        out_shape=jax.ShapeDtypeStruct((MEM_W, n), F32),
        scratch_shapes=[pltpu.VMEM((rows * MEM_HEADS, tokens), F32), pltpu.VMEM((rows * MEM_HEADS, tokens), F32)],
        compiler_params=_cparams("arbitrary"),
        name="mem_decode",
    )(mq_t, mem_kt, mem_vt)


def _moba_prefill_kernel(q_ref, k_ref, vt_ref, kmean_ref, seg_ref, g_ref, o_ref,
                         qa_scr, s_scr, mcur_scr, alpha_scr, m_scr, acc_scr, *, nblk):
    i = pl.program_id(1)
    blk = MOBA_BLOCK
    qn_t = _seg_rms(q_ref[...], seg_ref[...], g_ref[...]).T
    qs_t = (qn_t * (ATTN_SCALE * LOG2_E)).astype(BF16)
    kmean = kmean_ref[0]
    cols = Q_PER_KV * blk
    key_idx = lax.broadcasted_iota(jnp.int32, (blk, cols), 0)
    row_idx = lax.broadcasted_iota(jnp.int32, (blk, cols), 1) % blk
    causal = key_idx <= row_idx
    pad_t = jnp.zeros((KAUG_W - HEAD_DIM - nblk, blk), BF16)

    def start_of(n):
        return pl.multiple_of(jnp.where(n == 0, i, n - 1) * blk, blk)

    def score_phase(n, slot, g, diagonal=False):
        s = jnp.dot(k_ref[g, pl.ds(start_of(n), blk), :], qa_scr[g], preferred_element_type=F32)
        if diagonal:
            s = jnp.where(causal, s, MASK_VALUE)
        s_scr[slot, g] = s
        m_old = m_scr[g]
        m_new = jnp.maximum(m_old, jnp.max(s, axis=0, keepdims=True))
        alpha_scr[slot, g] = jnp.exp2(m_old - m_new)
        mcur_scr[slot, g] = m_new
        m_scr[g] = m_new

    def value_phase(n, slot, g):
        p = jnp.exp2(s_scr[slot, g] - mcur_scr[slot, g]).astype(BF16)
        acc_scr[g] = alpha_scr[slot, g] * acc_scr[g] + jnp.dot(
            vt_ref[g * VAUG_H:(g + 1) * VAUG_H, pl.ds(start_of(n), blk)], p,
            preferred_element_type=F32)

    m_scr[...] = jnp.full(m_scr.shape, MASK_VALUE, F32)
    acc_scr[...] = jnp.zeros_like(acc_scr)
    for g in range(MOBA_KV_HEADS):
        km = kmean[:, g * HEAD_DIM:(g + 1) * HEAD_DIM]
        parts = []
        for hh in range(Q_PER_KV):
            h = g * Q_PER_KV + hh
            gate_t = jnp.dot(km, qn_t[h * HEAD_DIM:(h + 1) * HEAD_DIM], precision=lax.Precision.HIGHEST,
                             preferred_element_type=F32)
            keep_t = _topk_keep(gate_t, i, axis=0)
            past_t = lax.broadcasted_iota(jnp.int32, gate_t.shape, 0) < i
            bias_t = jnp.where(past_t & ~keep_t, MASK_VALUE, 0.0).astype(BF16)
            parts.append(jnp.concatenate([qs_t[h * HEAD_DIM:(h + 1) * HEAD_DIM], bias_t, pad_t], axis=0))
        qa_scr[g] = jnp.concatenate(parts, axis=1)
        score_phase(0, 0, g, diagonal=True)

    def pair_step(t, _):
        for g in range(MOBA_KV_HEADS):
            value_phase(2 * t, 0, g)
            score_phase(2 * t + 1, 1, g)
        for g in range(MOBA_KV_HEADS):
            value_phase(2 * t + 1, 1, g)
            score_phase(2 * t + 2, 0, g)
        return 0

    lax.fori_loop(0, i // 2, pair_step, 0)

    @pl.when(i % 2 == 1)
    def _():
        for g in range(MOBA_KV_HEADS):
            value_phase(i - 1, 0, g)
            score_phase(i, 1, g)
        for g in range(MOBA_KV_HEADS):
            value_phase(i, 1, g)

    @pl.when(i % 2 == 0)
    def _():
        for g in range(MOBA_KV_HEADS):
            value_phase(i, 0, g)

    outs = []
    for g in range(MOBA_KV_HEADS):
        out_t = acc_scr[g, :HEAD_DIM] / acc_scr[g, HEAD_DIM:HEAD_DIM + 1]
        outs += [out_t[:, hh * blk:(hh + 1) * blk] for hh in range(Q_PER_KV)]
    o_ref[...] = jnp.concatenate(outs, axis=0).T.astype(BF16)


def _moba_prefill(proj, k_aug, v_t, kmean, seg_mat, g_row, bsz, seq):
    nblk = seq // MOBA_BLOCK
    cols = Q_PER_KV * MOBA_BLOCK
    return pl.pallas_call(
        functools.partial(_moba_prefill_kernel, nblk=nblk),
        grid=(bsz, nblk),
        in_specs=[pl.BlockSpec((MOBA_BLOCK, MIX_W), lambda b, i: (b * nblk + i, 0)),
                  pl.BlockSpec((MOBA_KV_HEADS, seq, KAUG_W), lambda b, i: (0, b, 0)),
                  pl.BlockSpec((MOBA_KV_HEADS * VAUG_H, seq), lambda b, i: (0, b)),
                  pl.BlockSpec((1, nblk, KV_W), lambda b, i: (b, 0, 0)),
                  pl.BlockSpec((MIX_W, MIX_W), lambda b, i: (0, 0)),
                  pl.BlockSpec((1, MIX_W), lambda b, i: (0, 0))],
        out_specs=pl.BlockSpec((MOBA_BLOCK, MIX_W), lambda b, i: (b * nblk + i, 0)),
        out_shape=jax.ShapeDtypeStruct((bsz * seq, MIX_W), BF16),
        scratch_shapes=[pltpu.VMEM((MOBA_KV_HEADS, KAUG_W, cols), BF16),
                        pltpu.VMEM((2, MOBA_KV_HEADS, MOBA_BLOCK, cols), F32),
                        pltpu.VMEM((2, MOBA_KV_HEADS, 1, cols), F32),
                        pltpu.VMEM((2, MOBA_KV_HEADS, 1, cols), F32),
                        pltpu.VMEM((MOBA_KV_HEADS, 1, cols), F32),
                        pltpu.VMEM((MOBA_KV_HEADS, VAUG_H, cols), F32)],
        compiler_params=_cparams("parallel", "arbitrary"),
        name="moba_prefill",
    )(proj, k_aug, v_t, kmean, seg_mat, g_row)


def _moba_decode_kernel(pt_ref, qt_ref, knt_ref, vnt_ref, *rest, n_pages, page):
    k_pages = rest[:n_pages]
    v_pages = rest[n_pages:2 * n_pages]
    ot_ref, s_scr, p_scr, own_scr = rest[2 * n_pages:]
    del pt_ref
    b = pl.program_id(0)
    n = qt_ref.shape[1]
    nblk = n_pages * page // MOBA_BLOCK

    @pl.when(b == 0)
    def _():
        ot_ref[...] = jnp.zeros_like(ot_ref)

    qrep = _column_lanes(qt_ref, b, page)
    knrep = _column_lanes(knt_ref, b, page)
    vnrep = _column_lanes(vnt_ref, b, page)

    s_scr[MOBA_HEADS:, :] = jnp.zeros((s_scr.shape[0] - MOBA_HEADS, s_scr.shape[1]), F32)
    own_scr[MOBA_HEADS:, :] = jnp.zeros((own_scr.shape[0] - MOBA_HEADS, page), F32)
    for h in range(MOBA_HEADS):
        g = h // Q_PER_KV
        qh = qrep[h * HEAD_DIM:(h + 1) * HEAD_DIM]
        for p in range(n_pages):
            s_scr[h:h + 1, p * page:(p + 1) * page] = jnp.sum(qh * k_pages[p][g], axis=0, keepdims=True)
        own_scr[h:h + 1, :] = jnp.sum(qh * knrep[g * HEAD_DIM:(g + 1) * HEAD_DIM], axis=0, keepdims=True)
    s = s_scr[...]

    gates = [jnp.sum(s[:, j * MOBA_BLOCK:(j + 1) * MOBA_BLOCK], axis=1, keepdims=True) for j in range(nblk)]
    parts = []
    for j in range(nblk):
        rank = jnp.zeros(gates[j].shape, jnp.int32)
        for jj in range(nblk):
            if jj != j:
                beats = (gates[jj] >= gates[j]) if jj < j else (gates[jj] > gates[j])
                rank = rank + jnp.where(beats, 1, 0)
        bias = jnp.where(rank < MOBA_TOPK, 0.0, MASK_VALUE)
        parts.append(s[:, j * MOBA_BLOCK:(j + 1) * MOBA_BLOCK] * ATTN_SCALE + bias)
    s = jnp.concatenate(parts, axis=1)
    s_own = own_scr[:, :1] * ATTN_SCALE
    m = jnp.maximum(jnp.max(s, axis=1, keepdims=True), s_own)
    pr = jnp.exp(s - m)
    p_own = jnp.exp(s_own - m)
    inv = 1.0 / (jnp.sum(pr, axis=1, keepdims=True) + p_own)
    p_scr[...] = pr * inv
    own_scr[...] = jnp.broadcast_to(p_own * inv * (1.0 / page), own_scr.shape)

    accs = []
    for h in range(MOBA_HEADS):
        g = h // Q_PER_KV
        acc = own_scr[h:h + 1, :] * vnrep[g * HEAD_DIM:(g + 1) * HEAD_DIM]
        for p in range(n_pages):
            acc = acc + p_scr[h:h + 1, p * page:(p + 1) * page] * v_pages[p][g]
        accs.append(acc)
    _add_lane_sum_column(ot_ref, jnp.concatenate(accs, axis=0), b)


def _column_lanes(xt_ref, col, lanes):
    rows, n = xt_ref.shape
    at_lane0 = pltpu.roll(xt_ref[...], lax.rem(n - col, n), axis=1)
    return jnp.broadcast_to(at_lane0[:, :1], (rows, lanes))


def _add_lane_sum_column(ot_ref, acc, col):
    place = (lax.broadcasted_iota(jnp.int32, (acc.shape[1], ot_ref.shape[1]), 1) == col).astype(BF16)
    hi = acc.astype(BF16)
    lo = (acc - hi.astype(F32)).astype(BF16)
    ot_ref[...] += (jnp.dot(hi, place, preferred_element_type=F32)
                    + jnp.dot(lo, place, preferred_element_type=F32))


def _moba_decode(q_t, knew_t, vnew_t, cache_kt, cache_vt, page_table):
    n = q_t.shape[1]
    page = cache_kt.shape[3]
    n_pages = page_table.shape[0] // n
    past = n_pages * page
    hrows = -(-MOBA_HEADS // 8) * 8

    def page_spec(p):
        return pl.BlockSpec((None, MOBA_KV_HEADS, HEAD_DIM, page),
                            lambda b, pt, p=p: (pt[b * n_pages + p], 0, 0, 0))

    const2 = lambda b, pt: (0, 0)
    grid_spec = pltpu.PrefetchScalarGridSpec(
        num_scalar_prefetch=1,
        grid=(n,),
        in_specs=([pl.BlockSpec((MIX_W, n), const2),
                   pl.BlockSpec((KV_W, n), const2),
                   pl.BlockSpec((KV_W, n), const2)]
                  + [page_spec(p) for p in range(n_pages)]
                  + [page_spec(p) for p in range(n_pages)]),
        out_specs=pl.BlockSpec((MIX_W, n), const2),
        scratch_shapes=[pltpu.VMEM((hrows, past), F32), pltpu.VMEM((hrows, past), F32),
                        pltpu.VMEM((hrows, page), F32)],
    )
    return pl.pallas_call(
        functools.partial(_moba_decode_kernel, n_pages=n_pages, page=page),
        grid_spec=grid_spec,
        out_shape=jax.ShapeDtypeStruct((MIX_W, n), F32),
        compiler_params=_cparams("arbitrary"),
        name="moba_decode",
    )(page_table, q_t, knew_t, vnew_t, *([cache_kt] * n_pages), *([cache_vt] * n_pages))


def _seg_matrix(width):
    head = jnp.arange(width) // HEAD_DIM
    return (head[:, None] == head[None, :]).astype(BF16) / HEAD_DIM


def _scan_perm():
    r = jnp.arange(SCAN_T)
    t = (r % SCAN_SUB) * SCAN_STEPS + r // SCAN_SUB
    return (t[:, None] == jnp.arange(SCAN_T)[None, :]).astype(BF16)


def _block_diag(x, row_axes):
    eye = jnp.eye(x.shape[1], dtype=x.dtype)
    if row_axes == "gh":
        full = jnp.einsum("kghp,gG->kghGp", x, eye)
    else:
        full = jnp.einsum("kghp,gG->kgpGh", x, eye)
    k, g, a, _, b = full.shape
    return full.reshape(k, g * a, g * b)


def _s5_constants(p, l, ab_re, ab_im, bb_re, bb_im):
    gpb = SSM_KB // SSM_GROUP
    shape4 = (N_KB, gpb, SSM_GROUP, SSM_STATE)
    bmat = jnp.concatenate([_block_diag(bb_re[l].reshape(shape4), "gh"),
                            _block_diag(bb_im[l].reshape(shape4), "gh")], axis=-1).astype(BF16)
    c_re = p["ssm_c_re"][l].reshape(shape4)
    c_im = p["ssm_c_im"][l].reshape(shape4)
    cmat = jnp.concatenate([_block_diag(c_re, "gp"), -_block_diag(c_im, "gp")], axis=1).astype(BF16)
    a_re = ab_re[l, ::SSM_GROUP].reshape(1, N_STATE)
    a_im = ab_im[l, ::SSM_GROUP].reshape(1, N_STATE)
    d_row = p["ssm_d"][l].reshape(1, MIX_W)
    return bmat, a_re, a_im, cmat, d_row, p["w_glu"][l].astype(BF16)


def kernel(x_prompt, x_sample, mem_prompt, state_ssm_re, state_ssm_im, cache_k, cache_v, page_table,
           cache_mem_k, cache_mem_v, g_mix, w_in, w_out, g_ffn, w_gu, w_down, ssm_a_re, ssm_a_im,
           ssm_log_dt, ssm_b_re, ssm_b_im, ssm_c_re, ssm_c_im, ssm_d, w_glu, g_q, g_mq, g_mem,
           w_mem_kv, g_mk, g_kv, w_kv, g_k):
    bsz, seq, _ = x_prompt.shape
    n_p = bsz * seq
    n_s = x_sample.shape[0]
    tokens = mem_prompt.shape[1]
    p = dict(ssm_c_re=ssm_c_re, ssm_c_im=ssm_c_im, ssm_d=ssm_d, w_glu=w_glu)

    w_in_b = w_in.astype(BF16)
    w_out_b = w_out.astype(BF16)
    w_gu_b = w_gu.astype(BF16)
    w_down_b = w_down.astype(BF16)
    seg_kv = _seg_matrix(KV_W)
    seg_mix = _seg_matrix(MIX_W)
    gq_rows = jnp.tile(g_q, (1, MOBA_HEADS))
    gmq_rows = jnp.tile(g_mq, (1, MEM_HEADS))

    rep = lambda a: jnp.repeat(a, SSM_GROUP, axis=1)
    b_rows = lambda b: jnp.swapaxes(b, 2, 3).reshape(N_A_LAYERS, MIX_W, SSM_STATE)
    ab_re, ab_im, bb_re, bb_im = _ssm_prep(rep(ssm_a_re), rep(ssm_a_im), rep(ssm_log_dt)[..., None],
                                           b_rows(ssm_b_re), b_rows(ssm_b_im))
    s5 = [_s5_constants(p, l, ab_re, ab_im, bb_re, bb_im) for l in range(N_A_LAYERS)]
    perm = _scan_perm()
    perm_t = perm.T

    mem_flat = mem_prompt.reshape(bsz * tokens, D_MODEL)
    pm_k, pm_v = _kv_proj(mem_flat, g_mem[:, None, :], w_mem_kv.astype(BF16), seg_kv,
                          jnp.tile(g_mk, (1, MEM_HEADS))[:, None, :], tm=256)
    p_mem_k = pm_k.reshape(DEPTH, bsz, tokens, MEM_HEADS, HEAD_DIM)
    p_mem_v = pm_v.reshape(DEPTH, bsz, tokens, MEM_HEADS, HEAD_DIM)
    pm_k_b = pm_k.reshape(DEPTH, bsz, tokens, MEM_W).astype(BF16)
    pm_v_b = pm_v.reshape(DEPTH, bsz, tokens, MEM_W).astype(BF16)

    g_kv_rows = g_kv.reshape(1, 1, D_MODEL)
    w_kv_b = w_kv.astype(BF16)[None]
    g_k_rows = jnp.tile(g_k, MOBA_KV_HEADS).reshape(1, 1, KV_W)

    h = x_prompt.reshape(n_p, D_MODEL)
    p_fin_re, p_fin_im = [], []
    k_aug = v_t = kmean = p_k = p_v = None
    nblk = seq // MOBA_BLOCK
    g_mix_rows = g_mix[:, None, :]
    next_in = lambda l: (g_mix_rows, w_in_b) if l + 1 < DEPTH else None
    tail_weights = (w_out_b, g_ffn[:, None, :], w_gu_b, w_down_b)
    proj = _norm_matmul(h, g_mix_rows, w_in_b, 0, tm=512)
    for l in range(DEPTH):
        if l < N_A_LAYERS:
            mix, f_re, f_im = _s5_scan(proj, bsz, seq, perm, perm_t, *s5[l])
            p_fin_re.append(f_re.reshape(bsz, SSM_GROUPS, SSM_STATE))
            p_fin_im.append(f_im.reshape(bsz, SSM_GROUPS, SSM_STATE))
        else:
            mix = _moba_prefill(proj, k_aug, v_t, kmean, seg_mix, gq_rows[l - N_A_LAYERS][None], bsz, seq)
        mem = _mem_attn(proj, pm_k_b[l], pm_v_b[l], seg_kv, gmq_rows[l][None], bsz, seq, tq=512)
        h, proj = _layer_tail(h, mix, mem, l, *tail_weights, tm=512, next_in=next_in(l))
        if l == N_A_LAYERS - 1:
            k_new, v_new, k_aug, v_t, kmean = _kv_proj(h, g_kv_rows, w_kv_b, seg_kv, g_k_rows, tm=MOBA_BLOCK,
                                                       blocks_per_seq=nblk)
            p_k = k_new.reshape(bsz, MOBA_KV_HEADS, HEAD_DIM, seq).transpose(0, 3, 1, 2)
            p_v = v_new.reshape(bsz, MOBA_KV_HEADS, HEAD_DIM, seq).transpose(0, 3, 1, 2)
            kmean = kmean.reshape(bsz, nblk, KV_W)
    y_prompt = h.reshape(bsz, seq, D_MODEL)

    ck_t = cache_k.transpose(0, 2, 3, 1)
    cv_t = cache_v.transpose(0, 2, 3, 1)
    cmk_t = cache_mem_k.transpose(0, 1, 3, 4, 2)
    cmv_t = cache_mem_v.transpose(0, 1, 3, 4, 2)

    h = x_sample.reshape(n_s, D_MODEL)
    s_fin_re, s_fin_im = [], []
    s_k = s_v = None
    proj = _norm_matmul(h, g_mix_rows, w_in_b, 0, tm=n_s)
    for l in range(DEPTH):
        if l < N_A_LAYERS:
            mix, x_re, x_im = _s5_step(proj, state_ssm_re[l].reshape(n_s, N_STATE),
                                       state_ssm_im[l].reshape(n_s, N_STATE), *s5[l])
            s_fin_re.append(x_re.reshape(n_s, SSM_GROUPS, SSM_STATE))
            s_fin_im.append(x_im.reshape(n_s, SSM_GROUPS, SSM_STATE))
            mq_t, = _decode_prep(proj, seg_kv, gmq_rows[l][None])
        else:
            mq_t, q_t, knew_t, vnew_t = _decode_prep(
                proj, seg_kv, gmq_rows[l][None],
                moba=(seg_mix, gq_rows[l - N_A_LAYERS][None], s_k.reshape(n_s, KV_W), s_v.reshape(n_s, KV_W)))
            mix = _moba_decode(q_t, knew_t, vnew_t, ck_t, cv_t, page_table.reshape(-1)).T.astype(BF16)
        mem = _mem_decode(mq_t, cmk_t, cmv_t, l).T.astype(BF16)
        h, proj = _layer_tail(h, mix, mem, l, *tail_weights, tm=n_s, next_in=next_in(l))
        if l == N_A_LAYERS - 1:
            s_k, s_v = _kv_proj(h, g_kv_rows, w_kv_b, seg_kv, g_k_rows, tm=n_s)
    y_sample = h.reshape(n_s, 1, D_MODEL)

    return (y_prompt, y_sample,
            jnp.stack(p_fin_re), jnp.stack(p_fin_im), p_k, p_v, p_mem_k, p_mem_v,
            jnp.stack(s_fin_re), jnp.stack(s_fin_im),
            s_k.reshape(n_s, 1, MOBA_KV_HEADS, HEAD_DIM), s_v.reshape(n_s, 1, MOBA_KV_HEADS, HEAD_DIM))
```

```python
import functools
import math

import jax
import jax.numpy as jnp
from jax import lax
from jax.experimental import pallas as pl
from jax.experimental.pallas import tpu as pltpu

F32 = jnp.float32
BF16 = jnp.bfloat16

D_MODEL = 1024
DEPTH = 4
N_A_LAYERS = DEPTH // 2
HEAD_DIM = 64
MIX_W = 3 * D_MODEL // 4
MEM_W = D_MODEL // 4
MEM_HEADS = MEM_W // HEAD_DIM
SSM_GROUP = 16
SSM_GROUPS = MIX_W // SSM_GROUP
SSM_STATE = 64
N_STATE = SSM_GROUPS * SSM_STATE
MOBA_HEADS = MIX_W // HEAD_DIM
MOBA_KV_HEADS = 4
Q_PER_KV = MOBA_HEADS // MOBA_KV_HEADS
KV_W = MOBA_KV_HEADS * HEAD_DIM
MOBA_BLOCK = 256
MOBA_TOPK = 3
D_FF = ((-(-8 * D_MODEL // 3) + 255) // 256) * 256
EPS = 1e-6
ATTN_SCALE = HEAD_DIM ** -0.5
MASK_VALUE = -1e30
KAUG_W = 128
VAUG_H = HEAD_DIM + 16
LOG2_E = math.log2(math.e)

SCAN_SUB = 8
SCAN_STEPS = 32
SCAN_T = SCAN_SUB * SCAN_STEPS
SSM_KB = 256
N_KB = MIX_W // SSM_KB
KB_STATES = (SSM_KB // SSM_GROUP) * SSM_STATE
SCAN_CW = 512

VMEM_LIMIT_BYTES = 56 * 1024 * 1024


def _cparams(*sem):
    return pltpu.CompilerParams(dimension_semantics=sem, vmem_limit_bytes=VMEM_LIMIT_BYTES)


def _rms(x, g_row):
    ms = jnp.mean(x * x, axis=-1, keepdims=True)
    return x * lax.rsqrt(ms + EPS) * g_row


def _seg_rms(x, seg_mat, g_row):
    sq = x * x
    hi = sq.astype(BF16)
    lo = (sq - hi.astype(F32)).astype(BF16)
    ms = (jnp.dot(hi, seg_mat, preferred_element_type=F32)
          + jnp.dot(lo, seg_mat, preferred_element_type=F32))
    return x * lax.rsqrt(ms + EPS) * g_row


def _dot_nt(a, b, precision=None):
    return lax.dot_general(a, b, (((1,), (1,)), ((), ())), precision=precision,
                           preferred_element_type=F32)


def _topk_keep(gate, n_valid, axis):
    n = gate.shape[axis]
    idx = lax.broadcasted_iota(jnp.int32, gate.shape, axis)
    rank = jnp.zeros(gate.shape, jnp.int32)
    for j in range(n):
        gj = lax.slice_in_dim(gate, j, j + 1, axis=axis)
        beats = (gj > gate) | ((gj == gate) & (j < idx))
        rank = rank + jnp.where(beats, jnp.where(j < n_valid, 1, 0), 0)
    return (rank < MOBA_TOPK) & (idx < n_valid)


def _norm_matmul_kernel(x_ref, g_ref, w_ref, o_ref):
    a = _rms(x_ref[...], g_ref[...]).astype(BF16)
    o_ref[...] = jnp.dot(a, w_ref[...], preferred_element_type=F32)


def _norm_matmul(x, g_rows, w, layer, tm):
    n, d = x.shape
    nout = w.shape[2]
    return pl.pallas_call(
        _norm_matmul_kernel,
        grid=(n // tm,),
        in_specs=[pl.BlockSpec((tm, d), lambda i: (i, 0)),
                  pl.BlockSpec((None, 1, d), lambda i: (layer, 0, 0)),
                  pl.BlockSpec((None, d, nout), lambda i: (layer, 0, 0))],
        out_specs=pl.BlockSpec((tm, nout), lambda i: (i, 0)),
        out_shape=jax.ShapeDtypeStruct((n, nout), F32),
        compiler_params=_cparams("parallel"),
        name="norm_matmul",
    )(x, g_rows, w)


def _kv_kernel(x_ref, g_ref, w_ref, seg_ref, gk_ref, k_ref, v_ref, *extra, half, blocks_per_seq):
    a = _rms(x_ref[...], g_ref[0]).astype(BF16)
    kv = jnp.dot(a, w_ref[0], preferred_element_type=F32)
    k = _seg_rms(kv[:, :half], seg_ref[...], gk_ref[0])
    v = kv[:, half:]
    if not blocks_per_seq:
        k_ref[0] = k
        v_ref[0] = v
    else:
        kaug_ref, vt_ref, kmean_ref = extra
        lane = lax.broadcasted_iota(jnp.int32, (k.shape[0], KAUG_W), 1)
        tail = jnp.where(lane == HEAD_DIM + pl.program_id(1) % blocks_per_seq, 1.0, 0.0)
        for g in range(MOBA_KV_HEADS):
            col = k[:, (g // 2) * KAUG_W:(g // 2 + 1) * KAUG_W]
            if g % 2:
                col = pltpu.roll(col, HEAD_DIM, axis=1)
            kaug_ref[g] = jnp.where(lane < HEAD_DIM, col, tail).astype(BF16)
        v_t = v.T
        k_ref[0] = k.T
        v_ref[0] = v_t
        ones_row = jnp.where(lax.broadcasted_iota(jnp.int32, (VAUG_H - HEAD_DIM, k.shape[0]), 0) == 0, 1.0, 0.0)
        vt_ref[...] = jnp.concatenate(
            [piece for g in range(MOBA_KV_HEADS) for piece in (v_t[g * HEAD_DIM:(g + 1) * HEAD_DIM], ones_row)],
            axis=0).astype(BF16)
        kmean_ref[0] = jnp.mean(k, axis=0, keepdims=True)


def _kv_proj(x, g_rows, w, seg_mat, gk_rows, tm, blocks_per_seq=0):
    n, d = x.shape
    layers, _, two_half = w.shape
    half = two_half // 2
    out_shape = [jax.ShapeDtypeStruct((layers, n, half), F32)] * 2
    out_specs = [pl.BlockSpec((1, tm, half), lambda l, i: (l, i, 0))] * 2
    if blocks_per_seq:
        assert layers == 1 and tm == MOBA_BLOCK and half == KV_W
        nseq = n // (tm * blocks_per_seq)
        out_shape = [jax.ShapeDtypeStruct((nseq, half, tm * blocks_per_seq), F32)] * 2
        out_specs = [pl.BlockSpec((1, half, tm), lambda l, i: (i // blocks_per_seq, 0, i % blocks_per_seq))] * 2
        out_shape += [jax.ShapeDtypeStruct((MOBA_KV_HEADS, n, KAUG_W), BF16),
                      jax.ShapeDtypeStruct((MOBA_KV_HEADS * VAUG_H, n), BF16),
                      jax.ShapeDtypeStruct((n // tm, 1, half), F32)]
        out_specs += [pl.BlockSpec((MOBA_KV_HEADS, tm, KAUG_W), lambda l, i: (0, i, 0)),
                      pl.BlockSpec((MOBA_KV_HEADS * VAUG_H, tm), lambda l, i: (0, i)),
                      pl.BlockSpec((1, 1, half), lambda l, i: (i, 0, 0))]
    return pl.pallas_call(
        functools.partial(_kv_kernel, half=half, blocks_per_seq=blocks_per_seq),
        grid=(layers, n // tm),
        in_specs=[pl.BlockSpec((tm, d), lambda l, i: (i, 0)),
                  pl.BlockSpec((1, 1, d), lambda l, i: (l, 0, 0)),
                  pl.BlockSpec((1, d, two_half), lambda l, i: (l, 0, 0)),
                  pl.BlockSpec((half, half), lambda l, i: (0, 0)),
                  pl.BlockSpec((1, 1, half), lambda l, i: (l, 0, 0))],
        out_specs=out_specs,
        out_shape=out_shape,
        compiler_params=_cparams("parallel", "parallel"),
        name="kv_proj",
    )(x, g_rows, w, seg_mat, gk_rows)


def _layer_tail_kernel(h_ref, mix_ref, mem_ref, wmix_ref, wmem_ref, g_ref, wgu_ref, wd_ref, *rest):
    h1 = (h_ref[...]
          + jnp.dot(mix_ref[...], wmix_ref[...], preferred_element_type=F32)
          + jnp.dot(mem_ref[...], wmem_ref[...], preferred_element_type=F32))
    a = _rms(h1, g_ref[...]).astype(BF16)
    gate = jnp.dot(a, wgu_ref[:, :D_FF], preferred_element_type=F32)
    up = jnp.dot(a, wgu_ref[:, D_FF:], preferred_element_type=F32)
    act = (gate * jax.nn.sigmoid(gate) * up).astype(BF16)
    h2 = h1 + jnp.dot(act, wd_ref[...], preferred_element_type=F32)
    if len(rest) == 1:
        rest[0][...] = h2
    else:
        gn_ref, win_ref, o_ref, proj_ref = rest
        o_ref[...] = h2
        proj_ref[...] = jnp.dot(_rms(h2, gn_ref[...]).astype(BF16), win_ref[...], preferred_element_type=F32)


def _layer_tail(h, mix, mem, layer, w_out, g_ffn, w_gu, w_down, tm, next_in=None):
    n, d = h.shape
    row = lambda w: pl.BlockSpec((tm, w), lambda i: (i, 0))

    def resident(a, l, rows=None, row_block=0):
        shape = (None, rows or a.shape[1], a.shape[2])
        return pl.BlockSpec(shape, lambda i: (l, row_block, 0), pipeline_mode=pl.Buffered(1))

    args = [h, mix, mem, w_out, w_out, g_ffn, w_gu, w_down]
    in_specs = [row(d), row(MIX_W), row(MEM_W),
                resident(w_out, layer, MIX_W), resident(w_out, layer, MEM_W, MIX_W // MEM_W),
                resident(g_ffn, layer), resident(w_gu, layer), resident(w_down, layer)]
    out_shape = [jax.ShapeDtypeStruct((n, d), F32)]
    if next_in is not None:
        args += list(next_in)
        in_specs += [resident(a, layer + 1) for a in next_in]
        out_shape.append(jax.ShapeDtypeStruct((n, next_in[1].shape[2]), F32))
    out = pl.pallas_call(
        _layer_tail_kernel,
        grid=(n // tm,),
        in_specs=in_specs,
        out_specs=[row(s.shape[1]) for s in out_shape],
        out_shape=out_shape,
        compiler_params=_cparams("parallel"),
        name="layer_tail",
    )(*args)
    return out if next_in is not None else (out[0], None)


def _ssm_prep_kernel(are_ref, aim_ref, ldt_ref, bre_ref, bim_ref, abr_ref, abi_ref, bbr_ref, bbi_ref):
    dt = jnp.exp(ldt_ref[0])
    lam_re = jnp.minimum(are_ref[0], -1e-4)
    lam_im = aim_ref[0]
    mag = jnp.exp(dt * lam_re)
    ang = dt * lam_im
    ab_re = mag * jnp.cos(ang)
    ab_im = mag * jnp.sin(ang)
    den = lam_re * lam_re + lam_im * lam_im
    num_re = ab_re - 1.0
    f_re = (num_re * lam_re + ab_im * lam_im) / den
    f_im = (ab_im * lam_re - num_re * lam_im) / den
    b_re = bre_ref[0]
    b_im = bim_ref[0]
    abr_ref[0] = ab_re
    abi_ref[0] = ab_im
    bbr_ref[0] = f_re * b_re - f_im * b_im
    bbi_ref[0] = f_re * b_im + f_im * b_re


def _ssm_prep(a_re, a_im, log_dt, b_re, b_im):
    layers = a_re.shape[0]
    big = pl.BlockSpec((1, MIX_W, SSM_STATE), lambda l: (l, 0, 0))
    return pl.pallas_call(
        _ssm_prep_kernel,
        grid=(layers,),
        in_specs=[big, big, pl.BlockSpec((1, MIX_W, 1), lambda l: (l, 0, 0)), big, big],
        out_specs=[big] * 4,
        out_shape=[jax.ShapeDtypeStruct((layers, MIX_W, SSM_STATE), F32)] * 4,
        compiler_params=_cparams("parallel"),
        name="ssm_prep",
    )(a_re, a_im, log_dt, b_re, b_im)


def _re_cols(n0):
    kb, off = divmod(n0, KB_STATES)
    return kb * 2 * KB_STATES + off


def _s5_output(y_state, u, d_ref, wglu_ref):
    y = jax.nn.gelu(y_state + d_ref[...] * u)
    z = jnp.dot(y.astype(BF16), wglu_ref[...], preferred_element_type=F32)
    return (y * jax.nn.sigmoid(z)).astype(BF16)


def _s5_scan_kernel(u_ref, perm_ref, permt_ref, bmat_ref, are_ref, aim_ref, cmat_ref, d_ref, wglu_ref,
                    y_ref, fre_ref, fim_ref, x_scr, apr_scr, api_scr, str_scr, sti_scr):
    c = pl.program_id(1)

    @pl.when(c == 0)
    def _():
        str_scr[...] = jnp.zeros_like(str_scr)
        sti_scr[...] = jnp.zeros_like(sti_scr)
        ar, ai = are_ref[...], aim_ref[...]
        pr, pi = ar, ai
        for i in range(SCAN_STEPS):
            apr_scr[i:i + 1, :] = pr
            api_scr[i:i + 1, :] = pi
            pr, pi = pr * ar - pi * ai, pr * ai + pi * ar

    u = u_ref[...]
    up = jnp.dot(perm_ref[...], u.astype(BF16), preferred_element_type=F32).astype(BF16)
    sub = lax.broadcasted_iota(jnp.int32, (SCAN_SUB, SCAN_CW), 0)
    ys = []
    for kb in range(N_KB):
        x_scr[:, kb * 2 * KB_STATES:(kb + 1) * 2 * KB_STATES] = jnp.dot(
            up[:, kb * SSM_KB:(kb + 1) * SSM_KB], bmat_ref[kb], preferred_element_type=F32)
        for n0 in range(kb * KB_STATES, (kb + 1) * KB_STATES, SCAN_CW):
            rc = _re_cols(n0)
            ic = rc + KB_STATES
            ar = jnp.broadcast_to(are_ref[:, n0:n0 + SCAN_CW], (SCAN_SUB, SCAN_CW))
            ai = jnp.broadcast_to(aim_ref[:, n0:n0 + SCAN_CW], (SCAN_SUB, SCAN_CW))
            xr = jnp.zeros((SCAN_SUB, SCAN_CW), F32)
            xi = jnp.zeros((SCAN_SUB, SCAN_CW), F32)
            for i in range(SCAN_STEPS):
                r0 = i * SCAN_SUB
                xr, xi = (ar * xr - ai * xi + x_scr[r0:r0 + SCAN_SUB, rc:rc + SCAN_CW],
                          ar * xi + ai * xr + x_scr[r0:r0 + SCAN_SUB, ic:ic + SCAN_CW])
                x_scr[r0:r0 + SCAN_SUB, rc:rc + SCAN_CW] = xr
                x_scr[r0:r0 + SCAN_SUB, ic:ic + SCAN_CW] = xi

            a_t_r = apr_scr[SCAN_STEPS - 1:SCAN_STEPS, n0:n0 + SCAN_CW]
            a_t_i = api_scr[SCAN_STEPS - 1:SCAN_STEPS, n0:n0 + SCAN_CW]
            cr = str_scr[:, n0:n0 + SCAN_CW]
            ci = sti_scr[:, n0:n0 + SCAN_CW]
            car_r = jnp.zeros((SCAN_SUB, SCAN_CW), F32)
            car_i = jnp.zeros((SCAN_SUB, SCAN_CW), F32)
            for j in range(SCAN_SUB):
                car_r = jnp.where(sub == j, cr, car_r)
                car_i = jnp.where(sub == j, ci, car_i)
                cr, ci = (xr[j:j + 1] + a_t_r * cr - a_t_i * ci,
                          xi[j:j + 1] + a_t_r * ci + a_t_i * cr)
            str_scr[:, n0:n0 + SCAN_CW] = cr
            sti_scr[:, n0:n0 + SCAN_CW] = ci

            for i in range(SCAN_STEPS):
                r0 = i * SCAN_SUB
                pr = apr_scr[i:i + 1, n0:n0 + SCAN_CW]
                pi = api_scr[i:i + 1, n0:n0 + SCAN_CW]
                x_scr[r0:r0 + SCAN_SUB, rc:rc + SCAN_CW] = (x_scr[r0:r0 + SCAN_SUB, rc:rc + SCAN_CW]
                                                            + (pr * car_r - pi * car_i))
                x_scr[r0:r0 + SCAN_SUB, ic:ic + SCAN_CW] = (x_scr[r0:r0 + SCAN_SUB, ic:ic + SCAN_CW]
                                                            + (pr * car_i + pi * car_r))
        ys.append(jnp.dot(x_scr[:, kb * 2 * KB_STATES:(kb + 1) * 2 * KB_STATES].astype(BF16), cmat_ref[kb],
                          preferred_element_type=F32))
    yp = jnp.concatenate(ys, axis=1)
    hi = yp.astype(BF16)
    lo = (yp - hi.astype(F32)).astype(BF16)
    y_state = (jnp.dot(permt_ref[...], hi, preferred_element_type=F32)
               + jnp.dot(permt_ref[...], lo, preferred_element_type=F32))
    y_ref[...] = _s5_output(y_state, u, d_ref, wglu_ref)
    fre_ref[0] = str_scr[...]
    fim_ref[0] = sti_scr[...]


def _s5_scan(proj, bsz, seq, perm, perm_t, bmat, a_re, a_im, cmat, d_row, w_glu):
    nchunk = seq // SCAN_T
    const2 = lambda b, c: (0, 0)
    const3 = lambda b, c: (0, 0, 0)
    state_spec = pl.BlockSpec((1, 1, N_STATE), lambda b, c: (b, 0, 0))
    return pl.pallas_call(
        _s5_scan_kernel,
        grid=(bsz, nchunk),
        in_specs=[pl.BlockSpec((SCAN_T, MIX_W), lambda b, c: (b * nchunk + c, 0)),
                  pl.BlockSpec((SCAN_T, SCAN_T), const2),
                  pl.BlockSpec((SCAN_T, SCAN_T), const2),
                  pl.BlockSpec((N_KB, SSM_KB, 2 * KB_STATES), const3),
                  pl.BlockSpec((1, N_STATE), const2),
                  pl.BlockSpec((1, N_STATE), const2),
                  pl.BlockSpec((N_KB, 2 * KB_STATES, SSM_KB), const3),
                  pl.BlockSpec((1, MIX_W), const2),
                  pl.BlockSpec((MIX_W, MIX_W), const2)],
        out_specs=[pl.BlockSpec((SCAN_T, MIX_W), lambda b, c: (b * nchunk + c, 0)), state_spec, state_spec],
        out_shape=[jax.ShapeDtypeStruct((bsz * seq, MIX_W), BF16),
                   jax.ShapeDtypeStruct((bsz, 1, N_STATE), F32),
                   jax.ShapeDtypeStruct((bsz, 1, N_STATE), F32)],
        scratch_shapes=[pltpu.VMEM((SCAN_T, 2 * N_STATE), F32),
                        pltpu.VMEM((SCAN_STEPS, N_STATE), F32),
                        pltpu.VMEM((SCAN_STEPS, N_STATE), F32),
                        pltpu.VMEM((1, N_STATE), F32),
                        pltpu.VMEM((1, N_STATE), F32)],
        compiler_params=_cparams("parallel", "arbitrary"),
        name="s5_scan",
    )(proj, perm, perm_t, bmat, a_re, a_im, cmat, d_row, w_glu)


def _s5_step_kernel(u_ref, hre_ref, him_ref, bmat_ref, are_ref, aim_ref, cmat_ref, d_ref, wglu_ref,
                    y_ref, xre_ref, xim_ref):
    u = u_ref[...]
    ub = u.astype(BF16)
    ys = []
    for kb in range(N_KB):
        bu = jnp.dot(ub[:, kb * SSM_KB:(kb + 1) * SSM_KB], bmat_ref[kb], preferred_element_type=F32)
        n0 = kb * KB_STATES
        ar, ai = are_ref[:, n0:n0 + KB_STATES], aim_ref[:, n0:n0 + KB_STATES]
        hr, hi = hre_ref[:, n0:n0 + KB_STATES], him_ref[:, n0:n0 + KB_STATES]
        xr = ar * hr - ai * hi + bu[:, :KB_STATES]
        xi = ar * hi + ai * hr + bu[:, KB_STATES:]
        xre_ref[:, n0:n0 + KB_STATES] = xr
        xim_ref[:, n0:n0 + KB_STATES] = xi
        x = jnp.concatenate([xr, xi], axis=1).astype(BF16)
        ys.append(jnp.dot(x, cmat_ref[kb], preferred_element_type=F32))
    y_ref[...] = _s5_output(jnp.concatenate(ys, axis=1), u, d_ref, wglu_ref)


def _s5_step(proj, h_re, h_im, bmat, a_re, a_im, cmat, d_row, w_glu):
    n = proj.shape[0]
    const2 = lambda i: (0, 0)
    const3 = lambda i: (0, 0, 0)
    return pl.pallas_call(
        _s5_step_kernel,
        grid=(1,),
        in_specs=[pl.BlockSpec((n, MIX_W), const2),
                  pl.BlockSpec((n, N_STATE), const2),
                  pl.BlockSpec((n, N_STATE), const2),
                  pl.BlockSpec((N_KB, SSM_KB, 2 * KB_STATES), const3),
                  pl.BlockSpec((1, N_STATE), const2),
                  pl.BlockSpec((1, N_STATE), const2),
                  pl.BlockSpec((N_KB, 2 * KB_STATES, SSM_KB), const3),
                  pl.BlockSpec((1, MIX_W), const2),
                  pl.BlockSpec((MIX_W, MIX_W), const2)],
        out_specs=[pl.BlockSpec((n, MIX_W), const2),
                   pl.BlockSpec((n, N_STATE), const2),
                   pl.BlockSpec((n, N_STATE), const2)],
        out_shape=[jax.ShapeDtypeStruct((n, MIX_W), BF16),
                   jax.ShapeDtypeStruct((n, N_STATE), F32),
                   jax.ShapeDtypeStruct((n, N_STATE), F32)],
        compiler_params=_cparams("arbitrary"),
        name="s5_step",
    )(proj, h_re, h_im, bmat, a_re, a_im, cmat, d_row, w_glu)


def _mem_attn_kernel(q_ref, k_ref, v_ref, seg_ref, g_ref, o_ref):
    q = _seg_rms(q_ref[...], seg_ref[...], g_ref[...]) * ATTN_SCALE
    k = k_ref[0]
    v = v_ref[0]
    lane_head = lax.broadcasted_iota(jnp.int32, q.shape, 1) // HEAD_DIM
    out = jnp.zeros(q.shape, F32)
    for h in range(MEM_HEADS):
        qh = jnp.where(lane_head == h, q, 0.0).astype(BF16)
        s = _dot_nt(qh, k)
        p = jnp.exp(s - jnp.max(s, axis=-1, keepdims=True))
        o = jnp.dot(p.astype(BF16), v, preferred_element_type=F32) / jnp.sum(p, axis=-1, keepdims=True)
        out = jnp.where(lane_head == h, o, out)
    o_ref[...] = out.astype(BF16)


def _mem_attn(proj, mem_k, mem_v, seg_mat, g_row, bsz, seq, tq):
    nq = seq // tq
    tokens = mem_k.shape[1]
    qcol = MIX_W // MEM_W
    kv_spec = pl.BlockSpec((1, tokens, MEM_W), lambda b, i: (b, 0, 0))
    return pl.pallas_call(
        _mem_attn_kernel,
        grid=(bsz, nq),
        in_specs=[pl.BlockSpec((tq, MEM_W), lambda b, i: (b * nq + i, qcol)),
                  kv_spec, kv_spec,
                  pl.BlockSpec((MEM_W, MEM_W), lambda b, i: (0, 0)),
                  pl.BlockSpec((1, MEM_W), lambda b, i: (0, 0))],
        out_specs=pl.BlockSpec((tq, MEM_W), lambda b, i: (b * nq + i, 0)),
        out_shape=jax.ShapeDtypeStruct((bsz * seq, MEM_W), BF16),
        compiler_params=_cparams("parallel", "parallel"),
        name="mem_attn",
    )(proj, mem_k, mem_v, seg_mat, g_row)


def _decode_prep_kernel(proj_ref, segk_ref, gmq_ref, *rest, with_moba):
    x = proj_ref[...]
    if with_moba:
        segm_ref, gq_ref, knew_ref, vnew_ref, mqt_ref, qt_ref, knt_ref, vnt_ref = rest
        qt_ref[...] = _seg_rms(x[:, :MIX_W], segm_ref[...], gq_ref[...]).T
        knt_ref[...] = knew_ref[...].T
        vnt_ref[...] = vnew_ref[...].T
    else:
        mqt_ref, = rest
    mqt_ref[...] = (_seg_rms(x[:, MIX_W:], segk_ref[...], gmq_ref[...]) * ATTN_SCALE).T


def _decode_prep(proj, seg_kv, gmq_row, moba=None):
    n, d = proj.shape
    const2 = lambda i: (0, 0)
    args = [proj, seg_kv, gmq_row]
    in_specs = [pl.BlockSpec((n, d), const2), pl.BlockSpec((MEM_W, MEM_W), const2), pl.BlockSpec((1, MEM_W), const2)]
    out_shape = [jax.ShapeDtypeStruct((MEM_W, n), F32)]
    if moba is not None:
        args += list(moba)
        in_specs += [pl.BlockSpec((MIX_W, MIX_W), const2), pl.BlockSpec((1, MIX_W), const2),
                     pl.BlockSpec((n, KV_W), const2), pl.BlockSpec((n, KV_W), const2)]
        out_shape += [jax.ShapeDtypeStruct((MIX_W, n), F32), jax.ShapeDtypeStruct((KV_W, n), F32),
                      jax.ShapeDtypeStruct((KV_W, n), F32)]
    return pl.pallas_call(
        functools.partial(_decode_prep_kernel, with_moba=moba is not None),
        grid=(1,),
        in_specs=in_specs,
        out_specs=[pl.BlockSpec(s.shape, const2) for s in out_shape],
        out_shape=out_shape,
        compiler_params=_cparams("arbitrary"),
        name="decode_prep",
    )(*args)


def _mem_decode_kernel(qt_ref, kt_ref, vt_ref, ot_ref, s_scr, p_scr, *, rows):
    i = pl.program_id(0)
    tokens = kt_ref.shape[-1]
    lanes = 128
    nchunk = tokens // lanes

    @pl.when(i == 0)
    def _():
        ot_ref[...] = jnp.zeros_like(ot_ref)

    for r in range(rows):
        qrep = _column_lanes(qt_ref, i * rows + r, lanes)
        for h in range(MEM_HEADS):
            qh = qrep[h * HEAD_DIM:(h + 1) * HEAD_DIM]
            for c in range(nchunk):
                s_scr[r * MEM_HEADS + h:r * MEM_HEADS + h + 1, c * lanes:(c + 1) * lanes] = jnp.sum(
                    qh * kt_ref[r, h, :, c * lanes:(c + 1) * lanes], axis=0, keepdims=True)
    s = s_scr[...]
    pr = jnp.exp(s - jnp.max(s, axis=1, keepdims=True))
    p_scr[...] = pr / jnp.sum(pr, axis=1, keepdims=True)
    for r in range(rows):
        accs = [p_scr[r * MEM_HEADS + h:r * MEM_HEADS + h + 1, :] * vt_ref[r, h] for h in range(MEM_HEADS)]
        _add_lane_sum_column(ot_ref, jnp.concatenate(accs, axis=0), i * rows + r)


def _mem_decode(mq_t, mem_kt, mem_vt, layer, rows=8):
    n = mq_t.shape[1]
    tokens = mem_kt.shape[-1]
    kv_spec = pl.BlockSpec((None, rows, MEM_HEADS, HEAD_DIM, tokens), lambda i: (layer, i, 0, 0, 0))
    return pl.pallas_call(
        functools.partial(_mem_decode_kernel, rows=rows),
        grid=(n // rows,),
        in_specs=[pl.BlockSpec((MEM_W, n), lambda i: (0, 0)), kv_spec, kv_spec],
        out_specs=pl.BlockSpec((MEM_W, n), lambda i: (0, 0)),
        out_shape=jax.ShapeDtypeStruct((MEM_W, n), F32),
        scratch_shapes=[pltpu.VMEM((rows * MEM_HEADS, tokens), F32), pltpu.VMEM((rows * MEM_HEADS, tokens), F32)],
        compiler_params=_cparams("arbitrary"),
        name="mem_decode",
    )(mq_t, mem_kt, mem_vt)


def _moba_prefill_kernel(q_ref, k_ref, vt_ref, kmean_ref, seg_ref, g_ref, o_ref,
                         qa_scr, s_scr, mcur_scr, alpha_scr, m_scr, acc_scr, *, nblk):
    i = pl.program_id(1)
    blk = MOBA_BLOCK
    qn_t = _seg_rms(q_ref[...], seg_ref[...], g_ref[...]).T
    qs_t = (qn_t * (ATTN_SCALE * LOG2_E)).astype(BF16)
    kmean = kmean_ref[0]
    cols = Q_PER_KV * blk
    key_idx = lax.broadcasted_iota(jnp.int32, (blk, cols), 0)
    row_idx = lax.broadcasted_iota(jnp.int32, (blk, cols), 1) % blk
    causal = key_idx <= row_idx
    pad_t = jnp.zeros((KAUG_W - HEAD_DIM - nblk, blk), BF16)

    def start_of(n):
        return pl.multiple_of(jnp.where(n == 0, i, n - 1) * blk, blk)

    def score_phase(n, slot, g, diagonal=False):
        s = jnp.dot(k_ref[g, pl.ds(start_of(n), blk), :], qa_scr[g], preferred_element_type=F32)
        if diagonal:
            s = jnp.where(causal, s, MASK_VALUE)
        s_scr[slot, g] = s
        m_old = m_scr[g]
        m_new = jnp.maximum(m_old, jnp.max(s, axis=0, keepdims=True))
        alpha_scr[slot, g] = jnp.exp2(m_old - m_new)
        mcur_scr[slot, g] = m_new
        m_scr[g] = m_new

    def value_phase(n, slot, g):
        p = jnp.exp2(s_scr[slot, g] - mcur_scr[slot, g]).astype(BF16)
        acc_scr[g] = alpha_scr[slot, g] * acc_scr[g] + jnp.dot(
            vt_ref[g * VAUG_H:(g + 1) * VAUG_H, pl.ds(start_of(n), blk)], p,
            preferred_element_type=F32)

    m_scr[...] = jnp.full(m_scr.shape, MASK_VALUE, F32)
    acc_scr[...] = jnp.zeros_like(acc_scr)
    for g in range(MOBA_KV_HEADS):
        km = kmean[:, g * HEAD_DIM:(g + 1) * HEAD_DIM]
        parts = []
        for hh in range(Q_PER_KV):
            h = g * Q_PER_KV + hh
            gate_t = jnp.dot(km, qn_t[h * HEAD_DIM:(h + 1) * HEAD_DIM], precision=lax.Precision.HIGHEST,
                             preferred_element_type=F32)
            keep_t = _topk_keep(gate_t, i, axis=0)
            past_t = lax.broadcasted_iota(jnp.int32, gate_t.shape, 0) < i
            bias_t = jnp.where(past_t & ~keep_t, MASK_VALUE, 0.0).astype(BF16)
            parts.append(jnp.concatenate([qs_t[h * HEAD_DIM:(h + 1) * HEAD_DIM], bias_t, pad_t], axis=0))
        qa_scr[g] = jnp.concatenate(parts, axis=1)
        score_phase(0, 0, g, diagonal=True)

    def pair_step(t, _):
        for g in range(MOBA_KV_HEADS):
            value_phase(2 * t, 0, g)
            score_phase(2 * t + 1, 1, g)
        for g in range(MOBA_KV_HEADS):
            value_phase(2 * t + 1, 1, g)
            score_phase(2 * t + 2, 0, g)
        return 0

    lax.fori_loop(0, i // 2, pair_step, 0)

    @pl.when(i % 2 == 1)
    def _():
        for g in range(MOBA_KV_HEADS):
            value_phase(i - 1, 0, g)
            score_phase(i, 1, g)
        for g in range(MOBA_KV_HEADS):
            value_phase(i, 1, g)

    @pl.when(i % 2 == 0)
    def _():
        for g in range(MOBA_KV_HEADS):
            value_phase(i, 0, g)

    outs = []
    for g in range(MOBA_KV_HEADS):
        out_t = acc_scr[g, :HEAD_DIM] / acc_scr[g, HEAD_DIM:HEAD_DIM + 1]
        outs += [out_t[:, hh * blk:(hh + 1) * blk] for hh in range(Q_PER_KV)]
    o_ref[...] = jnp.concatenate(outs, axis=0).T.astype(BF16)


def _moba_prefill(proj, k_aug, v_t, kmean, seg_mat, g_row, bsz, seq):
    nblk = seq // MOBA_BLOCK
    cols = Q_PER_KV * MOBA_BLOCK
    return pl.pallas_call(
        functools.partial(_moba_prefill_kernel, nblk=nblk),
        grid=(bsz, nblk),
        in_specs=[pl.BlockSpec((MOBA_BLOCK, MIX_W), lambda b, i: (b * nblk + i, 0)),
                  pl.BlockSpec((MOBA_KV_HEADS, seq, KAUG_W), lambda b, i: (0, b, 0)),
                  pl.BlockSpec((MOBA_KV_HEADS * VAUG_H, seq), lambda b, i: (0, b)),
                  pl.BlockSpec((1, nblk, KV_W), lambda b, i: (b, 0, 0)),
                  pl.BlockSpec((MIX_W, MIX_W), lambda b, i: (0, 0)),
                  pl.BlockSpec((1, MIX_W), lambda b, i: (0, 0))],
        out_specs=pl.BlockSpec((MOBA_BLOCK, MIX_W), lambda b, i: (b * nblk + i, 0)),
        out_shape=jax.ShapeDtypeStruct((bsz * seq, MIX_W), BF16),
        scratch_shapes=[pltpu.VMEM((MOBA_KV_HEADS, KAUG_W, cols), BF16),
                        pltpu.VMEM((2, MOBA_KV_HEADS, MOBA_BLOCK, cols), F32),
                        pltpu.VMEM((2, MOBA_KV_HEADS, 1, cols), F32),
                        pltpu.VMEM((2, MOBA_KV_HEADS, 1, cols), F32),
                        pltpu.VMEM((MOBA_KV_HEADS, 1, cols), F32),
                        pltpu.VMEM((MOBA_KV_HEADS, VAUG_H, cols), F32)],
        compiler_params=_cparams("parallel", "arbitrary"),
        name="moba_prefill",
    )(proj, k_aug, v_t, kmean, seg_mat, g_row)


def _moba_decode_kernel(pt_ref, qt_ref, knt_ref, vnt_ref, ck_hbm, cv_hbm, ot_ref,
                        kbuf, vbuf, sem, s_scr, p_scr, own_scr, *, n_pages, page):
    b = pl.program_id(0)
    nblk = n_pages * page // MOBA_BLOCK
    slot = b % 2

    def page_copies(row, into):
        copies = []
        for p in range(n_pages):
            src = pt_ref[row * n_pages + p]
            copies.append(pltpu.make_async_copy(ck_hbm.at[src], kbuf.at[into, p], sem.at[0, into]))
            copies.append(pltpu.make_async_copy(cv_hbm.at[src], vbuf.at[into, p], sem.at[1, into]))
        return copies

    @pl.when(b == 0)
    def _():
        ot_ref[...] = jnp.zeros_like(ot_ref)
        for c in page_copies(0, 0):
            c.start()

    @pl.when(b + 1 < pl.num_programs(0))
    def _():
        for c in page_copies(b + 1, 1 - slot):
            c.start()

    for c in page_copies(b, slot):
        c.wait()
    k_pages = [kbuf.at[slot, p] for p in range(n_pages)]
    v_pages = [vbuf.at[slot, p] for p in range(n_pages)]

    qrep = _column_lanes(qt_ref, b, page)
    knrep = _column_lanes(knt_ref, b, page)
    vnrep = _column_lanes(vnt_ref, b, page)

    s_scr[MOBA_HEADS:, :] = jnp.zeros((s_scr.shape[0] - MOBA_HEADS, s_scr.shape[1]), F32)
    own_scr[MOBA_HEADS:, :] = jnp.zeros((own_scr.shape[0] - MOBA_HEADS, page), F32)
    for h in range(MOBA_HEADS):
        g = h // Q_PER_KV
        qh = qrep[h * HEAD_DIM:(h + 1) * HEAD_DIM]
        for p in range(n_pages):
            s_scr[h:h + 1, p * page:(p + 1) * page] = jnp.sum(qh * k_pages[p][g], axis=0, keepdims=True)
        own_scr[h:h + 1, :] = jnp.sum(qh * knrep[g * HEAD_DIM:(g + 1) * HEAD_DIM], axis=0, keepdims=True)
    s = s_scr[...]

    gates = [jnp.sum(s[:, j * MOBA_BLOCK:(j + 1) * MOBA_BLOCK], axis=1, keepdims=True) for j in range(nblk)]
    parts = []
    for j in range(nblk):
        rank = jnp.zeros(gates[j].shape, jnp.int32)
        for jj in range(nblk):
            if jj != j:
                beats = (gates[jj] >= gates[j]) if jj < j else (gates[jj] > gates[j])
                rank = rank + jnp.where(beats, 1, 0)
        bias = jnp.where(rank < MOBA_TOPK, 0.0, MASK_VALUE)
        parts.append(s[:, j * MOBA_BLOCK:(j + 1) * MOBA_BLOCK] * ATTN_SCALE + bias)
    s = jnp.concatenate(parts, axis=1)
    s_own = own_scr[:, :1] * ATTN_SCALE
    m = jnp.maximum(jnp.max(s, axis=1, keepdims=True), s_own)
    pr = jnp.exp(s - m)
    p_own = jnp.exp(s_own - m)
    inv = 1.0 / (jnp.sum(pr, axis=1, keepdims=True) + p_own)
    p_scr[...] = pr * inv
    own_scr[...] = jnp.broadcast_to(p_own * inv * (1.0 / page), own_scr.shape)

    accs = []
    for h in range(MOBA_HEADS):
        g = h // Q_PER_KV
        acc = own_scr[h:h + 1, :] * vnrep[g * HEAD_DIM:(g + 1) * HEAD_DIM]
        for p in range(n_pages):
            acc = acc + p_scr[h:h + 1, p * page:(p + 1) * page] * v_pages[p][g]
        accs.append(acc)
    _add_lane_sum_column(ot_ref, jnp.concatenate(accs, axis=0), b)


def _column_lanes(xt_ref, col, lanes):
    rows, n = xt_ref.shape
    at_lane0 = pltpu.roll(xt_ref[...], lax.rem(n - col, n), axis=1)
    return jnp.broadcast_to(at_lane0[:, :1], (rows, lanes))


def _add_lane_sum_column(ot_ref, acc, col):
    place = (lax.broadcasted_iota(jnp.int32, (acc.shape[1], ot_ref.shape[1]), 1) == col).astype(BF16)
    hi = acc.astype(BF16)
    lo = (acc - hi.astype(F32)).astype(BF16)
    ot_ref[...] += (jnp.dot(hi, place, preferred_element_type=F32)
                    + jnp.dot(lo, place, preferred_element_type=F32))


def _moba_decode(q_t, knew_t, vnew_t, cache_kt, cache_vt, page_table):
    n = q_t.shape[1]
    page = cache_kt.shape[3]
    n_pages = page_table.shape[0] // n
    past = n_pages * page
    hrows = -(-MOBA_HEADS // 8) * 8

    const2 = lambda b, pt: (0, 0)
    page_buf = pltpu.VMEM((2, n_pages, MOBA_KV_HEADS, HEAD_DIM, page), F32)
    grid_spec = pltpu.PrefetchScalarGridSpec(
        num_scalar_prefetch=1,
        grid=(n,),
        in_specs=[pl.BlockSpec((MIX_W, n), const2),
                  pl.BlockSpec((KV_W, n), const2),
                  pl.BlockSpec((KV_W, n), const2),
                  pl.BlockSpec(memory_space=pl.ANY),
                  pl.BlockSpec(memory_space=pl.ANY)],
        out_specs=pl.BlockSpec((MIX_W, n), const2),
        scratch_shapes=[page_buf, page_buf, pltpu.SemaphoreType.DMA((2, 2)),
                        pltpu.VMEM((hrows, past), F32), pltpu.VMEM((hrows, past), F32),
                        pltpu.VMEM((hrows, page), F32)],
    )
    return pl.pallas_call(
        functools.partial(_moba_decode_kernel, n_pages=n_pages, page=page),
        grid_spec=grid_spec,
        out_shape=jax.ShapeDtypeStruct((MIX_W, n), F32),
        compiler_params=_cparams("arbitrary"),
        name="moba_decode",
    )(page_table, q_t, knew_t, vnew_t, cache_kt, cache_vt)


def _seg_matrix(width):
    head = jnp.arange(width) // HEAD_DIM
    return (head[:, None] == head[None, :]).astype(BF16) / HEAD_DIM


def _scan_perm():
    r = jnp.arange(SCAN_T)
    t = (r % SCAN_SUB) * SCAN_STEPS + r // SCAN_SUB
    return (t[:, None] == jnp.arange(SCAN_T)[None, :]).astype(BF16)


def _block_diag(x, row_axes):
    eye = jnp.eye(x.shape[1], dtype=x.dtype)
    if row_axes == "gh":
        full = jnp.einsum("kghp,gG->kghGp", x, eye)
    else:
        full = jnp.einsum("kghp,gG->kgpGh", x, eye)
    k, g, a, _, b = full.shape
    return full.reshape(k, g * a, g * b)


def _s5_constants(p, l, ab_re, ab_im, bb_re, bb_im):
    gpb = SSM_KB // SSM_GROUP
    shape4 = (N_KB, gpb, SSM_GROUP, SSM_STATE)
    bmat = jnp.concatenate([_block_diag(bb_re[l].reshape(shape4), "gh"),
                            _block_diag(bb_im[l].reshape(shape4), "gh")], axis=-1).astype(BF16)
    c_re = p["ssm_c_re"][l].reshape(shape4)
    c_im = p["ssm_c_im"][l].reshape(shape4)
    cmat = jnp.concatenate([_block_diag(c_re, "gp"), -_block_diag(c_im, "gp")], axis=1).astype(BF16)
    a_re = ab_re[l, ::SSM_GROUP].reshape(1, N_STATE)
    a_im = ab_im[l, ::SSM_GROUP].reshape(1, N_STATE)
    d_row = p["ssm_d"][l].reshape(1, MIX_W)
    return bmat, a_re, a_im, cmat, d_row, p["w_glu"][l].astype(BF16)


def kernel(x_prompt, x_sample, mem_prompt, state_ssm_re, state_ssm_im, cache_k, cache_v, page_table,
           cache_mem_k, cache_mem_v, g_mix, w_in, w_out, g_ffn, w_gu, w_down, ssm_a_re, ssm_a_im,
           ssm_log_dt, ssm_b_re, ssm_b_im, ssm_c_re, ssm_c_im, ssm_d, w_glu, g_q, g_mq, g_mem,
           w_mem_kv, g_mk, g_kv, w_kv, g_k):
    bsz, seq, _ = x_prompt.shape
    n_p = bsz * seq
    n_s = x_sample.shape[0]
    tokens = mem_prompt.shape[1]
    p = dict(ssm_c_re=ssm_c_re, ssm_c_im=ssm_c_im, ssm_d=ssm_d, w_glu=w_glu)

    w_in_b = w_in.astype(BF16)
    w_out_b = w_out.astype(BF16)
    w_gu_b = w_gu.astype(BF16)
    w_down_b = w_down.astype(BF16)
    seg_kv = _seg_matrix(KV_W)
    seg_mix = _seg_matrix(MIX_W)
    gq_rows = jnp.tile(g_q, (1, MOBA_HEADS))
    gmq_rows = jnp.tile(g_mq, (1, MEM_HEADS))

    rep = lambda a: jnp.repeat(a, SSM_GROUP, axis=1)
    b_rows = lambda b: jnp.swapaxes(b, 2, 3).reshape(N_A_LAYERS, MIX_W, SSM_STATE)
    ab_re, ab_im, bb_re, bb_im = _ssm_prep(rep(ssm_a_re), rep(ssm_a_im), rep(ssm_log_dt)[..., None],
                                           b_rows(ssm_b_re), b_rows(ssm_b_im))
    s5 = [_s5_constants(p, l, ab_re, ab_im, bb_re, bb_im) for l in range(N_A_LAYERS)]
    perm = _scan_perm()
    perm_t = perm.T

    mem_flat = mem_prompt.reshape(bsz * tokens, D_MODEL)
    pm_k, pm_v = _kv_proj(mem_flat, g_mem[:, None, :], w_mem_kv.astype(BF16), seg_kv,
                          jnp.tile(g_mk, (1, MEM_HEADS))[:, None, :], tm=256)
    p_mem_k = pm_k.reshape(DEPTH, bsz, tokens, MEM_HEADS, HEAD_DIM)
    p_mem_v = pm_v.reshape(DEPTH, bsz, tokens, MEM_HEADS, HEAD_DIM)
    pm_k_b = pm_k.reshape(DEPTH, bsz, tokens, MEM_W).astype(BF16)
    pm_v_b = pm_v.reshape(DEPTH, bsz, tokens, MEM_W).astype(BF16)

    g_kv_rows = g_kv.reshape(1, 1, D_MODEL)
    w_kv_b = w_kv.astype(BF16)[None]
    g_k_rows = jnp.tile(g_k, MOBA_KV_HEADS).reshape(1, 1, KV_W)

    h = x_prompt.reshape(n_p, D_MODEL)
    p_fin_re, p_fin_im = [], []
    k_aug = v_t = kmean = p_k = p_v = None
    nblk = seq // MOBA_BLOCK
    g_mix_rows = g_mix[:, None, :]
    next_in = lambda l: (g_mix_rows, w_in_b) if l + 1 < DEPTH else None
    tail_weights = (w_out_b, g_ffn[:, None, :], w_gu_b, w_down_b)
    proj = _norm_matmul(h, g_mix_rows, w_in_b, 0, tm=512)
    for l in range(DEPTH):
        if l < N_A_LAYERS:
            mix, f_re, f_im = _s5_scan(proj, bsz, seq, perm, perm_t, *s5[l])
            p_fin_re.append(f_re.reshape(bsz, SSM_GROUPS, SSM_STATE))
            p_fin_im.append(f_im.reshape(bsz, SSM_GROUPS, SSM_STATE))
        else:
            mix = _moba_prefill(proj, k_aug, v_t, kmean, seg_mix, gq_rows[l - N_A_LAYERS][None], bsz, seq)
        mem = _mem_attn(proj, pm_k_b[l], pm_v_b[l], seg_kv, gmq_rows[l][None], bsz, seq, tq=512)
        h, proj = _layer_tail(h, mix, mem, l, *tail_weights, tm=512, next_in=next_in(l))
        if l == N_A_LAYERS - 1:
            k_new, v_new, k_aug, v_t, kmean = _kv_proj(h, g_kv_rows, w_kv_b, seg_kv, g_k_rows, tm=MOBA_BLOCK,
                                                       blocks_per_seq=nblk)
            p_k = k_new.reshape(bsz, MOBA_KV_HEADS, HEAD_DIM, seq).transpose(0, 3, 1, 2)
            p_v = v_new.reshape(bsz, MOBA_KV_HEADS, HEAD_DIM, seq).transpose(0, 3, 1, 2)
            kmean = kmean.reshape(bsz, nblk, KV_W)
    y_prompt = h.reshape(bsz, seq, D_MODEL)

    ck_t = cache_k.transpose(0, 2, 3, 1)
    cv_t = cache_v.transpose(0, 2, 3, 1)
    cmk_t = cache_mem_k.transpose(0, 1, 3, 4, 2)
    cmv_t = cache_mem_v.transpose(0, 1, 3, 4, 2)

    h = x_sample.reshape(n_s, D_MODEL)
    s_fin_re, s_fin_im = [], []
    s_k = s_v = None
    proj = _norm_matmul(h, g_mix_rows, w_in_b, 0, tm=n_s)
    for l in range(DEPTH):
        if l < N_A_LAYERS:
            mix, x_re, x_im = _s5_step(proj, state_ssm_re[l].reshape(n_s, N_STATE),
                                       state_ssm_im[l].reshape(n_s, N_STATE), *s5[l])
            s_fin_re.append(x_re.reshape(n_s, SSM_GROUPS, SSM_STATE))
            s_fin_im.append(x_im.reshape(n_s, SSM_GROUPS, SSM_STATE))
            mq_t, = _decode_prep(proj, seg_kv, gmq_rows[l][None])
        else:
            mq_t, q_t, knew_t, vnew_t = _decode_prep(
                proj, seg_kv, gmq_rows[l][None],
                moba=(seg_mix, gq_rows[l - N_A_LAYERS][None], s_k.reshape(n_s, KV_W), s_v.reshape(n_s, KV_W)))
            mix = _moba_decode(q_t, knew_t, vnew_t, ck_t, cv_t, page_table.reshape(-1)).T.astype(BF16)
        mem = _mem_decode(mq_t, cmk_t, cmv_t, l).T.astype(BF16)
        h, proj = _layer_tail(h, mix, mem, l, *tail_weights, tm=n_s, next_in=next_in(l))
        if l == N_A_LAYERS - 1:
            s_k, s_v = _kv_proj(h, g_kv_rows, w_kv_b, seg_kv, g_k_rows, tm=n_s)
    y_sample = h.reshape(n_s, 1, D_MODEL)

    return (y_prompt, y_sample,
            jnp.stack(p_fin_re), jnp.stack(p_fin_im), p_k, p_v, p_mem_k, p_mem_v,
            jnp.stack(s_fin_re), jnp.stack(s_fin_im),
            s_k.reshape(n_s, 1, MOBA_KV_HEADS, HEAD_DIM), s_v.reshape(n_s, 1, MOBA_KV_HEADS, HEAD_DIM))
```

```python
import functools
import math

import jax
import jax.numpy as jnp
from jax import lax
from jax.experimental import pallas as pl
from jax.experimental.pallas import tpu as pltpu

F32 = jnp.float32
BF16 = jnp.bfloat16

D_MODEL = 1024
DEPTH = 4
N_A_LAYERS = DEPTH // 2
HEAD_DIM = 64
MIX_W = 3 * D_MODEL // 4
MEM_W = D_MODEL // 4
MEM_HEADS = MEM_W // HEAD_DIM
SSM_GROUP = 16
SSM_GROUPS = MIX_W // SSM_GROUP
SSM_STATE = 64
N_STATE = SSM_GROUPS * SSM_STATE
MOBA_HEADS = MIX_W // HEAD_DIM
MOBA_KV_HEADS = 4
Q_PER_KV = MOBA_HEADS // MOBA_KV_HEADS
KV_W = MOBA_KV_HEADS * HEAD_DIM
MOBA_BLOCK = 256
MOBA_TOPK = 3
D_FF = ((-(-8 * D_MODEL // 3) + 255) // 256) * 256
EPS = 1e-6
ATTN_SCALE = HEAD_DIM ** -0.5
MASK_VALUE = -1e30
KAUG_W = 128
VAUG_H = HEAD_DIM + 16
LOG2_E = math.log2(math.e)

SCAN_SUB = 8
SCAN_STEPS = 32
SCAN_T = SCAN_SUB * SCAN_STEPS
SSM_KB = 256
N_KB = MIX_W // SSM_KB
KB_STATES = (SSM_KB // SSM_GROUP) * SSM_STATE
SCAN_CW = 512

VMEM_LIMIT_BYTES = 56 * 1024 * 1024


def _cparams(*sem):
    return pltpu.CompilerParams(dimension_semantics=sem, vmem_limit_bytes=VMEM_LIMIT_BYTES)


def _rms(x, g_row):
    ms = jnp.mean(x * x, axis=-1, keepdims=True)
    return x * lax.rsqrt(ms + EPS) * g_row


def _seg_rms(x, seg_mat, g_row):
    sq = x * x
    hi = sq.astype(BF16)
    lo = (sq - hi.astype(F32)).astype(BF16)
    ms = (jnp.dot(hi, seg_mat, preferred_element_type=F32)
          + jnp.dot(lo, seg_mat, preferred_element_type=F32))
    return x * lax.rsqrt(ms + EPS) * g_row


def _dot_nt(a, b, precision=None):
    return lax.dot_general(a, b, (((1,), (1,)), ((), ())), precision=precision,
                           preferred_element_type=F32)


def _topk_keep(gate, n_valid, axis):
    n = gate.shape[axis]
    idx = lax.broadcasted_iota(jnp.int32, gate.shape, axis)
    rank = jnp.zeros(gate.shape, jnp.int32)
    for j in range(n):
        gj = lax.slice_in_dim(gate, j, j + 1, axis=axis)
        beats = (gj > gate) | ((gj == gate) & (j < idx))
        rank = rank + jnp.where(beats, jnp.where(j < n_valid, 1, 0), 0)
    return (rank < MOBA_TOPK) & (idx < n_valid)


def _norm_matmul_kernel(x_ref, g_ref, w_ref, o_ref):
    a = _rms(x_ref[...], g_ref[...]).astype(BF16)
    o_ref[...] = jnp.dot(a, w_ref[...], preferred_element_type=F32)


def _norm_matmul(x, g_rows, w, layer, tm):
    n, d = x.shape
    nout = w.shape[2]
    return pl.pallas_call(
        _norm_matmul_kernel,
        grid=(n // tm,),
        in_specs=[pl.BlockSpec((tm, d), lambda i: (i, 0)),
                  pl.BlockSpec((None, 1, d), lambda i: (layer, 0, 0)),
                  pl.BlockSpec((None, d, nout), lambda i: (layer, 0, 0))],
        out_specs=pl.BlockSpec((tm, nout), lambda i: (i, 0)),
        out_shape=jax.ShapeDtypeStruct((n, nout), F32),
        compiler_params=_cparams("parallel"),
        name="norm_matmul",
    )(x, g_rows, w)


def _kv_kernel(x_ref, g_ref, w_ref, seg_ref, gk_ref, k_ref, v_ref, *extra, half, blocks_per_seq):
    a = _rms(x_ref[...], g_ref[0]).astype(BF16)
    kv = jnp.dot(a, w_ref[0], preferred_element_type=F32)
    k = _seg_rms(kv[:, :half], seg_ref[...], gk_ref[0])
    v = kv[:, half:]
    if not blocks_per_seq:
        k_ref[0] = k
        v_ref[0] = v
    else:
        kaug_ref, vt_ref, kmean_ref = extra
        lane = lax.broadcasted_iota(jnp.int32, (k.shape[0], KAUG_W), 1)
        tail = jnp.where(lane == HEAD_DIM + pl.program_id(1) % blocks_per_seq, 1.0, 0.0)
        for g in range(MOBA_KV_HEADS):
            col = k[:, (g // 2) * KAUG_W:(g // 2 + 1) * KAUG_W]
            if g % 2:
                col = pltpu.roll(col, HEAD_DIM, axis=1)
            kaug_ref[g] = jnp.where(lane < HEAD_DIM, col, tail).astype(BF16)
        v_t = v.T
        k_ref[0] = k.T
        v_ref[0] = v_t
        ones_row = jnp.where(lax.broadcasted_iota(jnp.int32, (VAUG_H - HEAD_DIM, k.shape[0]), 0) == 0, 1.0, 0.0)
        vt_ref[...] = jnp.concatenate(
            [piece for g in range(MOBA_KV_HEADS) for piece in (v_t[g * HEAD_DIM:(g + 1) * HEAD_DIM], ones_row)],
            axis=0).astype(BF16)
        kmean_ref[0] = jnp.mean(k, axis=0, keepdims=True)


def _kv_proj(x, g_rows, w, seg_mat, gk_rows, tm, blocks_per_seq=0):
    n, d = x.shape
    layers, _, two_half = w.shape
    half = two_half // 2
    out_shape = [jax.ShapeDtypeStruct((layers, n, half), F32)] * 2
    out_specs = [pl.BlockSpec((1, tm, half), lambda l, i: (l, i, 0))] * 2
    if blocks_per_seq:
        assert layers == 1 and tm == MOBA_BLOCK and half == KV_W
        nseq = n // (tm * blocks_per_seq)
        out_shape = [jax.ShapeDtypeStruct((nseq, half, tm * blocks_per_seq), F32)] * 2
        out_specs = [pl.BlockSpec((1, half, tm), lambda l, i: (i // blocks_per_seq, 0, i % blocks_per_seq))] * 2
        out_shape += [jax.ShapeDtypeStruct((MOBA_KV_HEADS, n, KAUG_W), BF16),
                      jax.ShapeDtypeStruct((MOBA_KV_HEADS * VAUG_H, n), BF16),
                      jax.ShapeDtypeStruct((n // tm, 1, half), F32)]
        out_specs += [pl.BlockSpec((MOBA_KV_HEADS, tm, KAUG_W), lambda l, i: (0, i, 0)),
                      pl.BlockSpec((MOBA_KV_HEADS * VAUG_H, tm), lambda l, i: (0, i)),
                      pl.BlockSpec((1, 1, half), lambda l, i: (i, 0, 0))]
    return pl.pallas_call(
        functools.partial(_kv_kernel, half=half, blocks_per_seq=blocks_per_seq),
        grid=(layers, n // tm),
        in_specs=[pl.BlockSpec((tm, d), lambda l, i: (i, 0)),
                  pl.BlockSpec((1, 1, d), lambda l, i: (l, 0, 0)),
                  pl.BlockSpec((1, d, two_half), lambda l, i: (l, 0, 0)),
                  pl.BlockSpec((half, half), lambda l, i: (0, 0)),
                  pl.BlockSpec((1, 1, half), lambda l, i: (l, 0, 0))],
        out_specs=out_specs,
        out_shape=out_shape,
        compiler_params=_cparams("parallel", "parallel"),
        name="kv_proj",
    )(x, g_rows, w, seg_mat, gk_rows)


def _layer_tail_kernel(h_ref, mix_ref, mem_ref, wmix_ref, wmem_ref, g_ref, wgu_ref, wd_ref, *rest):
    h1 = (h_ref[...]
          + jnp.dot(mix_ref[...], wmix_ref[...], preferred_element_type=F32)
          + jnp.dot(mem_ref[...], wmem_ref[...], preferred_element_type=F32))
    a = _rms(h1, g_ref[...]).astype(BF16)
    gate = jnp.dot(a, wgu_ref[:, :D_FF], preferred_element_type=F32)
    up = jnp.dot(a, wgu_ref[:, D_FF:], preferred_element_type=F32)
    act = (gate * jax.nn.sigmoid(gate) * up).astype(BF16)
    h2 = h1 + jnp.dot(act, wd_ref[...], preferred_element_type=F32)
    if len(rest) == 1:
        rest[0][...] = h2
    else:
        gn_ref, win_ref, o_ref, proj_ref = rest
        o_ref[...] = h2
        proj_ref[...] = jnp.dot(_rms(h2, gn_ref[...]).astype(BF16), win_ref[...], preferred_element_type=F32)


def _layer_tail(h, mix, mem, layer, w_out, g_ffn, w_gu, w_down, tm, next_in=None):
    n, d = h.shape
    row = lambda w: pl.BlockSpec((tm, w), lambda i: (i, 0))

    def resident(a, l, rows=None, row_block=0):
        shape = (None, rows or a.shape[1], a.shape[2])
        return pl.BlockSpec(shape, lambda i: (l, row_block, 0), pipeline_mode=pl.Buffered(1))

    args = [h, mix, mem, w_out, w_out, g_ffn, w_gu, w_down]
    in_specs = [row(d), row(MIX_W), row(MEM_W),
                resident(w_out, layer, MIX_W), resident(w_out, layer, MEM_W, MIX_W // MEM_W),
                resident(g_ffn, layer), resident(w_gu, layer), resident(w_down, layer)]
    out_shape = [jax.ShapeDtypeStruct((n, d), F32)]
    if next_in is not None:
        args += list(next_in)
        in_specs += [resident(a, layer + 1) for a in next_in]
        out_shape.append(jax.ShapeDtypeStruct((n, next_in[1].shape[2]), F32))
    out = pl.pallas_call(
        _layer_tail_kernel,
        grid=(n // tm,),
        in_specs=in_specs,
        out_specs=[row(s.shape[1]) for s in out_shape],
        out_shape=out_shape,
        compiler_params=_cparams("parallel"),
        name="layer_tail",
    )(*args)
    return out if next_in is not None else (out[0], None)


def _ssm_prep_kernel(are_ref, aim_ref, ldt_ref, bre_ref, bim_ref, abr_ref, abi_ref, bbr_ref, bbi_ref):
    dt = jnp.exp(ldt_ref[0])
    lam_re = jnp.minimum(are_ref[0], -1e-4)
    lam_im = aim_ref[0]
    mag = jnp.exp(dt * lam_re)
    ang = dt * lam_im
    ab_re = mag * jnp.cos(ang)
    ab_im = mag * jnp.sin(ang)
    den = lam_re * lam_re + lam_im * lam_im
    num_re = ab_re - 1.0
    f_re = (num_re * lam_re + ab_im * lam_im) / den
    f_im = (ab_im * lam_re - num_re * lam_im) / den
    b_re = bre_ref[0]
    b_im = bim_ref[0]
    abr_ref[0] = ab_re
    abi_ref[0] = ab_im
    bbr_ref[0] = f_re * b_re - f_im * b_im
    bbi_ref[0] = f_re * b_im + f_im * b_re


def _ssm_prep(a_re, a_im, log_dt, b_re, b_im):
    layers = a_re.shape[0]
    big = pl.BlockSpec((1, MIX_W, SSM_STATE), lambda l: (l, 0, 0))
    return pl.pallas_call(
        _ssm_prep_kernel,
        grid=(layers,),
        in_specs=[big, big, pl.BlockSpec((1, MIX_W, 1), lambda l: (l, 0, 0)), big, big],
        out_specs=[big] * 4,
        out_shape=[jax.ShapeDtypeStruct((layers, MIX_W, SSM_STATE), F32)] * 4,
        compiler_params=_cparams("parallel"),
        name="ssm_prep",
    )(a_re, a_im, log_dt, b_re, b_im)


def _re_cols(n0):
    kb, off = divmod(n0, KB_STATES)
    return kb * 2 * KB_STATES + off


def _s5_output(y_state, u, d_ref, wglu_ref):
    y = jax.nn.gelu(y_state + d_ref[...] * u)
    z = jnp.dot(y.astype(BF16), wglu_ref[...], preferred_element_type=F32)
    return (y * jax.nn.sigmoid(z)).astype(BF16)


def _s5_scan_kernel(u_ref, perm_ref, permt_ref, bmat_ref, are_ref, aim_ref, cmat_ref, d_ref, wglu_ref,
                    mq_ref, mk_ref, mv_ref, mseg_ref, mg_ref,
                    y_ref, mem_ref, fre_ref, fim_ref, x_scr, apr_scr, api_scr, str_scr, sti_scr):
    c = pl.program_id(1)

    @pl.when(c == 0)
    def _():
        str_scr[...] = jnp.zeros_like(str_scr)
        sti_scr[...] = jnp.zeros_like(sti_scr)
        ar, ai = are_ref[...], aim_ref[...]
        pr, pi = ar, ai
        for i in range(SCAN_STEPS):
            apr_scr[i:i + 1, :] = pr
            api_scr[i:i + 1, :] = pi
            pr, pi = pr * ar - pi * ai, pr * ai + pi * ar

    mem_ref[...] = _mem_attend(mq_ref[...], mk_ref[0], mv_ref[0], mseg_ref[...], mg_ref[...])

    u = u_ref[...]
    up = jnp.dot(perm_ref[...], u.astype(BF16), preferred_element_type=F32).astype(BF16)
    sub = lax.broadcasted_iota(jnp.int32, (SCAN_SUB, SCAN_CW), 0)
    ys = []
    for kb in range(N_KB):
        x_scr[:, kb * 2 * KB_STATES:(kb + 1) * 2 * KB_STATES] = jnp.dot(
            up[:, kb * SSM_KB:(kb + 1) * SSM_KB], bmat_ref[kb], preferred_element_type=F32)
        for n0 in range(kb * KB_STATES, (kb + 1) * KB_STATES, SCAN_CW):
            rc = _re_cols(n0)
            ic = rc + KB_STATES
            ar = jnp.broadcast_to(are_ref[:, n0:n0 + SCAN_CW], (SCAN_SUB, SCAN_CW))
            ai = jnp.broadcast_to(aim_ref[:, n0:n0 + SCAN_CW], (SCAN_SUB, SCAN_CW))
            xr = jnp.zeros((SCAN_SUB, SCAN_CW), F32)
            xi = jnp.zeros((SCAN_SUB, SCAN_CW), F32)
            for i in range(SCAN_STEPS):
                r0 = i * SCAN_SUB
                xr, xi = (ar * xr - ai * xi + x_scr[r0:r0 + SCAN_SUB, rc:rc + SCAN_CW],
                          ar * xi + ai * xr + x_scr[r0:r0 + SCAN_SUB, ic:ic + SCAN_CW])
                x_scr[r0:r0 + SCAN_SUB, rc:rc + SCAN_CW] = xr
                x_scr[r0:r0 + SCAN_SUB, ic:ic + SCAN_CW] = xi

            a_t_r = apr_scr[SCAN_STEPS - 1:SCAN_STEPS, n0:n0 + SCAN_CW]
            a_t_i = api_scr[SCAN_STEPS - 1:SCAN_STEPS, n0:n0 + SCAN_CW]
            cr = str_scr[:, n0:n0 + SCAN_CW]
            ci = sti_scr[:, n0:n0 + SCAN_CW]
            car_r = jnp.zeros((SCAN_SUB, SCAN_CW), F32)
            car_i = jnp.zeros((SCAN_SUB, SCAN_CW), F32)
            for j in range(SCAN_SUB):
                car_r = jnp.where(sub == j, cr, car_r)
                car_i = jnp.where(sub == j, ci, car_i)
                cr, ci = (xr[j:j + 1] + a_t_r * cr - a_t_i * ci,
                          xi[j:j + 1] + a_t_r * ci + a_t_i * cr)
            str_scr[:, n0:n0 + SCAN_CW] = cr
            sti_scr[:, n0:n0 + SCAN_CW] = ci

            for i in range(SCAN_STEPS):
                r0 = i * SCAN_SUB
                pr = apr_scr[i:i + 1, n0:n0 + SCAN_CW]
                pi = api_scr[i:i + 1, n0:n0 + SCAN_CW]
                x_scr[r0:r0 + SCAN_SUB, rc:rc + SCAN_CW] = (x_scr[r0:r0 + SCAN_SUB, rc:rc + SCAN_CW]
                                                            + (pr * car_r - pi * car_i))
                x_scr[r0:r0 + SCAN_SUB, ic:ic + SCAN_CW] = (x_scr[r0:r0 + SCAN_SUB, ic:ic + SCAN_CW]
                                                            + (pr * car_i + pi * car_r))
        ys.append(jnp.dot(x_scr[:, kb * 2 * KB_STATES:(kb + 1) * 2 * KB_STATES].astype(BF16), cmat_ref[kb],
                          preferred_element_type=F32))
    yp = jnp.concatenate(ys, axis=1)
    hi = yp.astype(BF16)
    lo = (yp - hi.astype(F32)).astype(BF16)
    y_state = (jnp.dot(permt_ref[...], hi, preferred_element_type=F32)
               + jnp.dot(permt_ref[...], lo, preferred_element_type=F32))
    y_ref[...] = _s5_output(y_state, u, d_ref, wglu_ref)
    fre_ref[0] = str_scr[...]
    fim_ref[0] = sti_scr[...]


def _s5_scan(proj, mem_args, bsz, seq, perm, perm_t, bmat, a_re, a_im, cmat, d_row, w_glu):
    nchunk = seq // SCAN_T
    const2 = lambda b, c: (0, 0)
    const3 = lambda b, c: (0, 0, 0)
    state_spec = pl.BlockSpec((1, 1, N_STATE), lambda b, c: (b, 0, 0))
    return pl.pallas_call(
        _s5_scan_kernel,
        grid=(bsz, nchunk),
        in_specs=[pl.BlockSpec((SCAN_T, MIX_W), lambda b, c: (b * nchunk + c, 0)),
                  pl.BlockSpec((SCAN_T, SCAN_T), const2),
                  pl.BlockSpec((SCAN_T, SCAN_T), const2),
                  pl.BlockSpec((N_KB, SSM_KB, 2 * KB_STATES), const3),
                  pl.BlockSpec((1, N_STATE), const2),
                  pl.BlockSpec((1, N_STATE), const2),
                  pl.BlockSpec((N_KB, 2 * KB_STATES, SSM_KB), const3),
                  pl.BlockSpec((1, MIX_W), const2),
                  pl.BlockSpec((MIX_W, MIX_W), const2)] + _mem_specs(SCAN_T, nchunk, mem_args[0].shape[1]),
        out_specs=[pl.BlockSpec((SCAN_T, MIX_W), lambda b, c: (b * nchunk + c, 0)),
                   pl.BlockSpec((SCAN_T, MEM_W), lambda b, c: (b * nchunk + c, 0)), state_spec, state_spec],
        out_shape=[jax.ShapeDtypeStruct((bsz * seq, MIX_W), BF16),
                   jax.ShapeDtypeStruct((bsz * seq, MEM_W), BF16),
                   jax.ShapeDtypeStruct((bsz, 1, N_STATE), F32),
                   jax.ShapeDtypeStruct((bsz, 1, N_STATE), F32)],
        scratch_shapes=[pltpu.VMEM((SCAN_T, 2 * N_STATE), F32),
                        pltpu.VMEM((SCAN_STEPS, N_STATE), F32),
                        pltpu.VMEM((SCAN_STEPS, N_STATE), F32),
                        pltpu.VMEM((1, N_STATE), F32),
                        pltpu.VMEM((1, N_STATE), F32)],
        compiler_params=_cparams("parallel", "arbitrary"),
        name="s5_scan",
    )(proj, perm, perm_t, bmat, a_re, a_im, cmat, d_row, w_glu, proj, *mem_args)


def _s5_step_kernel(u_ref, hre_ref, him_ref, bmat_ref, are_ref, aim_ref, cmat_ref, d_ref, wglu_ref,
                    y_ref, xre_ref, xim_ref):
    u = u_ref[...]
    ub = u.astype(BF16)
    ys = []
    for kb in range(N_KB):
        bu = jnp.dot(ub[:, kb * SSM_KB:(kb + 1) * SSM_KB], bmat_ref[kb], preferred_element_type=F32)
        n0 = kb * KB_STATES
        ar, ai = are_ref[:, n0:n0 + KB_STATES], aim_ref[:, n0:n0 + KB_STATES]
        hr, hi = hre_ref[:, n0:n0 + KB_STATES], him_ref[:, n0:n0 + KB_STATES]
        xr = ar * hr - ai * hi + bu[:, :KB_STATES]
        xi = ar * hi + ai * hr + bu[:, KB_STATES:]
        xre_ref[:, n0:n0 + KB_STATES] = xr
        xim_ref[:, n0:n0 + KB_STATES] = xi
        x = jnp.concatenate([xr, xi], axis=1).astype(BF16)
        ys.append(jnp.dot(x, cmat_ref[kb], preferred_element_type=F32))
    y_ref[...] = _s5_output(jnp.concatenate(ys, axis=1), u, d_ref, wglu_ref)


def _s5_step(proj, h_re, h_im, bmat, a_re, a_im, cmat, d_row, w_glu):
    n = proj.shape[0]
    const2 = lambda i: (0, 0)
    const3 = lambda i: (0, 0, 0)
    return pl.pallas_call(
        _s5_step_kernel,
        grid=(1,),
        in_specs=[pl.BlockSpec((n, MIX_W), const2),
                  pl.BlockSpec((n, N_STATE), const2),
                  pl.BlockSpec((n, N_STATE), const2),
                  pl.BlockSpec((N_KB, SSM_KB, 2 * KB_STATES), const3),
                  pl.BlockSpec((1, N_STATE), const2),
                  pl.BlockSpec((1, N_STATE), const2),
                  pl.BlockSpec((N_KB, 2 * KB_STATES, SSM_KB), const3),
                  pl.BlockSpec((1, MIX_W), const2),
                  pl.BlockSpec((MIX_W, MIX_W), const2)],
        out_specs=[pl.BlockSpec((n, MIX_W), const2),
                   pl.BlockSpec((n, N_STATE), const2),
                   pl.BlockSpec((n, N_STATE), const2)],
        out_shape=[jax.ShapeDtypeStruct((n, MIX_W), BF16),
                   jax.ShapeDtypeStruct((n, N_STATE), F32),
                   jax.ShapeDtypeStruct((n, N_STATE), F32)],
        compiler_params=_cparams("arbitrary"),
        name="s5_step",
    )(proj, h_re, h_im, bmat, a_re, a_im, cmat, d_row, w_glu)


def _mem_attend(q, k, v, seg_mat, g_row):
    q = _seg_rms(q, seg_mat, g_row) * ATTN_SCALE
    lane_head = lax.broadcasted_iota(jnp.int32, q.shape, 1) // HEAD_DIM
    out = jnp.zeros(q.shape, F32)
    for h in range(MEM_HEADS):
        qh = jnp.where(lane_head == h, q, 0.0).astype(BF16)
        s = _dot_nt(qh, k)
        p = jnp.exp(s - jnp.max(s, axis=-1, keepdims=True))
        o = jnp.dot(p.astype(BF16), v, preferred_element_type=F32) / jnp.sum(p, axis=-1, keepdims=True)
        out = jnp.where(lane_head == h, o, out)
    return out.astype(BF16)


def _mem_specs(rows, blocks_per_seq, tokens):
    const2 = lambda b, i: (0, 0)
    kv_spec = pl.BlockSpec((1, tokens, MEM_W), lambda b, i: (b, 0, 0))
    return [pl.BlockSpec((rows, MEM_W), lambda b, i: (b * blocks_per_seq + i, MIX_W // MEM_W)),
            kv_spec, kv_spec, pl.BlockSpec((MEM_W, MEM_W), const2), pl.BlockSpec((1, MEM_W), const2)]


def _decode_prep_kernel(proj_ref, segk_ref, gmq_ref, *rest, with_moba):
    x = proj_ref[...]
    if with_moba:
        segm_ref, gq_ref, knew_ref, vnew_ref, mqt_ref, qt_ref, knt_ref, vnt_ref = rest
        qt_ref[...] = _seg_rms(x[:, :MIX_W], segm_ref[...], gq_ref[...]).T
        knt_ref[...] = knew_ref[...].T
        vnt_ref[...] = vnew_ref[...].T
    else:
        mqt_ref, = rest
    mqt_ref[...] = (_seg_rms(x[:, MIX_W:], segk_ref[...], gmq_ref[...]) * ATTN_SCALE).T


def _decode_prep(proj, seg_kv, gmq_row, moba=None):
    n, d = proj.shape
    const2 = lambda i: (0, 0)
    args = [proj, seg_kv, gmq_row]
    in_specs = [pl.BlockSpec((n, d), const2), pl.BlockSpec((MEM_W, MEM_W), const2), pl.BlockSpec((1, MEM_W), const2)]
    out_shape = [jax.ShapeDtypeStruct((MEM_W, n), F32)]
    if moba is not None:
        args += list(moba)
        in_specs += [pl.BlockSpec((MIX_W, MIX_W), const2), pl.BlockSpec((1, MIX_W), const2),
                     pl.BlockSpec((n, KV_W), const2), pl.BlockSpec((n, KV_W), const2)]
        out_shape += [jax.ShapeDtypeStruct((MIX_W, n), F32), jax.ShapeDtypeStruct((KV_W, n), F32),
                      jax.ShapeDtypeStruct((KV_W, n), F32)]
    return pl.pallas_call(
        functools.partial(_decode_prep_kernel, with_moba=moba is not None),
        grid=(1,),
        in_specs=in_specs,
        out_specs=[pl.BlockSpec(s.shape, const2) for s in out_shape],
        out_shape=out_shape,
        compiler_params=_cparams("arbitrary"),
        name="decode_prep",
    )(*args)


def _mem_decode_kernel(qt_ref, kt_ref, vt_ref, ot_ref, s_scr, p_scr, *, rows):
    i = pl.program_id(0)
    tokens = kt_ref.shape[-1]
    lanes = 128
    nchunk = tokens // lanes

    @pl.when(i == 0)
    def _():
        ot_ref[...] = jnp.zeros_like(ot_ref)

    for r in range(rows):
        qrep = _column_lanes(qt_ref, i * rows + r, lanes)
        for h in range(MEM_HEADS):
            qh = qrep[h * HEAD_DIM:(h + 1) * HEAD_DIM]
            for c in range(nchunk):
                s_scr[r * MEM_HEADS + h:r * MEM_HEADS + h + 1, c * lanes:(c + 1) * lanes] = jnp.sum(
                    qh * kt_ref[r, h, :, c * lanes:(c + 1) * lanes], axis=0, keepdims=True)
    s = s_scr[...]
    pr = jnp.exp(s - jnp.max(s, axis=1, keepdims=True))
    p_scr[...] = pr / jnp.sum(pr, axis=1, keepdims=True)
    for r in range(rows):
        accs = [p_scr[r * MEM_HEADS + h:r * MEM_HEADS + h + 1, :] * vt_ref[r, h] for h in range(MEM_HEADS)]
        _add_lane_sum_column(ot_ref, jnp.concatenate(accs, axis=0), i * rows + r)


def _mem_decode(mq_t, mem_kt, mem_vt, layer, rows=8):
    n = mq_t.shape[1]
    tokens = mem_kt.shape[-1]
    kv_spec = pl.BlockSpec((None, rows, MEM_HEADS, HEAD_DIM, tokens), lambda i: (layer, i, 0, 0, 0))
    return pl.pallas_call(
        functools.partial(_mem_decode_kernel, rows=rows),
        grid=(n // rows,),
        in_specs=[pl.BlockSpec((MEM_W, n), lambda i: (0, 0)), kv_spec, kv_spec],
        out_specs=pl.BlockSpec((MEM_W, n), lambda i: (0, 0)),
        out_shape=jax.ShapeDtypeStruct((MEM_W, n), F32),
        scratch_shapes=[pltpu.VMEM((rows * MEM_HEADS, tokens), F32), pltpu.VMEM((rows * MEM_HEADS, tokens), F32)],
        compiler_params=_cparams("arbitrary"),
        name="mem_decode",
    )(mq_t, mem_kt, mem_vt)


def _moba_prefill_kernel(q_ref, k_ref, vt_ref, kmean_ref, seg_ref, g_ref,
                         mq_ref, mk_ref, mv_ref, mseg_ref, mg_ref, o_ref, mem_ref,
                         qa_scr, s_scr, mcur_scr, alpha_scr, m_scr, acc_scr, *, nblk):
    mem_ref[...] = _mem_attend(mq_ref[...], mk_ref[0], mv_ref[0], mseg_ref[...], mg_ref[...])
    i = pl.program_id(1)
    blk = MOBA_BLOCK
    qn_t = _seg_rms(q_ref[...], seg_ref[...], g_ref[...]).T
    qs_t = (qn_t * (ATTN_SCALE * LOG2_E)).astype(BF16)
    kmean = kmean_ref[0]
    cols = Q_PER_KV * blk
    key_idx = lax.broadcasted_iota(jnp.int32, (blk, cols), 0)
    row_idx = lax.broadcasted_iota(jnp.int32, (blk, cols), 1) % blk
    causal = key_idx <= row_idx
    pad_t = jnp.zeros((KAUG_W - HEAD_DIM - nblk, blk), BF16)

    def start_of(n):
        return pl.multiple_of(jnp.where(n == 0, i, n - 1) * blk, blk)

    def score_phase(n, slot, g, diagonal=False):
        s = jnp.dot(k_ref[g, pl.ds(start_of(n), blk), :], qa_scr[g], preferred_element_type=F32)
        if diagonal:
            s = jnp.where(causal, s, MASK_VALUE)
        s_scr[slot, g] = s
        m_old = m_scr[g]
        m_new = jnp.maximum(m_old, jnp.max(s, axis=0, keepdims=True))
        alpha_scr[slot, g] = jnp.exp2(m_old - m_new)
        mcur_scr[slot, g] = m_new
        m_scr[g] = m_new

    def value_phase(n, slot, g):
        p = jnp.exp2(s_scr[slot, g] - mcur_scr[slot, g]).astype(BF16)
        acc_scr[g] = alpha_scr[slot, g] * acc_scr[g] + jnp.dot(
            vt_ref[g * VAUG_H:(g + 1) * VAUG_H, pl.ds(start_of(n), blk)], p,
            preferred_element_type=F32)

    m_scr[...] = jnp.full(m_scr.shape, MASK_VALUE, F32)
    acc_scr[...] = jnp.zeros_like(acc_scr)
    for g in range(MOBA_KV_HEADS):
        km = kmean[:, g * HEAD_DIM:(g + 1) * HEAD_DIM]
        parts = []
        for hh in range(Q_PER_KV):
            h = g * Q_PER_KV + hh
            gate_t = jnp.dot(km, qn_t[h * HEAD_DIM:(h + 1) * HEAD_DIM], precision=lax.Precision.HIGHEST,
                             preferred_element_type=F32)
            keep_t = _topk_keep(gate_t, i, axis=0)
            past_t = lax.broadcasted_iota(jnp.int32, gate_t.shape, 0) < i
            bias_t = jnp.where(past_t & ~keep_t, MASK_VALUE, 0.0).astype(BF16)
            parts.append(jnp.concatenate([qs_t[h * HEAD_DIM:(h + 1) * HEAD_DIM], bias_t, pad_t], axis=0))
        qa_scr[g] = jnp.concatenate(parts, axis=1)
        score_phase(0, 0, g, diagonal=True)

    def pair_step(t, _):
        for g in range(MOBA_KV_HEADS):
            value_phase(2 * t, 0, g)
            score_phase(2 * t + 1, 1, g)
        for g in range(MOBA_KV_HEADS):
            value_phase(2 * t + 1, 1, g)
            score_phase(2 * t + 2, 0, g)
        return 0

    lax.fori_loop(0, i // 2, pair_step, 0)

    @pl.when(i % 2 == 1)
    def _():
        for g in range(MOBA_KV_HEADS):
            value_phase(i - 1, 0, g)
            score_phase(i, 1, g)
        for g in range(MOBA_KV_HEADS):
            value_phase(i, 1, g)

    @pl.when(i % 2 == 0)
    def _():
        for g in range(MOBA_KV_HEADS):
            value_phase(i, 0, g)

    outs = []
    for g in range(MOBA_KV_HEADS):
        out_t = acc_scr[g, :HEAD_DIM] / acc_scr[g, HEAD_DIM:HEAD_DIM + 1]
        outs += [out_t[:, hh * blk:(hh + 1) * blk] for hh in range(Q_PER_KV)]
    o_ref[...] = jnp.concatenate(outs, axis=0).T.astype(BF16)


def _moba_prefill(proj, mem_args, k_aug, v_t, kmean, seg_mat, g_row, bsz, seq):
    nblk = seq // MOBA_BLOCK
    cols = Q_PER_KV * MOBA_BLOCK
    return pl.pallas_call(
        functools.partial(_moba_prefill_kernel, nblk=nblk),
        grid=(bsz, nblk),
        in_specs=[pl.BlockSpec((MOBA_BLOCK, MIX_W), lambda b, i: (b * nblk + i, 0)),
                  pl.BlockSpec((MOBA_KV_HEADS, seq, KAUG_W), lambda b, i: (0, b, 0)),
                  pl.BlockSpec((MOBA_KV_HEADS * VAUG_H, seq), lambda b, i: (0, b)),
                  pl.BlockSpec((1, nblk, KV_W), lambda b, i: (b, 0, 0)),
                  pl.BlockSpec((MIX_W, MIX_W), lambda b, i: (0, 0)),
                  pl.BlockSpec((1, MIX_W), lambda b, i: (0, 0))] + _mem_specs(MOBA_BLOCK, nblk, mem_args[0].shape[1]),
        out_specs=[pl.BlockSpec((MOBA_BLOCK, MIX_W), lambda b, i: (b * nblk + i, 0)),
                   pl.BlockSpec((MOBA_BLOCK, MEM_W), lambda b, i: (b * nblk + i, 0))],
        out_shape=[jax.ShapeDtypeStruct((bsz * seq, MIX_W), BF16),
                   jax.ShapeDtypeStruct((bsz * seq, MEM_W), BF16)],
        scratch_shapes=[pltpu.VMEM((MOBA_KV_HEADS, KAUG_W, cols), BF16),
                        pltpu.VMEM((2, MOBA_KV_HEADS, MOBA_BLOCK, cols), F32),
                        pltpu.VMEM((2, MOBA_KV_HEADS, 1, cols), F32),
                        pltpu.VMEM((2, MOBA_KV_HEADS, 1, cols), F32),
                        pltpu.VMEM((MOBA_KV_HEADS, 1, cols), F32),
                        pltpu.VMEM((MOBA_KV_HEADS, VAUG_H, cols), F32)],
        compiler_params=_cparams("parallel", "arbitrary"),
        name="moba_prefill",
    )(proj, k_aug, v_t, kmean, seg_mat, g_row, proj, *mem_args)


def _moba_decode_kernel(pt_ref, qt_ref, knt_ref, vnt_ref, ck_hbm, cv_hbm, ot_ref,
                        kbuf, vbuf, sem, qrep_scr, s_scr, p_scr, own_scr, *, n_pages, page):
    b = pl.program_id(0)
    nblk = n_pages * page // MOBA_BLOCK
    slot = b % 2

    last = pl.num_programs(0) - 1
    nxt = jnp.minimum(b + 1, last)

    def page_copies(row, into):
        copies = []
        for p in range(n_pages):
            src = pt_ref[row * n_pages + p]
            copies.append(pltpu.make_async_copy(ck_hbm.at[src], kbuf.at[into, :, :, p], sem.at[0, into]))
            copies.append(pltpu.make_async_copy(cv_hbm.at[src], vbuf.at[into, p], sem.at[1, into]))
        return copies

    @pl.when(b == 0)
    def _():
        ot_ref[...] = jnp.zeros_like(ot_ref)
        for c in page_copies(0, 0):
            c.start()

    for c in page_copies(b, slot):
        c.wait()
    v_pages = [vbuf.at[slot, p] for p in range(n_pages)]

    qrep = _column_lanes(qt_ref, b, page)
    knrep = _column_lanes(knt_ref, b, page)
    vnrep = _column_lanes(vnt_ref, b, page)

    qrep_scr[...] = qrep
    s_scr[MOBA_HEADS:, :] = jnp.zeros((s_scr.shape[0] - MOBA_HEADS, s_scr.shape[1]), F32)
    own_scr[MOBA_HEADS:, :] = jnp.zeros((own_scr.shape[0] - MOBA_HEADS, page), F32)
    for g in range(MOBA_KV_HEADS):
        heads = range(g * Q_PER_KV, (g + 1) * Q_PER_KV)
        accs = {h: jnp.zeros((n_pages, page), F32) for h in heads}
        for d in range(HEAD_DIM):
            k_d = kbuf[slot, g, d]
            for h in heads:
                accs[h] = accs[h] + qrep_scr[h * HEAD_DIM + d:h * HEAD_DIM + d + 1, :] * k_d
        for h in heads:
            for p in range(n_pages):
                s_scr[h:h + 1, p * page:(p + 1) * page] = accs[h][p:p + 1]
            own_scr[h:h + 1, :] = jnp.sum(qrep[h * HEAD_DIM:(h + 1) * HEAD_DIM] * knrep[g * HEAD_DIM:(g + 1) * HEAD_DIM],
                                          axis=0, keepdims=True)
    s = s_scr[...]
    for c in page_copies(nxt, 1 - slot):
        c.start()

    gates = [jnp.sum(s[:, j * MOBA_BLOCK:(j + 1) * MOBA_BLOCK], axis=1, keepdims=True) for j in range(nblk)]
    parts = []
    for j in range(nblk):
        rank = jnp.zeros(gates[j].shape, jnp.int32)
        for jj in range(nblk):
            if jj != j:
                beats = (gates[jj] >= gates[j]) if jj < j else (gates[jj] > gates[j])
                rank = rank + jnp.where(beats, 1, 0)
        bias = jnp.where(rank < MOBA_TOPK, 0.0, MASK_VALUE)
        parts.append(s[:, j * MOBA_BLOCK:(j + 1) * MOBA_BLOCK] * ATTN_SCALE + bias)
    s = jnp.concatenate(parts, axis=1)
    s_own = own_scr[:, :1] * ATTN_SCALE
    m = jnp.maximum(jnp.max(s, axis=1, keepdims=True), s_own)
    pr = jnp.exp(s - m)
    p_own = jnp.exp(s_own - m)
    inv = 1.0 / (jnp.sum(pr, axis=1, keepdims=True) + p_own)
    p_scr[...] = pr * inv
    own_scr[...] = jnp.broadcast_to(p_own * inv * (1.0 / page), own_scr.shape)

    accs = []
    for h in range(MOBA_HEADS):
        g = h // Q_PER_KV
        acc = own_scr[h:h + 1, :] * vnrep[g * HEAD_DIM:(g + 1) * HEAD_DIM]
        for p in range(n_pages):
            acc = acc + p_scr[h:h + 1, p * page:(p + 1) * page] * v_pages[p][g]
        accs.append(acc)
    _add_lane_sum_column(ot_ref, jnp.concatenate(accs, axis=0), b)

    @pl.when(b == last)
    def _():
        for c in page_copies(nxt, 1 - slot):
            c.wait()


def _column_lanes(xt_ref, col, lanes):
    rows, n = xt_ref.shape
    at_lane0 = pltpu.roll(xt_ref[...], lax.rem(n - col, n), axis=1)
    return jnp.broadcast_to(at_lane0[:, :1], (rows, lanes))


def _add_lane_sum_column(ot_ref, acc, col):
    place = (lax.broadcasted_iota(jnp.int32, (acc.shape[1], ot_ref.shape[1]), 1) == col).astype(BF16)
    hi = acc.astype(BF16)
    lo = (acc - hi.astype(F32)).astype(BF16)
    ot_ref[...] += (jnp.dot(hi, place, preferred_element_type=F32)
                    + jnp.dot(lo, place, preferred_element_type=F32))


def _moba_decode(q_t, knew_t, vnew_t, cache_kt, cache_vt, page_table):
    n = q_t.shape[1]
    page = cache_kt.shape[3]
    n_pages = page_table.shape[0] // n
    past = n_pages * page
    hrows = -(-MOBA_HEADS // 8) * 8

    const2 = lambda b, pt: (0, 0)
    page_buf = pltpu.VMEM((2, n_pages, MOBA_KV_HEADS, HEAD_DIM, page), F32)
    grid_spec = pltpu.PrefetchScalarGridSpec(
        num_scalar_prefetch=1,
        grid=(n,),
        in_specs=[pl.BlockSpec((MIX_W, n), const2),
                  pl.BlockSpec((KV_W, n), const2),
                  pl.BlockSpec((KV_W, n), const2),
                  pl.BlockSpec(memory_space=pl.ANY),
                  pl.BlockSpec(memory_space=pl.ANY)],
        out_specs=pl.BlockSpec((MIX_W, n), const2),
        scratch_shapes=[pltpu.VMEM((2, MOBA_KV_HEADS, HEAD_DIM, n_pages, page), F32), page_buf,
                        pltpu.SemaphoreType.DMA((2, 2)), pltpu.VMEM((MIX_W, page), F32),
                        pltpu.VMEM((hrows, past), F32), pltpu.VMEM((hrows, past), F32),
                        pltpu.VMEM((hrows, page), F32)],
    )
    return pl.pallas_call(
        functools.partial(_moba_decode_kernel, n_pages=n_pages, page=page),
        grid_spec=grid_spec,
        out_shape=jax.ShapeDtypeStruct((MIX_W, n), F32),
        compiler_params=_cparams("arbitrary"),
        name="moba_decode",
    )(page_table, q_t, knew_t, vnew_t, cache_kt, cache_vt)


def _seg_matrix(width):
    head = jnp.arange(width) // HEAD_DIM
    return (head[:, None] == head[None, :]).astype(BF16) / HEAD_DIM


def _scan_perm():
    r = jnp.arange(SCAN_T)
    t = (r % SCAN_SUB) * SCAN_STEPS + r // SCAN_SUB
    return (t[:, None] == jnp.arange(SCAN_T)[None, :]).astype(BF16)


def _block_diag(x, row_axes):
    eye = jnp.eye(x.shape[1], dtype=x.dtype)
    if row_axes == "gh":
        full = jnp.einsum("kghp,gG->kghGp", x, eye)
    else:
        full = jnp.einsum("kghp,gG->kgpGh", x, eye)
    k, g, a, _, b = full.shape
    return full.reshape(k, g * a, g * b)


def _s5_constants(p, l, ab_re, ab_im, bb_re, bb_im):
    gpb = SSM_KB // SSM_GROUP
    shape4 = (N_KB, gpb, SSM_GROUP, SSM_STATE)
    bmat = jnp.concatenate([_block_diag(bb_re[l].reshape(shape4), "gh"),
                            _block_diag(bb_im[l].reshape(shape4), "gh")], axis=-1).astype(BF16)
    c_re = p["ssm_c_re"][l].reshape(shape4)
    c_im = p["ssm_c_im"][l].reshape(shape4)
    cmat = jnp.concatenate([_block_diag(c_re, "gp"), -_block_diag(c_im, "gp")], axis=1).astype(BF16)
    a_re = ab_re[l, ::SSM_GROUP].reshape(1, N_STATE)
    a_im = ab_im[l, ::SSM_GROUP].reshape(1, N_STATE)
    d_row = p["ssm_d"][l].reshape(1, MIX_W)
    return bmat, a_re, a_im, cmat, d_row, p["w_glu"][l].astype(BF16)


def kernel(x_prompt, x_sample, mem_prompt, state_ssm_re, state_ssm_im, cache_k, cache_v, page_table,
           cache_mem_k, cache_mem_v, g_mix, w_in, w_out, g_ffn, w_gu, w_down, ssm_a_re, ssm_a_im,
           ssm_log_dt, ssm_b_re, ssm_b_im, ssm_c_re, ssm_c_im, ssm_d, w_glu, g_q, g_mq, g_mem,
           w_mem_kv, g_mk, g_kv, w_kv, g_k):
    bsz, seq, _ = x_prompt.shape
    n_p = bsz * seq
    n_s = x_sample.shape[0]
    tokens = mem_prompt.shape[1]
    p = dict(ssm_c_re=ssm_c_re, ssm_c_im=ssm_c_im, ssm_d=ssm_d, w_glu=w_glu)

    w_in_b = w_in.astype(BF16)
    w_out_b = w_out.astype(BF16)
    w_gu_b = w_gu.astype(BF16)
    w_down_b = w_down.astype(BF16)
    seg_kv = _seg_matrix(KV_W)
    seg_mix = _seg_matrix(MIX_W)
    gq_rows = jnp.tile(g_q, (1, MOBA_HEADS))
    gmq_rows = jnp.tile(g_mq, (1, MEM_HEADS))

    rep = lambda a: jnp.repeat(a, SSM_GROUP, axis=1)
    b_rows = lambda b: jnp.swapaxes(b, 2, 3).reshape(N_A_LAYERS, MIX_W, SSM_STATE)
    ab_re, ab_im, bb_re, bb_im = _ssm_prep(rep(ssm_a_re), rep(ssm_a_im), rep(ssm_log_dt)[..., None],
                                           b_rows(ssm_b_re), b_rows(ssm_b_im))
    s5 = [_s5_constants(p, l, ab_re, ab_im, bb_re, bb_im) for l in range(N_A_LAYERS)]
    perm = _scan_perm()
    perm_t = perm.T

    mem_flat = mem_prompt.reshape(bsz * tokens, D_MODEL)
    pm_k, pm_v = _kv_proj(mem_flat, g_mem[:, None, :], w_mem_kv.astype(BF16), seg_kv,
                          jnp.tile(g_mk, (1, MEM_HEADS))[:, None, :], tm=256)
    p_mem_k = pm_k.reshape(DEPTH, bsz, tokens, MEM_HEADS, HEAD_DIM)
    p_mem_v = pm_v.reshape(DEPTH, bsz, tokens, MEM_HEADS, HEAD_DIM)
    pm_k_b = pm_k.reshape(DEPTH, bsz, tokens, MEM_W).astype(BF16)
    pm_v_b = pm_v.reshape(DEPTH, bsz, tokens, MEM_W).astype(BF16)

    g_kv_rows = g_kv.reshape(1, 1, D_MODEL)
    w_kv_b = w_kv.astype(BF16)[None]
    g_k_rows = jnp.tile(g_k, MOBA_KV_HEADS).reshape(1, 1, KV_W)

    h = x_prompt.reshape(n_p, D_MODEL)
    p_fin_re, p_fin_im = [], []
    k_aug = v_t = kmean = p_k = p_v = None
    nblk = seq // MOBA_BLOCK
    g_mix_rows = g_mix[:, None, :]
    next_in = lambda l: (g_mix_rows, w_in_b) if l + 1 < DEPTH else None
    tail_weights = (w_out_b, g_ffn[:, None, :], w_gu_b, w_down_b)
    proj = _norm_matmul(h, g_mix_rows, w_in_b, 0, tm=512)
    for l in range(DEPTH):
        mem_args = (pm_k_b[l], pm_v_b[l], seg_kv, gmq_rows[l][None])
        if l < N_A_LAYERS:
            mix, mem, f_re, f_im = _s5_scan(proj, mem_args, bsz, seq, perm, perm_t, *s5[l])
            p_fin_re.append(f_re.reshape(bsz, SSM_GROUPS, SSM_STATE))
            p_fin_im.append(f_im.reshape(bsz, SSM_GROUPS, SSM_STATE))
        else:
            mix, mem = _moba_prefill(proj, mem_args, k_aug, v_t, kmean, seg_mix, gq_rows[l - N_A_LAYERS][None],
                                     bsz, seq)
        h, proj = _layer_tail(h, mix, mem, l, *tail_weights, tm=512, next_in=next_in(l))
        if l == N_A_LAYERS - 1:
            k_new, v_new, k_aug, v_t, kmean = _kv_proj(h, g_kv_rows, w_kv_b, seg_kv, g_k_rows, tm=MOBA_BLOCK,
                                                       blocks_per_seq=nblk)
            p_k = k_new.reshape(bsz, MOBA_KV_HEADS, HEAD_DIM, seq).transpose(0, 3, 1, 2)
            p_v = v_new.reshape(bsz, MOBA_KV_HEADS, HEAD_DIM, seq).transpose(0, 3, 1, 2)
            kmean = kmean.reshape(bsz, nblk, KV_W)
    y_prompt = h.reshape(bsz, seq, D_MODEL)

    ck_t = cache_k.transpose(0, 2, 3, 1)
    cv_t = cache_v.transpose(0, 2, 3, 1)
    cmk_t = cache_mem_k.transpose(0, 1, 3, 4, 2)
    cmv_t = cache_mem_v.transpose(0, 1, 3, 4, 2)

    h = x_sample.reshape(n_s, D_MODEL)
    s_fin_re, s_fin_im = [], []
    s_k = s_v = None
    proj = _norm_matmul(h, g_mix_rows, w_in_b, 0, tm=n_s)
    for l in range(DEPTH):
        if l < N_A_LAYERS:
            mix, x_re, x_im = _s5_step(proj, state_ssm_re[l].reshape(n_s, N_STATE),
                                       state_ssm_im[l].reshape(n_s, N_STATE), *s5[l])
            s_fin_re.append(x_re.reshape(n_s, SSM_GROUPS, SSM_STATE))
            s_fin_im.append(x_im.reshape(n_s, SSM_GROUPS, SSM_STATE))
            mq_t, = _decode_prep(proj, seg_kv, gmq_rows[l][None])
        else:
            mq_t, q_t, knew_t, vnew_t = _decode_prep(
                proj, seg_kv, gmq_rows[l][None],
                moba=(seg_mix, gq_rows[l - N_A_LAYERS][None], s_k.reshape(n_s, KV_W), s_v.reshape(n_s, KV_W)))
            mix = _moba_decode(q_t, knew_t, vnew_t, ck_t, cv_t, page_table.reshape(-1)).T.astype(BF16)
        mem = _mem_decode(mq_t, cmk_t, cmv_t, l).T.astype(BF16)
        h, proj = _layer_tail(h, mix, mem, l, *tail_weights, tm=n_s, next_in=next_in(l))
        if l == N_A_LAYERS - 1:
            s_k, s_v = _kv_proj(h, g_kv_rows, w_kv_b, seg_kv, g_k_rows, tm=n_s)
    y_sample = h.reshape(n_s, 1, D_MODEL)

    return (y_prompt, y_sample,
            jnp.stack(p_fin_re), jnp.stack(p_fin_im), p_k, p_v, p_mem_k, p_mem_v,
            jnp.stack(s_fin_re), jnp.stack(s_fin_im),
            s_k.reshape(n_s, 1, MOBA_KV_HEADS, HEAD_DIM), s_v.reshape(n_s, 1, MOBA_KV_HEADS, HEAD_DIM))
```

```python
import functools
import math

import jax
import jax.numpy as jnp
from jax import lax
from jax.experimental import pallas as pl
from jax.experimental.pallas import tpu as pltpu

F32 = jnp.float32
BF16 = jnp.bfloat16

D_MODEL = 1024
DEPTH = 4
N_A_LAYERS = DEPTH // 2
HEAD_DIM = 64
MIX_W = 3 * D_MODEL // 4
MEM_W = D_MODEL // 4
MEM_HEADS = MEM_W // HEAD_DIM
SSM_GROUP = 16
SSM_GROUPS = MIX_W // SSM_GROUP
SSM_STATE = 64
N_STATE = SSM_GROUPS * SSM_STATE
MOBA_HEADS = MIX_W // HEAD_DIM
MOBA_KV_HEADS = 4
Q_PER_KV = MOBA_HEADS // MOBA_KV_HEADS
KV_W = MOBA_KV_HEADS * HEAD_DIM
MOBA_BLOCK = 256
MOBA_TOPK = 3
D_FF = ((-(-8 * D_MODEL // 3) + 255) // 256) * 256
EPS = 1e-6
ATTN_SCALE = HEAD_DIM ** -0.5
MASK_VALUE = -1e30
KAUG_W = 128
VAUG_H = HEAD_DIM + 16
LOG2_E = math.log2(math.e)

SCAN_SUB = 8
SCAN_STEPS = 32
SCAN_T = SCAN_SUB * SCAN_STEPS
SSM_KB = 256
N_KB = MIX_W // SSM_KB
KB_STATES = (SSM_KB // SSM_GROUP) * SSM_STATE
SCAN_CW = 512

VMEM_LIMIT_BYTES = 56 * 1024 * 1024


def _cparams(*sem):
    return pltpu.CompilerParams(dimension_semantics=sem, vmem_limit_bytes=VMEM_LIMIT_BYTES)


def _rms(x, g_row):
    ms = jnp.mean(x * x, axis=-1, keepdims=True)
    return x * lax.rsqrt(ms + EPS) * g_row


def _seg_rms(x, seg_mat, g_row):
    sq = x * x
    hi = sq.astype(BF16)
    lo = (sq - hi.astype(F32)).astype(BF16)
    ms = (jnp.dot(hi, seg_mat, preferred_element_type=F32)
          + jnp.dot(lo, seg_mat, preferred_element_type=F32))
    return x * lax.rsqrt(ms + EPS) * g_row


def _dot_nt(a, b, precision=None):
    return lax.dot_general(a, b, (((1,), (1,)), ((), ())), precision=precision,
                           preferred_element_type=F32)


def _topk_keep(gate, n_valid, axis):
    n = gate.shape[axis]
    idx = lax.broadcasted_iota(jnp.int32, gate.shape, axis)
    rank = jnp.zeros(gate.shape, jnp.int32)
    for j in range(n):
        gj = lax.slice_in_dim(gate, j, j + 1, axis=axis)
        beats = (gj > gate) | ((gj == gate) & (j < idx))
        rank = rank + jnp.where(beats, jnp.where(j < n_valid, 1, 0), 0)
    return (rank < MOBA_TOPK) & (idx < n_valid)


def _norm_matmul_kernel(x_ref, g_ref, w_ref, o_ref):
    a = _rms(x_ref[...], g_ref[...]).astype(BF16)
    o_ref[...] = jnp.dot(a, w_ref[...], preferred_element_type=F32)


def _norm_matmul(x, g_rows, w, layer, tm):
    n, d = x.shape
    nout = w.shape[2]
    return pl.pallas_call(
        _norm_matmul_kernel,
        grid=(n // tm,),
        in_specs=[pl.BlockSpec((tm, d), lambda i: (i, 0)),
                  pl.BlockSpec((None, 1, d), lambda i: (layer, 0, 0)),
                  pl.BlockSpec((None, d, nout), lambda i: (layer, 0, 0))],
        out_specs=pl.BlockSpec((tm, nout), lambda i: (i, 0)),
        out_shape=jax.ShapeDtypeStruct((n, nout), F32),
        compiler_params=_cparams("parallel"),
        name="norm_matmul",
    )(x, g_rows, w)


def _kv_kernel(x_ref, g_ref, w_ref, seg_ref, gk_ref, k_ref, v_ref, *extra, half, blocks_per_seq):
    a = _rms(x_ref[...], g_ref[0]).astype(BF16)
    kv = jnp.dot(a, w_ref[0], preferred_element_type=F32)
    k = _seg_rms(kv[:, :half], seg_ref[...], gk_ref[0])
    v = kv[:, half:]
    if not blocks_per_seq:
        k_ref[0] = k
        v_ref[0] = v
    else:
        kaug_ref, vt_ref, kmean_ref = extra
        lane = lax.broadcasted_iota(jnp.int32, (k.shape[0], KAUG_W), 1)
        tail = jnp.where(lane == HEAD_DIM + pl.program_id(1) % blocks_per_seq, 1.0, 0.0)
        for g in range(MOBA_KV_HEADS):
            col = k[:, (g // 2) * KAUG_W:(g // 2 + 1) * KAUG_W]
            if g % 2:
                col = pltpu.roll(col, HEAD_DIM, axis=1)
            kaug_ref[g] = jnp.where(lane < HEAD_DIM, col, tail).astype(BF16)
        v_t = v.T
        k_ref[0] = k.T
        v_ref[0] = v_t
        ones_row = jnp.where(lax.broadcasted_iota(jnp.int32, (VAUG_H - HEAD_DIM, k.shape[0]), 0) == 0, 1.0, 0.0)
        vt_ref[...] = jnp.concatenate(
            [piece for g in range(MOBA_KV_HEADS) for piece in (v_t[g * HEAD_DIM:(g + 1) * HEAD_DIM], ones_row)],
            axis=0).astype(BF16)
        kmean_ref[0] = jnp.mean(k, axis=0, keepdims=True)


def _kv_proj(x, g_rows, w, seg_mat, gk_rows, tm, blocks_per_seq=0):
    n, d = x.shape
    layers, _, two_half = w.shape
    half = two_half // 2
    out_shape = [jax.ShapeDtypeStruct((layers, n, half), F32)] * 2
    out_specs = [pl.BlockSpec((1, tm, half), lambda l, i: (l, i, 0))] * 2
    if blocks_per_seq:
        assert layers == 1 and tm == MOBA_BLOCK and half == KV_W
        nseq = n // (tm * blocks_per_seq)
        out_shape = [jax.ShapeDtypeStruct((nseq, half, tm * blocks_per_seq), F32)] * 2
        out_specs = [pl.BlockSpec((1, half, tm), lambda l, i: (i // blocks_per_seq, 0, i % blocks_per_seq))] * 2
        out_shape += [jax.ShapeDtypeStruct((MOBA_KV_HEADS, n, KAUG_W), BF16),
                      jax.ShapeDtypeStruct((MOBA_KV_HEADS * VAUG_H, n), BF16),
                      jax.ShapeDtypeStruct((n // tm, 1, half), F32)]
        out_specs += [pl.BlockSpec((MOBA_KV_HEADS, tm, KAUG_W), lambda l, i: (0, i, 0)),
                      pl.BlockSpec((MOBA_KV_HEADS * VAUG_H, tm), lambda l, i: (0, i)),
                      pl.BlockSpec((1, 1, half), lambda l, i: (i, 0, 0))]
    return pl.pallas_call(
        functools.partial(_kv_kernel, half=half, blocks_per_seq=blocks_per_seq),
        grid=(layers, n // tm),
        in_specs=[pl.BlockSpec((tm, d), lambda l, i: (i, 0)),
                  pl.BlockSpec((1, 1, d), lambda l, i: (l, 0, 0)),
                  pl.BlockSpec((1, d, two_half), lambda l, i: (l, 0, 0)),
                  pl.BlockSpec((half, half), lambda l, i: (0, 0)),
                  pl.BlockSpec((1, 1, half), lambda l, i: (l, 0, 0))],
        out_specs=out_specs,
        out_shape=out_shape,
        compiler_params=_cparams("parallel", "parallel"),
        name="kv_proj",
    )(x, g_rows, w, seg_mat, gk_rows)


def _layer_tail_kernel(h_ref, mix_ref, mem_ref, wmix_ref, wmem_ref, g_ref, wgu_ref, wd_ref, *rest):
    h1 = (h_ref[...]
          + jnp.dot(mix_ref[...], wmix_ref[...], preferred_element_type=F32)
          + jnp.dot(mem_ref[...], wmem_ref[...], preferred_element_type=F32))
    a = _rms(h1, g_ref[...]).astype(BF16)
    gate = jnp.dot(a, wgu_ref[:, :D_FF], preferred_element_type=F32)
    up = jnp.dot(a, wgu_ref[:, D_FF:], preferred_element_type=F32)
    act = (gate * jax.nn.sigmoid(gate) * up).astype(BF16)
    h2 = h1 + jnp.dot(act, wd_ref[...], preferred_element_type=F32)
    if len(rest) == 1:
        rest[0][...] = h2
    else:
        gn_ref, win_ref, o_ref, proj_ref = rest
        o_ref[...] = h2
        proj_ref[...] = jnp.dot(_rms(h2, gn_ref[...]).astype(BF16), win_ref[...], preferred_element_type=F32)


def _layer_tail(h, mix, mem, layer, w_out, g_ffn, w_gu, w_down, tm, next_in=None):
    n, d = h.shape
    row = lambda w: pl.BlockSpec((tm, w), lambda i: (i, 0))

    def resident(a, l, rows=None, row_block=0):
        shape = (None, rows or a.shape[1], a.shape[2])
        return pl.BlockSpec(shape, lambda i: (l, row_block, 0), pipeline_mode=pl.Buffered(1))

    args = [h, mix, mem, w_out, w_out, g_ffn, w_gu, w_down]
    in_specs = [row(d), row(MIX_W), row(MEM_W),
                resident(w_out, layer, MIX_W), resident(w_out, layer, MEM_W, MIX_W // MEM_W),
                resident(g_ffn, layer), resident(w_gu, layer), resident(w_down, layer)]
    out_shape = [jax.ShapeDtypeStruct((n, d), F32)]
    if next_in is not None:
        args += list(next_in)
        in_specs += [resident(a, layer + 1) for a in next_in]
        out_shape.append(jax.ShapeDtypeStruct((n, next_in[1].shape[2]), F32))
    out = pl.pallas_call(
        _layer_tail_kernel,
        grid=(n // tm,),
        in_specs=in_specs,
        out_specs=[row(s.shape[1]) for s in out_shape],
        out_shape=out_shape,
        compiler_params=_cparams("parallel"),
        name="layer_tail",
    )(*args)
    return out if next_in is not None else (out[0], None)


def _ssm_prep_kernel(are_ref, aim_ref, ldt_ref, bre_ref, bim_ref, abr_ref, abi_ref, bbr_ref, bbi_ref):
    dt = jnp.exp(ldt_ref[0])
    lam_re = jnp.minimum(are_ref[0], -1e-4)
    lam_im = aim_ref[0]
    mag = jnp.exp(dt * lam_re)
    ang = dt * lam_im
    ab_re = mag * jnp.cos(ang)
    ab_im = mag * jnp.sin(ang)
    den = lam_re * lam_re + lam_im * lam_im
    num_re = ab_re - 1.0
    f_re = (num_re * lam_re + ab_im * lam_im) / den
    f_im = (ab_im * lam_re - num_re * lam_im) / den
    b_re = bre_ref[0]
    b_im = bim_ref[0]
    abr_ref[0] = ab_re
    abi_ref[0] = ab_im
    bbr_ref[0] = f_re * b_re - f_im * b_im
    bbi_ref[0] = f_re * b_im + f_im * b_re


def _ssm_prep(a_re, a_im, log_dt, b_re, b_im):
    layers = a_re.shape[0]
    big = pl.BlockSpec((1, MIX_W, SSM_STATE), lambda l: (l, 0, 0))
    return pl.pallas_call(
        _ssm_prep_kernel,
        grid=(layers,),
        in_specs=[big, big, pl.BlockSpec((1, MIX_W, 1), lambda l: (l, 0, 0)), big, big],
        out_specs=[big] * 4,
        out_shape=[jax.ShapeDtypeStruct((layers, MIX_W, SSM_STATE), F32)] * 4,
        compiler_params=_cparams("parallel"),
        name="ssm_prep",
    )(a_re, a_im, log_dt, b_re, b_im)


def _re_cols(n0):
    kb, off = divmod(n0, KB_STATES)
    return kb * 2 * KB_STATES + off


def _s5_output(y_state, u, d_ref, wglu_ref):
    y = jax.nn.gelu(y_state + d_ref[...] * u)
    z = jnp.dot(y.astype(BF16), wglu_ref[...], preferred_element_type=F32)
    return (y * jax.nn.sigmoid(z)).astype(BF16)


def _s5_scan_kernel(u_ref, perm_ref, permt_ref, bmat_ref, are_ref, aim_ref, cmat_ref, d_ref, wglu_ref,
                    mq_ref, mk_ref, mv_ref, mseg_ref, mg_ref,
                    y_ref, mem_ref, fre_ref, fim_ref, x_scr, apr_scr, api_scr, str_scr, sti_scr):
    c = pl.program_id(1)

    @pl.when(c == 0)
    def _():
        str_scr[...] = jnp.zeros_like(str_scr)
        sti_scr[...] = jnp.zeros_like(sti_scr)
        ar, ai = are_ref[...], aim_ref[...]
        pr, pi = ar, ai
        for i in range(SCAN_STEPS):
            apr_scr[i:i + 1, :] = pr
            api_scr[i:i + 1, :] = pi
            pr, pi = pr * ar - pi * ai, pr * ai + pi * ar

    mem_ref[...] = _mem_attend(mq_ref[...], mk_ref[0], mv_ref[0], mseg_ref[...], mg_ref[...])

    u = u_ref[...]
    up = jnp.dot(perm_ref[...], u.astype(BF16), preferred_element_type=F32).astype(BF16)
    sub = lax.broadcasted_iota(jnp.int32, (SCAN_SUB, SCAN_CW), 0)
    ys = []
    for kb in range(N_KB):
        x_scr[:, kb * 2 * KB_STATES:(kb + 1) * 2 * KB_STATES] = jnp.dot(
            up[:, kb * SSM_KB:(kb + 1) * SSM_KB], bmat_ref[kb], preferred_element_type=F32)
        for n0 in range(kb * KB_STATES, (kb + 1) * KB_STATES, SCAN_CW):
            rc = _re_cols(n0)
            ic = rc + KB_STATES
            ar = jnp.broadcast_to(are_ref[:, n0:n0 + SCAN_CW], (SCAN_SUB, SCAN_CW))
            ai = jnp.broadcast_to(aim_ref[:, n0:n0 + SCAN_CW], (SCAN_SUB, SCAN_CW))
            xr = jnp.zeros((SCAN_SUB, SCAN_CW), F32)
            xi = jnp.zeros((SCAN_SUB, SCAN_CW), F32)
            for i in range(SCAN_STEPS):
                r0 = i * SCAN_SUB
                xr, xi = (ar * xr - ai * xi + x_scr[r0:r0 + SCAN_SUB, rc:rc + SCAN_CW],
                          ar * xi + ai * xr + x_scr[r0:r0 + SCAN_SUB, ic:ic + SCAN_CW])
                x_scr[r0:r0 + SCAN_SUB, rc:rc + SCAN_CW] = xr
                x_scr[r0:r0 + SCAN_SUB, ic:ic + SCAN_CW] = xi

            a_t_r = apr_scr[SCAN_STEPS - 1:SCAN_STEPS, n0:n0 + SCAN_CW]
            a_t_i = api_scr[SCAN_STEPS - 1:SCAN_STEPS, n0:n0 + SCAN_CW]
            cr = str_scr[:, n0:n0 + SCAN_CW]
            ci = sti_scr[:, n0:n0 + SCAN_CW]
            car_r = jnp.zeros((SCAN_SUB, SCAN_CW), F32)
            car_i = jnp.zeros((SCAN_SUB, SCAN_CW), F32)
            for j in range(SCAN_SUB):
                car_r = jnp.where(sub == j, cr, car_r)
                car_i = jnp.where(sub == j, ci, car_i)
                cr, ci = (xr[j:j + 1] + a_t_r * cr - a_t_i * ci,
                          xi[j:j + 1] + a_t_r * ci + a_t_i * cr)
            str_scr[:, n0:n0 + SCAN_CW] = cr
            sti_scr[:, n0:n0 + SCAN_CW] = ci

            for i in range(SCAN_STEPS):
                r0 = i * SCAN_SUB
                pr = apr_scr[i:i + 1, n0:n0 + SCAN_CW]
                pi = api_scr[i:i + 1, n0:n0 + SCAN_CW]
                x_scr[r0:r0 + SCAN_SUB, rc:rc + SCAN_CW] = (x_scr[r0:r0 + SCAN_SUB, rc:rc + SCAN_CW]
                                                            + (pr * car_r - pi * car_i))
                x_scr[r0:r0 + SCAN_SUB, ic:ic + SCAN_CW] = (x_scr[r0:r0 + SCAN_SUB, ic:ic + SCAN_CW]
                                                            + (pr * car_i + pi * car_r))
        ys.append(jnp.dot(x_scr[:, kb * 2 * KB_STATES:(kb + 1) * 2 * KB_STATES].astype(BF16), cmat_ref[kb],
                          preferred_element_type=F32))
    yp = jnp.concatenate(ys, axis=1)
    hi = yp.astype(BF16)
    lo = (yp - hi.astype(F32)).astype(BF16)
    y_state = (jnp.dot(permt_ref[...], hi, preferred_element_type=F32)
               + jnp.dot(permt_ref[...], lo, preferred_element_type=F32))
    y_ref[...] = _s5_output(y_state, u, d_ref, wglu_ref)
    fre_ref[0] = str_scr[...]
    fim_ref[0] = sti_scr[...]


def _s5_scan(proj, mem_args, bsz, seq, perm, perm_t, bmat, a_re, a_im, cmat, d_row, w_glu):
    nchunk = seq // SCAN_T
    const2 = lambda b, c: (0, 0)
    const3 = lambda b, c: (0, 0, 0)
    state_spec = pl.BlockSpec((1, 1, N_STATE), lambda b, c: (b, 0, 0))
    return pl.pallas_call(
        _s5_scan_kernel,
        grid=(bsz, nchunk),
        in_specs=[pl.BlockSpec((SCAN_T, MIX_W), lambda b, c: (b * nchunk + c, 0)),
                  pl.BlockSpec((SCAN_T, SCAN_T), const2),
                  pl.BlockSpec((SCAN_T, SCAN_T), const2),
                  pl.BlockSpec((N_KB, SSM_KB, 2 * KB_STATES), const3),
                  pl.BlockSpec((1, N_STATE), const2),
                  pl.BlockSpec((1, N_STATE), const2),
                  pl.BlockSpec((N_KB, 2 * KB_STATES, SSM_KB), const3),
                  pl.BlockSpec((1, MIX_W), const2),
                  pl.BlockSpec((MIX_W, MIX_W), const2)] + _mem_specs(SCAN_T, nchunk, mem_args[0].shape[1]),
        out_specs=[pl.BlockSpec((SCAN_T, MIX_W), lambda b, c: (b * nchunk + c, 0)),
                   pl.BlockSpec((SCAN_T, MEM_W), lambda b, c: (b * nchunk + c, 0)), state_spec, state_spec],
        out_shape=[jax.ShapeDtypeStruct((bsz * seq, MIX_W), BF16),
                   jax.ShapeDtypeStruct((bsz * seq, MEM_W), BF16),
                   jax.ShapeDtypeStruct((bsz, 1, N_STATE), F32),
                   jax.ShapeDtypeStruct((bsz, 1, N_STATE), F32)],
        scratch_shapes=[pltpu.VMEM((SCAN_T, 2 * N_STATE), F32),
                        pltpu.VMEM((SCAN_STEPS, N_STATE), F32),
                        pltpu.VMEM((SCAN_STEPS, N_STATE), F32),
                        pltpu.VMEM((1, N_STATE), F32),
                        pltpu.VMEM((1, N_STATE), F32)],
        compiler_params=_cparams("parallel", "arbitrary"),
        name="s5_scan",
    )(proj, perm, perm_t, bmat, a_re, a_im, cmat, d_row, w_glu, proj, *mem_args)


def _s5_step_kernel(u_ref, hre_ref, him_ref, bmat_ref, are_ref, aim_ref, cmat_ref, d_ref, wglu_ref,
                    y_ref, xre_ref, xim_ref):
    u = u_ref[...]
    ub = u.astype(BF16)
    ys = []
    for kb in range(N_KB):
        bu = jnp.dot(ub[:, kb * SSM_KB:(kb + 1) * SSM_KB], bmat_ref[kb], preferred_element_type=F32)
        n0 = kb * KB_STATES
        ar, ai = are_ref[:, n0:n0 + KB_STATES], aim_ref[:, n0:n0 + KB_STATES]
        hr, hi = hre_ref[:, n0:n0 + KB_STATES], him_ref[:, n0:n0 + KB_STATES]
        xr = ar * hr - ai * hi + bu[:, :KB_STATES]
        xi = ar * hi + ai * hr + bu[:, KB_STATES:]
        xre_ref[:, n0:n0 + KB_STATES] = xr
        xim_ref[:, n0:n0 + KB_STATES] = xi
        x = jnp.concatenate([xr, xi], axis=1).astype(BF16)
        ys.append(jnp.dot(x, cmat_ref[kb], preferred_element_type=F32))
    y_ref[...] = _s5_output(jnp.concatenate(ys, axis=1), u, d_ref, wglu_ref)


def _s5_step(proj, h_re, h_im, bmat, a_re, a_im, cmat, d_row, w_glu):
    n = proj.shape[0]
    const2 = lambda i: (0, 0)
    const3 = lambda i: (0, 0, 0)
    return pl.pallas_call(
        _s5_step_kernel,
        grid=(1,),
        in_specs=[pl.BlockSpec((n, MIX_W), const2),
                  pl.BlockSpec((n, N_STATE), const2),
                  pl.BlockSpec((n, N_STATE), const2),
                  pl.BlockSpec((N_KB, SSM_KB, 2 * KB_STATES), const3),
                  pl.BlockSpec((1, N_STATE), const2),
                  pl.BlockSpec((1, N_STATE), const2),
                  pl.BlockSpec((N_KB, 2 * KB_STATES, SSM_KB), const3),
                  pl.BlockSpec((1, MIX_W), const2),
                  pl.BlockSpec((MIX_W, MIX_W), const2)],
        out_specs=[pl.BlockSpec((n, MIX_W), const2),
                   pl.BlockSpec((n, N_STATE), const2),
                   pl.BlockSpec((n, N_STATE), const2)],
        out_shape=[jax.ShapeDtypeStruct((n, MIX_W), BF16),
                   jax.ShapeDtypeStruct((n, N_STATE), F32),
                   jax.ShapeDtypeStruct((n, N_STATE), F32)],
        compiler_params=_cparams("arbitrary"),
        name="s5_step",
    )(proj, h_re, h_im, bmat, a_re, a_im, cmat, d_row, w_glu)


def _mem_attend(q, k, v, seg_mat, g_row):
    q = _seg_rms(q, seg_mat, g_row) * ATTN_SCALE
    lane_head = lax.broadcasted_iota(jnp.int32, q.shape, 1) // HEAD_DIM
    out = jnp.zeros(q.shape, F32)
    for h in range(MEM_HEADS):
        qh = jnp.where(lane_head == h, q, 0.0).astype(BF16)
        s = _dot_nt(qh, k)
        p = jnp.exp(s - jnp.max(s, axis=-1, keepdims=True))
        o = jnp.dot(p.astype(BF16), v, preferred_element_type=F32) / jnp.sum(p, axis=-1, keepdims=True)
        out = jnp.where(lane_head == h, o, out)
    return out.astype(BF16)


def _mem_specs(rows, blocks_per_seq, tokens):
    const2 = lambda b, i: (0, 0)
    kv_spec = pl.BlockSpec((1, tokens, MEM_W), lambda b, i: (b, 0, 0))
    return [pl.BlockSpec((rows, MEM_W), lambda b, i: (b * blocks_per_seq + i, MIX_W // MEM_W)),
            kv_spec, kv_spec, pl.BlockSpec((MEM_W, MEM_W), const2), pl.BlockSpec((1, MEM_W), const2)]


def _decode_prep_kernel(proj_ref, segk_ref, gmq_ref, *rest, with_moba):
    x = proj_ref[...]
    if with_moba:
        segm_ref, gq_ref, knew_ref, vnew_ref, mqt_ref, qt_ref, knt_ref, vnt_ref = rest
        qt_ref[...] = _seg_rms(x[:, :MIX_W], segm_ref[...], gq_ref[...]).T
        knt_ref[...] = knew_ref[...].T
        vnt_ref[...] = vnew_ref[...].T
    else:
        mqt_ref, = rest
    mqt_ref[...] = (_seg_rms(x[:, MIX_W:], segk_ref[...], gmq_ref[...]) * ATTN_SCALE).T


def _decode_prep(proj, seg_kv, gmq_row, moba=None):
    n, d = proj.shape
    const2 = lambda i: (0, 0)
    args = [proj, seg_kv, gmq_row]
    in_specs = [pl.BlockSpec((n, d), const2), pl.BlockSpec((MEM_W, MEM_W), const2), pl.BlockSpec((1, MEM_W), const2)]
    out_shape = [jax.ShapeDtypeStruct((MEM_W, n), F32)]
    if moba is not None:
        args += list(moba)
        in_specs += [pl.BlockSpec((MIX_W, MIX_W), const2), pl.BlockSpec((1, MIX_W), const2),
                     pl.BlockSpec((n, KV_W), const2), pl.BlockSpec((n, KV_W), const2)]
        out_shape += [jax.ShapeDtypeStruct((MIX_W, n), F32), jax.ShapeDtypeStruct((KV_W, n), F32),
                      jax.ShapeDtypeStruct((KV_W, n), F32)]
    return pl.pallas_call(
        functools.partial(_decode_prep_kernel, with_moba=moba is not None),
        grid=(1,),
        in_specs=in_specs,
        out_specs=[pl.BlockSpec(s.shape, const2) for s in out_shape],
        out_shape=out_shape,
        compiler_params=_cparams("arbitrary"),
        name="decode_prep",
    )(*args)


def _mem_decode_kernel(qt_ref, kt_ref, vt_ref, ot_ref, s_scr, p_scr, *, rows):
    i = pl.program_id(0)
    tokens = kt_ref.shape[-1]
    lanes = 128
    nchunk = tokens // lanes

    @pl.when(i == 0)
    def _():
        ot_ref[...] = jnp.zeros_like(ot_ref)

    for r in range(rows):
        qrep = _column_lanes(qt_ref, i * rows + r, lanes)
        for h in range(MEM_HEADS):
            qh = qrep[h * HEAD_DIM:(h + 1) * HEAD_DIM]
            for c in range(nchunk):
                s_scr[r * MEM_HEADS + h:r * MEM_HEADS + h + 1, c * lanes:(c + 1) * lanes] = jnp.sum(
                    qh * kt_ref[r, h, :, c * lanes:(c + 1) * lanes], axis=0, keepdims=True)
    s = s_scr[...]
    pr = jnp.exp(s - jnp.max(s, axis=1, keepdims=True))
    p_scr[...] = pr / jnp.sum(pr, axis=1, keepdims=True)
    for r in range(rows):
        accs = [p_scr[r * MEM_HEADS + h:r * MEM_HEADS + h + 1, :] * vt_ref[r, h] for h in range(MEM_HEADS)]
        _add_lane_sum_column(ot_ref, jnp.concatenate(accs, axis=0), i * rows + r)


def _mem_decode(mq_t, mem_kt, mem_vt, layer, rows=8):
    n = mq_t.shape[1]
    tokens = mem_kt.shape[-1]
    kv_spec = pl.BlockSpec((None, rows, MEM_HEADS, HEAD_DIM, tokens), lambda i: (layer, i, 0, 0, 0))
    return pl.pallas_call(
        functools.partial(_mem_decode_kernel, rows=rows),
        grid=(n // rows,),
        in_specs=[pl.BlockSpec((MEM_W, n), lambda i: (0, 0)), kv_spec, kv_spec],
        out_specs=pl.BlockSpec((MEM_W, n), lambda i: (0, 0)),
        out_shape=jax.ShapeDtypeStruct((MEM_W, n), F32),
        scratch_shapes=[pltpu.VMEM((rows * MEM_HEADS, tokens), F32), pltpu.VMEM((rows * MEM_HEADS, tokens), F32)],
        compiler_params=_cparams("arbitrary"),
        name="mem_decode",
    )(mq_t, mem_kt, mem_vt)


def _moba_prefill_kernel(q_ref, k_ref, vt_ref, kmean_ref, seg_ref, g_ref,
                         mq_ref, mk_ref, mv_ref, mseg_ref, mg_ref, o_ref, mem_ref,
                         qa_scr, s_scr, mcur_scr, alpha_scr, m_scr, acc_scr, *, nblk):
    mem_ref[...] = _mem_attend(mq_ref[...], mk_ref[0], mv_ref[0], mseg_ref[...], mg_ref[...])
    i = pl.program_id(1)
    blk = MOBA_BLOCK
    qn_t = _seg_rms(q_ref[...], seg_ref[...], g_ref[...]).T
    qs_t = (qn_t * (ATTN_SCALE * LOG2_E)).astype(BF16)
    kmean = kmean_ref[0]
    cols = Q_PER_KV * blk
    key_idx = lax.broadcasted_iota(jnp.int32, (blk, cols), 0)
    row_idx = lax.broadcasted_iota(jnp.int32, (blk, cols), 1) % blk
    causal = key_idx <= row_idx
    pad_t = jnp.zeros((KAUG_W - HEAD_DIM - nblk, blk), BF16)

    def start_of(n):
        return pl.multiple_of(jnp.where(n == 0, i, n - 1) * blk, blk)

    def score_phase(n, slot, g, diagonal=False):
        s = jnp.dot(k_ref[g, pl.ds(start_of(n), blk), :], qa_scr[g], preferred_element_type=F32)
        if diagonal:
            s = jnp.where(causal, s, MASK_VALUE)
        s_scr[slot, g] = s
        m_old = m_scr[g]
        m_new = jnp.maximum(m_old, jnp.max(s, axis=0, keepdims=True))
        alpha_scr[slot, g] = jnp.exp2(m_old - m_new)
        mcur_scr[slot, g] = m_new
        m_scr[g] = m_new

    def value_phase(n, slot, g):
        p = jnp.exp2(s_scr[slot, g] - mcur_scr[slot, g]).astype(BF16)
        acc_scr[g] = alpha_scr[slot, g] * acc_scr[g] + jnp.dot(
            vt_ref[g * VAUG_H:(g + 1) * VAUG_H, pl.ds(start_of(n), blk)], p,
            preferred_element_type=F32)

    m_scr[...] = jnp.full(m_scr.shape, MASK_VALUE, F32)
    acc_scr[...] = jnp.zeros_like(acc_scr)
    for g in range(MOBA_KV_HEADS):
        km = kmean[:, g * HEAD_DIM:(g + 1) * HEAD_DIM]
        parts = []
        for hh in range(Q_PER_KV):
            h = g * Q_PER_KV + hh
            gate_t = jnp.dot(km, qn_t[h * HEAD_DIM:(h + 1) * HEAD_DIM], precision=lax.Precision.HIGHEST,
                             preferred_element_type=F32)
            keep_t = _topk_keep(gate_t, i, axis=0)
            past_t = lax.broadcasted_iota(jnp.int32, gate_t.shape, 0) < i
            bias_t = jnp.where(past_t & ~keep_t, MASK_VALUE, 0.0).astype(BF16)
            parts.append(jnp.concatenate([qs_t[h * HEAD_DIM:(h + 1) * HEAD_DIM], bias_t, pad_t], axis=0))
        qa_scr[g] = jnp.concatenate(parts, axis=1)
        score_phase(0, 0, g, diagonal=True)

    def pair_step(t, _):
        for g in range(MOBA_KV_HEADS):
            value_phase(2 * t, 0, g)
            score_phase(2 * t + 1, 1, g)
        for g in range(MOBA_KV_HEADS):
            value_phase(2 * t + 1, 1, g)
            score_phase(2 * t + 2, 0, g)
        return 0

    lax.fori_loop(0, i // 2, pair_step, 0)

    @pl.when(i % 2 == 1)
    def _():
        for g in range(MOBA_KV_HEADS):
            value_phase(i - 1, 0, g)
            score_phase(i, 1, g)
        for g in range(MOBA_KV_HEADS):
            value_phase(i, 1, g)

    @pl.when(i % 2 == 0)
    def _():
        for g in range(MOBA_KV_HEADS):
            value_phase(i, 0, g)

    outs = []
    for g in range(MOBA_KV_HEADS):
        out_t = acc_scr[g, :HEAD_DIM] / acc_scr[g, HEAD_DIM:HEAD_DIM + 1]
        outs += [out_t[:, hh * blk:(hh + 1) * blk] for hh in range(Q_PER_KV)]
    o_ref[...] = jnp.concatenate(outs, axis=0).T.astype(BF16)


def _moba_prefill(proj, mem_args, k_aug, v_t, kmean, seg_mat, g_row, bsz, seq):
    nblk = seq // MOBA_BLOCK
    cols = Q_PER_KV * MOBA_BLOCK
    return pl.pallas_call(
        functools.partial(_moba_prefill_kernel, nblk=nblk),
        grid=(bsz, nblk),
        in_specs=[pl.BlockSpec((MOBA_BLOCK, MIX_W), lambda b, i: (b * nblk + i, 0)),
                  pl.BlockSpec((MOBA_KV_HEADS, seq, KAUG_W), lambda b, i: (0, b, 0)),
                  pl.BlockSpec((MOBA_KV_HEADS * VAUG_H, seq), lambda b, i: (0, b)),
                  pl.BlockSpec((1, nblk, KV_W), lambda b, i: (b, 0, 0)),
                  pl.BlockSpec((MIX_W, MIX_W), lambda b, i: (0, 0)),
                  pl.BlockSpec((1, MIX_W), lambda b, i: (0, 0))] + _mem_specs(MOBA_BLOCK, nblk, mem_args[0].shape[1]),
        out_specs=[pl.BlockSpec((MOBA_BLOCK, MIX_W), lambda b, i: (b * nblk + i, 0)),
                   pl.BlockSpec((MOBA_BLOCK, MEM_W), lambda b, i: (b * nblk + i, 0))],
        out_shape=[jax.ShapeDtypeStruct((bsz * seq, MIX_W), BF16),
                   jax.ShapeDtypeStruct((bsz * seq, MEM_W), BF16)],
        scratch_shapes=[pltpu.VMEM((MOBA_KV_HEADS, KAUG_W, cols), BF16),
                        pltpu.VMEM((2, MOBA_KV_HEADS, MOBA_BLOCK, cols), F32),
                        pltpu.VMEM((2, MOBA_KV_HEADS, 1, cols), F32),
                        pltpu.VMEM((2, MOBA_KV_HEADS, 1, cols), F32),
                        pltpu.VMEM((MOBA_KV_HEADS, 1, cols), F32),
                        pltpu.VMEM((MOBA_KV_HEADS, VAUG_H, cols), F32)],
        compiler_params=_cparams("parallel", "arbitrary"),
        name="moba_prefill",
    )(proj, k_aug, v_t, kmean, seg_mat, g_row, proj, *mem_args)


def _moba_decode_kernel(pt_ref, qt_ref, knt_ref, vnt_ref, ck_hbm, cv_hbm, ot_ref,
                        kbuf, vbuf, sem, s_scr, p_scr, own_scr, *, n_pages, page):
    b = pl.program_id(0)
    nblk = n_pages * page // MOBA_BLOCK
    slot = b % 2

    def page_copies(row, into):
        copies = []
        for p in range(n_pages):
            src = pt_ref[row * n_pages + p]
            copies.append(pltpu.make_async_copy(ck_hbm.at[src], kbuf.at[into, p], sem.at[0, into]))
            copies.append(pltpu.make_async_copy(cv_hbm.at[src], vbuf.at[into, p], sem.at[1, into]))
        return copies

    @pl.when(b == 0)
    def _():
        ot_ref[...] = jnp.zeros_like(ot_ref)
        for c in page_copies(0, 0):
            c.start()

    @pl.when(b + 1 < pl.num_programs(0))
    def _():
        for c in page_copies(b + 1, 1 - slot):
            c.start()

    for c in page_copies(b, slot):
        c.wait()
    k_pages = [kbuf.at[slot, p] for p in range(n_pages)]
    v_pages = [vbuf.at[slot, p] for p in range(n_pages)]

    qrep = _column_lanes(qt_ref, b, page)
    knrep = _column_lanes(knt_ref, b, page)
    vnrep = _column_lanes(vnt_ref, b, page)

    s_scr[MOBA_HEADS:, :] = jnp.zeros((s_scr.shape[0] - MOBA_HEADS, s_scr.shape[1]), F32)
    own_scr[MOBA_HEADS:, :] = jnp.zeros((own_scr.shape[0] - MOBA_HEADS, page), F32)
    for h in range(MOBA_HEADS):
        g = h // Q_PER_KV
        qh = qrep[h * HEAD_DIM:(h + 1) * HEAD_DIM]
        for p in range(n_pages):
            s_scr[h:h + 1, p * page:(p + 1) * page] = jnp.sum(qh * k_pages[p][g], axis=0, keepdims=True)
        own_scr[h:h + 1, :] = jnp.sum(qh * knrep[g * HEAD_DIM:(g + 1) * HEAD_DIM], axis=0, keepdims=True)
    s = s_scr[...]

    gates = [jnp.sum(s[:, j * MOBA_BLOCK:(j + 1) * MOBA_BLOCK], axis=1, keepdims=True) for j in range(nblk)]
    parts = []
    for j in range(nblk):
        rank = jnp.zeros(gates[j].shape, jnp.int32)
        for jj in range(nblk):
            if jj != j:
                beats = (gates[jj] >= gates[j]) if jj < j else (gates[jj] > gates[j])
                rank = rank + jnp.where(beats, 1, 0)
        bias = jnp.where(rank < MOBA_TOPK, 0.0, MASK_VALUE)
        parts.append(s[:, j * MOBA_BLOCK:(j + 1) * MOBA_BLOCK] * ATTN_SCALE + bias)
    s = jnp.concatenate(parts, axis=1)
    s_own = own_scr[:, :1] * ATTN_SCALE
    m = jnp.maximum(jnp.max(s, axis=1, keepdims=True), s_own)
    pr = jnp.exp(s - m)
    p_own = jnp.exp(s_own - m)
    inv = 1.0 / (jnp.sum(pr, axis=1, keepdims=True) + p_own)
    p_scr[...] = pr * inv
    own_scr[...] = jnp.broadcast_to(p_own * inv * (1.0 / page), own_scr.shape)

    accs = []
    for h in range(MOBA_HEADS):
        g = h // Q_PER_KV
        acc = own_scr[h:h + 1, :] * vnrep[g * HEAD_DIM:(g + 1) * HEAD_DIM]
        for p in range(n_pages):
            acc = acc + p_scr[h:h + 1, p * page:(p + 1) * page] * v_pages[p][g]
        accs.append(acc)
    _add_lane_sum_column(ot_ref, jnp.concatenate(accs, axis=0), b)


def _column_lanes(xt_ref, col, lanes):
    rows, n = xt_ref.shape
    at_lane0 = pltpu.roll(xt_ref[...], lax.rem(n - col, n), axis=1)
    return jnp.broadcast_to(at_lane0[:, :1], (rows, lanes))


def _add_lane_sum_column(ot_ref, acc, col):
    place = (lax.broadcasted_iota(jnp.int32, (acc.shape[1], ot_ref.shape[1]), 1) == col).astype(BF16)
    hi = acc.astype(BF16)
    lo = (acc - hi.astype(F32)).astype(BF16)
    ot_ref[...] += (jnp.dot(hi, place, preferred_element_type=F32)
                    + jnp.dot(lo, place, preferred_element_type=F32))


def _moba_decode(q_t, knew_t, vnew_t, cache_kt, cache_vt, page_table):
    n = q_t.shape[1]
    page = cache_kt.shape[3]
    n_pages = page_table.shape[0] // n
    past = n_pages * page
    hrows = -(-MOBA_HEADS // 8) * 8

    const2 = lambda b, pt: (0, 0)
    page_buf = pltpu.VMEM((2, n_pages, MOBA_KV_HEADS, HEAD_DIM, page), F32)
    grid_spec = pltpu.PrefetchScalarGridSpec(
        num_scalar_prefetch=1,
        grid=(n,),
        in_specs=[pl.BlockSpec((MIX_W, n), const2),
                  pl.BlockSpec((KV_W, n), const2),
                  pl.BlockSpec((KV_W, n), const2),
                  pl.BlockSpec(memory_space=pl.ANY),
                  pl.BlockSpec(memory_space=pl.ANY)],
        out_specs=pl.BlockSpec((MIX_W, n), const2),
        scratch_shapes=[page_buf, page_buf, pltpu.SemaphoreType.DMA((2, 2)),
                        pltpu.VMEM((hrows, past), F32), pltpu.VMEM((hrows, past), F32),
                        pltpu.VMEM((hrows, page), F32)],
    )
    return pl.pallas_call(
        functools.partial(_moba_decode_kernel, n_pages=n_pages, page=page),
        grid_spec=grid_spec,
        out_shape=jax.ShapeDtypeStruct((MIX_W, n), F32),
        compiler_params=_cparams("arbitrary"),
        name="moba_decode",
    )(page_table, q_t, knew_t, vnew_t, cache_kt, cache_vt)


def _seg_matrix(width):
    head = jnp.arange(width) // HEAD_DIM
    return (head[:, None] == head[None, :]).astype(BF16) / HEAD_DIM


def _scan_perm():
    r = jnp.arange(SCAN_T)
    t = (r % SCAN_SUB) * SCAN_STEPS + r // SCAN_SUB
    return (t[:, None] == jnp.arange(SCAN_T)[None, :]).astype(BF16)


def _block_diag(x, row_axes):
    eye = jnp.eye(x.shape[1], dtype=x.dtype)
    if row_axes == "gh":
        full = jnp.einsum("kghp,gG->kghGp", x, eye)
    else:
        full = jnp.einsum("kghp,gG->kgpGh", x, eye)
    k, g, a, _, b = full.shape
    return full.reshape(k, g * a, g * b)


def _s5_constants(p, l, ab_re, ab_im, bb_re, bb_im):
    gpb = SSM_KB // SSM_GROUP
    shape4 = (N_KB, gpb, SSM_GROUP, SSM_STATE)
    bmat = jnp.concatenate([_block_diag(bb_re[l].reshape(shape4), "gh"),
                            _block_diag(bb_im[l].reshape(shape4), "gh")], axis=-1).astype(BF16)
    c_re = p["ssm_c_re"][l].reshape(shape4)
    c_im = p["ssm_c_im"][l].reshape(shape4)
    cmat = jnp.concatenate([_block_diag(c_re, "gp"), -_block_diag(c_im, "gp")], axis=1).astype(BF16)
    a_re = ab_re[l, ::SSM_GROUP].reshape(1, N_STATE)
    a_im = ab_im[l, ::SSM_GROUP].reshape(1, N_STATE)
    d_row = p["ssm_d"][l].reshape(1, MIX_W)
    return bmat, a_re, a_im, cmat, d_row, p["w_glu"][l].astype(BF16)


def kernel(x_prompt, x_sample, mem_prompt, state_ssm_re, state_ssm_im, cache_k, cache_v, page_table,
           cache_mem_k, cache_mem_v, g_mix, w_in, w_out, g_ffn, w_gu, w_down, ssm_a_re, ssm_a_im,
           ssm_log_dt, ssm_b_re, ssm_b_im, ssm_c_re, ssm_c_im, ssm_d, w_glu, g_q, g_mq, g_mem,
           w_mem_kv, g_mk, g_kv, w_kv, g_k):
    bsz, seq, _ = x_prompt.shape
    n_p = bsz * seq
    n_s = x_sample.shape[0]
    tokens = mem_prompt.shape[1]
    p = dict(ssm_c_re=ssm_c_re, ssm_c_im=ssm_c_im, ssm_d=ssm_d, w_glu=w_glu)

    w_in_b = w_in.astype(BF16)
    w_out_b = w_out.astype(BF16)
    w_gu_b = w_gu.astype(BF16)
    w_down_b = w_down.astype(BF16)
    seg_kv = _seg_matrix(KV_W)
    seg_mix = _seg_matrix(MIX_W)
    gq_rows = jnp.tile(g_q, (1, MOBA_HEADS))
    gmq_rows = jnp.tile(g_mq, (1, MEM_HEADS))

    rep = lambda a: jnp.repeat(a, SSM_GROUP, axis=1)
    b_rows = lambda b: jnp.swapaxes(b, 2, 3).reshape(N_A_LAYERS, MIX_W, SSM_STATE)
    ab_re, ab_im, bb_re, bb_im = _ssm_prep(rep(ssm_a_re), rep(ssm_a_im), rep(ssm_log_dt)[..., None],
                                           b_rows(ssm_b_re), b_rows(ssm_b_im))
    s5 = [_s5_constants(p, l, ab_re, ab_im, bb_re, bb_im) for l in range(N_A_LAYERS)]
    perm = _scan_perm()
    perm_t = perm.T

    mem_flat = mem_prompt.reshape(bsz * tokens, D_MODEL)
    pm_k, pm_v = _kv_proj(mem_flat, g_mem[:, None, :], w_mem_kv.astype(BF16), seg_kv,
                          jnp.tile(g_mk, (1, MEM_HEADS))[:, None, :], tm=256)
    p_mem_k = pm_k.reshape(DEPTH, bsz, tokens, MEM_HEADS, HEAD_DIM)
    p_mem_v = pm_v.reshape(DEPTH, bsz, tokens, MEM_HEADS, HEAD_DIM)
    pm_k_b = pm_k.reshape(DEPTH, bsz, tokens, MEM_W).astype(BF16)
    pm_v_b = pm_v.reshape(DEPTH, bsz, tokens, MEM_W).astype(BF16)

    g_kv_rows = g_kv.reshape(1, 1, D_MODEL)
    w_kv_b = w_kv.astype(BF16)[None]
    g_k_rows = jnp.tile(g_k, MOBA_KV_HEADS).reshape(1, 1, KV_W)

    h = x_prompt.reshape(n_p, D_MODEL)
    p_fin_re, p_fin_im = [], []
    k_aug = v_t = kmean = p_k = p_v = None
    nblk = seq // MOBA_BLOCK
    g_mix_rows = g_mix[:, None, :]
    next_in = lambda l: (g_mix_rows, w_in_b) if l + 1 < DEPTH else None
    tail_weights = (w_out_b, g_ffn[:, None, :], w_gu_b, w_down_b)
    proj = _norm_matmul(h, g_mix_rows, w_in_b, 0, tm=512)
    for l in range(DEPTH):
        mem_args = (pm_k_b[l], pm_v_b[l], seg_kv, gmq_rows[l][None])
        if l < N_A_LAYERS:
            mix, mem, f_re, f_im = _s5_scan(proj, mem_args, bsz, seq, perm, perm_t, *s5[l])
            p_fin_re.append(f_re.reshape(bsz, SSM_GROUPS, SSM_STATE))
            p_fin_im.append(f_im.reshape(bsz, SSM_GROUPS, SSM_STATE))
        else:
            mix, mem = _moba_prefill(proj, mem_args, k_aug, v_t, kmean, seg_mix, gq_rows[l - N_A_LAYERS][None],
                                     bsz, seq)
        h, proj = _layer_tail(h, mix, mem, l, *tail_weights, tm=512, next_in=next_in(l))
        if l == N_A_LAYERS - 1:
            k_new, v_new, k_aug, v_t, kmean = _kv_proj(h, g_kv_rows, w_kv_b, seg_kv, g_k_rows, tm=MOBA_BLOCK,
                                                       blocks_per_seq=nblk)
            p_k = k_new.reshape(bsz, MOBA_KV_HEADS, HEAD_DIM, seq).transpose(0, 3, 1, 2)
            p_v = v_new.reshape(bsz, MOBA_KV_HEADS, HEAD_DIM, seq).transpose(0, 3, 1, 2)
            kmean = kmean.reshape(bsz, nblk, KV_W)
    y_prompt = h.reshape(bsz, seq, D_MODEL)

    ck_t = cache_k.transpose(0, 2, 3, 1)
    cv_t = cache_v.transpose(0, 2, 3, 1)
    cmk_t = cache_mem_k.transpose(0, 1, 3, 4, 2)
    cmv_t = cache_mem_v.transpose(0, 1, 3, 4, 2)

    h = x_sample.reshape(n_s, D_MODEL)
    s_fin_re, s_fin_im = [], []
    s_k = s_v = None
    proj = _norm_matmul(h, g_mix_rows, w_in_b, 0, tm=n_s)
    for l in range(DEPTH):
        if l < N_A_LAYERS:
            mix, x_re, x_im = _s5_step(proj, state_ssm_re[l].reshape(n_s, N_STATE),
                                       state_ssm_im[l].reshape(n_s, N_STATE), *s5[l])
            s_fin_re.append(x_re.reshape(n_s, SSM_GROUPS, SSM_STATE))
            s_fin_im.append(x_im.reshape(n_s, SSM_GROUPS, SSM_STATE))
            mq_t, = _decode_prep(proj, seg_kv, gmq_rows[l][None])
        else:
            mq_t, q_t, knew_t, vnew_t = _decode_prep(
                proj, seg_kv, gmq_rows[l][None],
                moba=(seg_mix, gq_rows[l - N_A_LAYERS][None], s_k.reshape(n_s, KV_W), s_v.reshape(n_s, KV_W)))
            mix = _moba_decode(q_t, knew_t, vnew_t, ck_t, cv_t, page_table.reshape(-1)).T.astype(BF16)
        mem = _mem_decode(mq_t, cmk_t, cmv_t, l).T.astype(BF16)
        h, proj = _layer_tail(h, mix, mem, l, *tail_weights, tm=n_s, next_in=next_in(l))
        if l == N_A_LAYERS - 1:
            s_k, s_v = _kv_proj(h, g_kv_rows, w_kv_b, seg_kv, g_k_rows, tm=n_s)
    y_sample = h.reshape(n_s, 1, D_MODEL)

    return (y_prompt, y_sample,
            jnp.stack(p_fin_re), jnp.stack(p_fin_im), p_k, p_v, p_mem_k, p_mem_v,
            jnp.stack(s_fin_re), jnp.stack(s_fin_im),
            s_k.reshape(n_s, 1, MOBA_KV_HEADS, HEAD_DIM), s_v.reshape(n_s, 1, MOBA_KV_HEADS, HEAD_DIM))
```

```python
import functools
import math

import jax
import jax.numpy as jnp
from jax import lax
from jax.experimental import pallas as pl
from jax.experimental.pallas import tpu as pltpu

F32 = jnp.float32
BF16 = jnp.bfloat16

D_MODEL = 1024
DEPTH = 4
N_A_LAYERS = DEPTH // 2
HEAD_DIM = 64
MIX_W = 3 * D_MODEL // 4
MEM_W = D_MODEL // 4
MEM_HEADS = MEM_W // HEAD_DIM
SSM_GROUP = 16
SSM_GROUPS = MIX_W // SSM_GROUP
SSM_STATE = 64
N_STATE = SSM_GROUPS * SSM_STATE
MOBA_HEADS = MIX_W // HEAD_DIM
MOBA_KV_HEADS = 4
Q_PER_KV = MOBA_HEADS // MOBA_KV_HEADS
KV_W = MOBA_KV_HEADS * HEAD_DIM
MOBA_BLOCK = 256
MOBA_TOPK = 3
D_FF = ((-(-8 * D_MODEL // 3) + 255) // 256) * 256
EPS = 1e-6
ATTN_SCALE = HEAD_DIM ** -0.5
MASK_VALUE = -1e30
KAUG_W = 128
VAUG_H = HEAD_DIM + 16
LOG2_E = math.log2(math.e)

SCAN_SUB = 8
SCAN_STEPS = 32
SCAN_T = SCAN_SUB * SCAN_STEPS
SSM_KB = 256
N_KB = MIX_W // SSM_KB
KB_STATES = (SSM_KB // SSM_GROUP) * SSM_STATE
SCAN_CW = 128

VMEM_LIMIT_BYTES = 56 * 1024 * 1024


def _cparams(*sem):
    return pltpu.CompilerParams(dimension_semantics=sem, vmem_limit_bytes=VMEM_LIMIT_BYTES)


def _rms(x, g_row):
    ms = jnp.mean(x * x, axis=-1, keepdims=True)
    return x * lax.rsqrt(ms + EPS) * g_row


def _seg_rms(x, seg_mat, g_row):
    sq = x * x
    hi = sq.astype(BF16)
    lo = (sq - hi.astype(F32)).astype(BF16)
    ms = (jnp.dot(hi, seg_mat, preferred_element_type=F32)
          + jnp.dot(lo, seg_mat, preferred_element_type=F32))
    return x * lax.rsqrt(ms + EPS) * g_row


def _dot_nt(a, b, precision=None):
    return lax.dot_general(a, b, (((1,), (1,)), ((), ())), precision=precision,
                           preferred_element_type=F32)


def _topk_keep(gate, n_valid, axis):
    n = gate.shape[axis]
    idx = lax.broadcasted_iota(jnp.int32, gate.shape, axis)
    rank = jnp.zeros(gate.shape, jnp.int32)
    for j in range(n):
        gj = lax.slice_in_dim(gate, j, j + 1, axis=axis)
        beats = (gj > gate) | ((gj == gate) & (j < idx))
        rank = rank + jnp.where(beats, jnp.where(j < n_valid, 1, 0), 0)
    return (rank < MOBA_TOPK) & (idx < n_valid)


def _norm_matmul_kernel(x_ref, g_ref, w_ref, o_ref):
    a = _rms(x_ref[...], g_ref[...]).astype(BF16)
    o_ref[...] = jnp.dot(a, w_ref[...], preferred_element_type=F32)


def _norm_matmul(x, g_rows, w, layer, tm):
    n, d = x.shape
    nout = w.shape[2]
    return pl.pallas_call(
        _norm_matmul_kernel,
        grid=(n // tm,),
        in_specs=[pl.BlockSpec((tm, d), lambda i: (i, 0)),
                  pl.BlockSpec((None, 1, d), lambda i: (layer, 0, 0)),
                  pl.BlockSpec((None, d, nout), lambda i: (layer, 0, 0))],
        out_specs=pl.BlockSpec((tm, nout), lambda i: (i, 0)),
        out_shape=jax.ShapeDtypeStruct((n, nout), F32),
        compiler_params=_cparams("parallel"),
        name="norm_matmul",
    )(x, g_rows, w)


def _kv_kernel(x_ref, g_ref, w_ref, seg_ref, gk_ref, k_ref, v_ref, *extra, half, blocks_per_seq):
    a = _rms(x_ref[...], g_ref[0]).astype(BF16)
    kv = jnp.dot(a, w_ref[0], preferred_element_type=F32)
    k = _seg_rms(kv[:, :half], seg_ref[...], gk_ref[0])
    v = kv[:, half:]
    if not blocks_per_seq:
        k_ref[0] = k
        v_ref[0] = v
    else:
        kaug_ref, vt_ref, kmean_ref = extra
        lane = lax.broadcasted_iota(jnp.int32, (k.shape[0], KAUG_W), 1)
        tail = jnp.where(lane == HEAD_DIM + pl.program_id(1) % blocks_per_seq, 1.0, 0.0)
        for g in range(MOBA_KV_HEADS):
            col = k[:, (g // 2) * KAUG_W:(g // 2 + 1) * KAUG_W]
            if g % 2:
                col = pltpu.roll(col, HEAD_DIM, axis=1)
            kaug_ref[g] = jnp.where(lane < HEAD_DIM, col, tail).astype(BF16)
        v_t = v.T
        k_ref[0] = k.T
        v_ref[0] = v_t
        ones_row = jnp.where(lax.broadcasted_iota(jnp.int32, (VAUG_H - HEAD_DIM, k.shape[0]), 0) == 0, 1.0, 0.0)
        vt_ref[...] = jnp.concatenate(
            [piece for g in range(MOBA_KV_HEADS) for piece in (v_t[g * HEAD_DIM:(g + 1) * HEAD_DIM], ones_row)],
            axis=0).astype(BF16)
        kmean_ref[0] = jnp.mean(k, axis=0, keepdims=True)


def _kv_proj(x, g_rows, w, seg_mat, gk_rows, tm, blocks_per_seq=0):
    n, d = x.shape
    layers, _, two_half = w.shape
    half = two_half // 2
    out_shape = [jax.ShapeDtypeStruct((layers, n, half), F32)] * 2
    out_specs = [pl.BlockSpec((1, tm, half), lambda l, i: (l, i, 0))] * 2
    if blocks_per_seq:
        assert layers == 1 and tm == MOBA_BLOCK and half == KV_W
        nseq = n // (tm * blocks_per_seq)
        out_shape = [jax.ShapeDtypeStruct((nseq, half, tm * blocks_per_seq), F32)] * 2
        out_specs = [pl.BlockSpec((1, half, tm), lambda l, i: (i // blocks_per_seq, 0, i % blocks_per_seq))] * 2
        out_shape += [jax.ShapeDtypeStruct((MOBA_KV_HEADS, n, KAUG_W), BF16),
                      jax.ShapeDtypeStruct((MOBA_KV_HEADS * VAUG_H, n), BF16),
                      jax.ShapeDtypeStruct((n // tm, 1, half), F32)]
        out_specs += [pl.BlockSpec((MOBA_KV_HEADS, tm, KAUG_W), lambda l, i: (0, i, 0)),
                      pl.BlockSpec((MOBA_KV_HEADS * VAUG_H, tm), lambda l, i: (0, i)),
                      pl.BlockSpec((1, 1, half), lambda l, i: (i, 0, 0))]
    return pl.pallas_call(
        functools.partial(_kv_kernel, half=half, blocks_per_seq=blocks_per_seq),
        grid=(layers, n // tm),
        in_specs=[pl.BlockSpec((tm, d), lambda l, i: (i, 0)),
                  pl.BlockSpec((1, 1, d), lambda l, i: (l, 0, 0)),
                  pl.BlockSpec((1, d, two_half), lambda l, i: (l, 0, 0)),
                  pl.BlockSpec((half, half), lambda l, i: (0, 0)),
                  pl.BlockSpec((1, 1, half), lambda l, i: (l, 0, 0))],
        out_specs=out_specs,
        out_shape=out_shape,
        compiler_params=_cparams("parallel", "parallel"),
        name="kv_proj",
    )(x, g_rows, w, seg_mat, gk_rows)


def _layer_tail_kernel(h_ref, mix_ref, mem_ref, wmix_ref, wmem_ref, g_ref, wgu_ref, wd_ref, *rest):
    h1 = (h_ref[...]
          + jnp.dot(mix_ref[...], wmix_ref[...], preferred_element_type=F32)
          + jnp.dot(mem_ref[...], wmem_ref[...], preferred_element_type=F32))
    a = _rms(h1, g_ref[...]).astype(BF16)
    gate = jnp.dot(a, wgu_ref[:, :D_FF], preferred_element_type=F32)
    up = jnp.dot(a, wgu_ref[:, D_FF:], preferred_element_type=F32)
    act = (gate * jax.nn.sigmoid(gate) * up).astype(BF16)
    h2 = h1 + jnp.dot(act, wd_ref[...], preferred_element_type=F32)
    if len(rest) == 1:
        rest[0][...] = h2
    else:
        gn_ref, win_ref, o_ref, proj_ref = rest
        o_ref[...] = h2
        proj_ref[...] = jnp.dot(_rms(h2, gn_ref[...]).astype(BF16), win_ref[...], preferred_element_type=F32)


def _layer_tail(h, mix, mem, layer, w_out, g_ffn, w_gu, w_down, tm, next_in=None):
    n, d = h.shape
    row = lambda w: pl.BlockSpec((tm, w), lambda i: (i, 0))

    def resident(a, l, rows=None, row_block=0):
        shape = (None, rows or a.shape[1], a.shape[2])
        return pl.BlockSpec(shape, lambda i: (l, row_block, 0), pipeline_mode=pl.Buffered(1))

    args = [h, mix, mem, w_out, w_out, g_ffn, w_gu, w_down]
    in_specs = [row(d), row(MIX_W), row(MEM_W),
                resident(w_out, layer, MIX_W), resident(w_out, layer, MEM_W, MIX_W // MEM_W),
                resident(g_ffn, layer), resident(w_gu, layer), resident(w_down, layer)]
    out_shape = [jax.ShapeDtypeStruct((n, d), F32)]
    if next_in is not None:
        args += list(next_in)
        in_specs += [resident(a, layer + 1) for a in next_in]
        out_shape.append(jax.ShapeDtypeStruct((n, next_in[1].shape[2]), F32))
    out = pl.pallas_call(
        _layer_tail_kernel,
        grid=(n // tm,),
        in_specs=in_specs,
        out_specs=[row(s.shape[1]) for s in out_shape],
        out_shape=out_shape,
        compiler_params=_cparams("parallel"),
        name="layer_tail",
    )(*args)
    return out if next_in is not None else (out[0], None)


def _ssm_prep_kernel(are_ref, aim_ref, ldt_ref, bre_ref, bim_ref, abr_ref, abi_ref, bbr_ref, bbi_ref):
    dt = jnp.exp(ldt_ref[0])
    lam_re = jnp.minimum(are_ref[0], -1e-4)
    lam_im = aim_ref[0]
    mag = jnp.exp(dt * lam_re)
    ang = dt * lam_im
    ab_re = mag * jnp.cos(ang)
    ab_im = mag * jnp.sin(ang)
    den = lam_re * lam_re + lam_im * lam_im
    num_re = ab_re - 1.0
    f_re = (num_re * lam_re + ab_im * lam_im) / den
    f_im = (ab_im * lam_re - num_re * lam_im) / den
    b_re = bre_ref[0]
    b_im = bim_ref[0]
    abr_ref[0] = ab_re
    abi_ref[0] = ab_im
    bbr_ref[0] = f_re * b_re - f_im * b_im
    bbi_ref[0] = f_re * b_im + f_im * b_re


def _ssm_prep(a_re, a_im, log_dt, b_re, b_im):
    layers = a_re.shape[0]
    big = pl.BlockSpec((1, MIX_W, SSM_STATE), lambda l: (l, 0, 0))
    return pl.pallas_call(
        _ssm_prep_kernel,
        grid=(layers,),
        in_specs=[big, big, pl.BlockSpec((1, MIX_W, 1), lambda l: (l, 0, 0)), big, big],
        out_specs=[big] * 4,
        out_shape=[jax.ShapeDtypeStruct((layers, MIX_W, SSM_STATE), F32)] * 4,
        compiler_params=_cparams("parallel"),
        name="ssm_prep",
    )(a_re, a_im, log_dt, b_re, b_im)


def _re_cols(n0):
    kb, off = divmod(n0, KB_STATES)
    return kb * 2 * KB_STATES + off


def _s5_output(y_state, u, d_ref, wglu_ref):
    y = jax.nn.gelu(y_state + d_ref[...] * u)
    z = jnp.dot(y.astype(BF16), wglu_ref[...], preferred_element_type=F32)
    return (y * jax.nn.sigmoid(z)).astype(BF16)


def _s5_scan_kernel(u_ref, perm_ref, permt_ref, bmat_ref, are_ref, aim_ref, cmat_ref, d_ref, wglu_ref,
                    mq_ref, mk_ref, mv_ref, mseg_ref, mg_ref,
                    y_ref, mem_ref, fre_ref, fim_ref, x_scr, apr_scr, api_scr, str_scr, sti_scr):
    c = pl.program_id(1)

    @pl.when(c == 0)
    def _():
        str_scr[...] = jnp.zeros_like(str_scr)
        sti_scr[...] = jnp.zeros_like(sti_scr)
        ar, ai = are_ref[...], aim_ref[...]
        pr, pi = ar, ai
        for i in range(SCAN_STEPS):
            apr_scr[i:i + 1, :] = pr
            api_scr[i:i + 1, :] = pi
            pr, pi = pr * ar - pi * ai, pr * ai + pi * ar

    mem_ref[...] = _mem_attend(mq_ref[...], mk_ref[0], mv_ref[0], mseg_ref[...], mg_ref[...])

    u = u_ref[...]
    up = jnp.dot(perm_ref[...], u.astype(BF16), preferred_element_type=F32).astype(BF16)
    sub = lax.broadcasted_iota(jnp.int32, (SCAN_SUB, SCAN_CW), 0)
    ys = []
    for kb in range(N_KB):
        x_scr[:, kb * 2 * KB_STATES:(kb + 1) * 2 * KB_STATES] = jnp.dot(
            up[:, kb * SSM_KB:(kb + 1) * SSM_KB], bmat_ref[kb], preferred_element_type=F32)
        for n0 in range(kb * KB_STATES, (kb + 1) * KB_STATES, SCAN_CW):
            rc = _re_cols(n0)
            ic = rc + KB_STATES
            ar = jnp.broadcast_to(are_ref[:, n0:n0 + SCAN_CW], (SCAN_SUB, SCAN_CW))
            ai = jnp.broadcast_to(aim_ref[:, n0:n0 + SCAN_CW], (SCAN_SUB, SCAN_CW))
            xr = jnp.zeros((SCAN_SUB, SCAN_CW), F32)
            xi = jnp.zeros((SCAN_SUB, SCAN_CW), F32)
            for i in range(SCAN_STEPS):
                r0 = i * SCAN_SUB
                xr, xi = (ar * xr - ai * xi + x_scr[r0:r0 + SCAN_SUB, rc:rc + SCAN_CW],
                          ar * xi + ai * xr + x_scr[r0:r0 + SCAN_SUB, ic:ic + SCAN_CW])
                x_scr[r0:r0 + SCAN_SUB, rc:rc + SCAN_CW] = xr
                x_scr[r0:r0 + SCAN_SUB, ic:ic + SCAN_CW] = xi

            a_t_r = apr_scr[SCAN_STEPS - 1:SCAN_STEPS, n0:n0 + SCAN_CW]
            a_t_i = api_scr[SCAN_STEPS - 1:SCAN_STEPS, n0:n0 + SCAN_CW]
            cr = str_scr[:, n0:n0 + SCAN_CW]
            ci = sti_scr[:, n0:n0 + SCAN_CW]
            car_r = jnp.zeros((SCAN_SUB, SCAN_CW), F32)
            car_i = jnp.zeros((SCAN_SUB, SCAN_CW), F32)
            for j in range(SCAN_SUB):
                car_r = jnp.where(sub == j, cr, car_r)
                car_i = jnp.where(sub == j, ci, car_i)
                cr, ci = (xr[j:j + 1] + a_t_r * cr - a_t_i * ci,
                          xi[j:j + 1] + a_t_r * ci + a_t_i * cr)
            str_scr[:, n0:n0 + SCAN_CW] = cr
            sti_scr[:, n0:n0 + SCAN_CW] = ci

            for i in range(SCAN_STEPS):
                r0 = i * SCAN_SUB
                pr = apr_scr[i:i + 1, n0:n0 + SCAN_CW]
                pi = api_scr[i:i + 1, n0:n0 + SCAN_CW]
                x_scr[r0:r0 + SCAN_SUB, rc:rc + SCAN_CW] = (x_scr[r0:r0 + SCAN_SUB, rc:rc + SCAN_CW]
                                                            + (pr * car_r - pi * car_i))
                x_scr[r0:r0 + SCAN_SUB, ic:ic + SCAN_CW] = (x_scr[r0:r0 + SCAN_SUB, ic:ic + SCAN_CW]
                                                            + (pr * car_i + pi * car_r))
        ys.append(jnp.dot(x_scr[:, kb * 2 * KB_STATES:(kb + 1) * 2 * KB_STATES].astype(BF16), cmat_ref[kb],
                          preferred_element_type=F32))
    yp = jnp.concatenate(ys, axis=1)
    hi = yp.astype(BF16)
    lo = (yp - hi.astype(F32)).astype(BF16)
    y_state = (jnp.dot(permt_ref[...], hi, preferred_element_type=F32)
               + jnp.dot(permt_ref[...], lo, preferred_element_type=F32))
    y_ref[...] = _s5_output(y_state, u, d_ref, wglu_ref)
    fre_ref[0] = str_scr[...]
    fim_ref[0] = sti_scr[...]


def _s5_scan(proj, mem_args, bsz, seq, perm, perm_t, bmat, a_re, a_im, cmat, d_row, w_glu):
    nchunk = seq // SCAN_T
    const2 = lambda b, c: (0, 0)
    const3 = lambda b, c: (0, 0, 0)
    state_spec = pl.BlockSpec((1, 1, N_STATE), lambda b, c: (b, 0, 0))
    return pl.pallas_call(
        _s5_scan_kernel,
        grid=(bsz, nchunk),
        in_specs=[pl.BlockSpec((SCAN_T, MIX_W), lambda b, c: (b * nchunk + c, 0)),
                  pl.BlockSpec((SCAN_T, SCAN_T), const2),
                  pl.BlockSpec((SCAN_T, SCAN_T), const2),
                  pl.BlockSpec((N_KB, SSM_KB, 2 * KB_STATES), const3),
                  pl.BlockSpec((1, N_STATE), const2),
                  pl.BlockSpec((1, N_STATE), const2),
                  pl.BlockSpec((N_KB, 2 * KB_STATES, SSM_KB), const3),
                  pl.BlockSpec((1, MIX_W), const2),
                  pl.BlockSpec((MIX_W, MIX_W), const2)] + _mem_specs(SCAN_T, nchunk, mem_args[0].shape[1]),
        out_specs=[pl.BlockSpec((SCAN_T, MIX_W), lambda b, c: (b * nchunk + c, 0)),
                   pl.BlockSpec((SCAN_T, MEM_W), lambda b, c: (b * nchunk + c, 0)), state_spec, state_spec],
        out_shape=[jax.ShapeDtypeStruct((bsz * seq, MIX_W), BF16),
                   jax.ShapeDtypeStruct((bsz * seq, MEM_W), BF16),
                   jax.ShapeDtypeStruct((bsz, 1, N_STATE), F32),
                   jax.ShapeDtypeStruct((bsz, 1, N_STATE), F32)],
        scratch_shapes=[pltpu.VMEM((SCAN_T, 2 * N_STATE), F32),
                        pltpu.VMEM((SCAN_STEPS, N_STATE), F32),
                        pltpu.VMEM((SCAN_STEPS, N_STATE), F32),
                        pltpu.VMEM((1, N_STATE), F32),
                        pltpu.VMEM((1, N_STATE), F32)],
        compiler_params=_cparams("parallel", "arbitrary"),
        name="s5_scan",
    )(proj, perm, perm_t, bmat, a_re, a_im, cmat, d_row, w_glu, proj, *mem_args)


def _s5_step_kernel(u_ref, hre_ref, him_ref, bmat_ref, are_ref, aim_ref, cmat_ref, d_ref, wglu_ref,
                    y_ref, xre_ref, xim_ref):
    u = u_ref[...]
    ub = u.astype(BF16)
    ys = []
    for kb in range(N_KB):
        bu = jnp.dot(ub[:, kb * SSM_KB:(kb + 1) * SSM_KB], bmat_ref[kb], preferred_element_type=F32)
        n0 = kb * KB_STATES
        ar, ai = are_ref[:, n0:n0 + KB_STATES], aim_ref[:, n0:n0 + KB_STATES]
        hr, hi = hre_ref[:, n0:n0 + KB_STATES], him_ref[:, n0:n0 + KB_STATES]
        xr = ar * hr - ai * hi + bu[:, :KB_STATES]
        xi = ar * hi + ai * hr + bu[:, KB_STATES:]
        xre_ref[:, n0:n0 + KB_STATES] = xr
        xim_ref[:, n0:n0 + KB_STATES] = xi
        x = jnp.concatenate([xr, xi], axis=1).astype(BF16)
        ys.append(jnp.dot(x, cmat_ref[kb], preferred_element_type=F32))
    y_ref[...] = _s5_output(jnp.concatenate(ys, axis=1), u, d_ref, wglu_ref)


def _s5_step(proj, h_re, h_im, bmat, a_re, a_im, cmat, d_row, w_glu):
    n = proj.shape[0]
    const2 = lambda i: (0, 0)
    const3 = lambda i: (0, 0, 0)
    return pl.pallas_call(
        _s5_step_kernel,
        grid=(1,),
        in_specs=[pl.BlockSpec((n, MIX_W), const2),
                  pl.BlockSpec((n, N_STATE), const2),
                  pl.BlockSpec((n, N_STATE), const2),
                  pl.BlockSpec((N_KB, SSM_KB, 2 * KB_STATES), const3),
                  pl.BlockSpec((1, N_STATE), const2),
                  pl.BlockSpec((1, N_STATE), const2),
                  pl.BlockSpec((N_KB, 2 * KB_STATES, SSM_KB), const3),
                  pl.BlockSpec((1, MIX_W), const2),
                  pl.BlockSpec((MIX_W, MIX_W), const2)],
        out_specs=[pl.BlockSpec((n, MIX_W), const2),
                   pl.BlockSpec((n, N_STATE), const2),
                   pl.BlockSpec((n, N_STATE), const2)],
        out_shape=[jax.ShapeDtypeStruct((n, MIX_W), BF16),
                   jax.ShapeDtypeStruct((n, N_STATE), F32),
                   jax.ShapeDtypeStruct((n, N_STATE), F32)],
        compiler_params=_cparams("arbitrary"),
        name="s5_step",
    )(proj, h_re, h_im, bmat, a_re, a_im, cmat, d_row, w_glu)


def _mem_attend(q, k, v, seg_mat, g_row):
    q = _seg_rms(q, seg_mat, g_row) * ATTN_SCALE
    lane_head = lax.broadcasted_iota(jnp.int32, q.shape, 1) // HEAD_DIM
    out = jnp.zeros(q.shape, F32)
    for h in range(MEM_HEADS):
        qh = jnp.where(lane_head == h, q, 0.0).astype(BF16)
        s = _dot_nt(qh, k)
        p = jnp.exp(s - jnp.max(s, axis=-1, keepdims=True))
        o = jnp.dot(p.astype(BF16), v, preferred_element_type=F32) / jnp.sum(p, axis=-1, keepdims=True)
        out = jnp.where(lane_head == h, o, out)
    return out.astype(BF16)


def _mem_specs(rows, blocks_per_seq, tokens):
    const2 = lambda b, i: (0, 0)
    kv_spec = pl.BlockSpec((1, tokens, MEM_W), lambda b, i: (b, 0, 0))
    return [pl.BlockSpec((rows, MEM_W), lambda b, i: (b * blocks_per_seq + i, MIX_W // MEM_W)),
            kv_spec, kv_spec, pl.BlockSpec((MEM_W, MEM_W), const2), pl.BlockSpec((1, MEM_W), const2)]


def _decode_prep_kernel(proj_ref, segk_ref, gmq_ref, *rest, with_moba):
    x = proj_ref[...]
    if with_moba:
        segm_ref, gq_ref, knew_ref, vnew_ref, mqt_ref, qt_ref, knt_ref, vnt_ref = rest
        qt_ref[...] = _seg_rms(x[:, :MIX_W], segm_ref[...], gq_ref[...]).T
        knt_ref[...] = knew_ref[...].T
        vnt_ref[...] = vnew_ref[...].T
    else:
        mqt_ref, = rest
    mqt_ref[...] = (_seg_rms(x[:, MIX_W:], segk_ref[...], gmq_ref[...]) * ATTN_SCALE).T


def _decode_prep(proj, seg_kv, gmq_row, moba=None):
    n, d = proj.shape
    const2 = lambda i: (0, 0)
    args = [proj, seg_kv, gmq_row]
    in_specs = [pl.BlockSpec((n, d), const2), pl.BlockSpec((MEM_W, MEM_W), const2), pl.BlockSpec((1, MEM_W), const2)]
    out_shape = [jax.ShapeDtypeStruct((MEM_W, n), F32)]
    if moba is not None:
        args += list(moba)
        in_specs += [pl.BlockSpec((MIX_W, MIX_W), const2), pl.BlockSpec((1, MIX_W), const2),
                     pl.BlockSpec((n, KV_W), const2), pl.BlockSpec((n, KV_W), const2)]
        out_shape += [jax.ShapeDtypeStruct((MIX_W, n), F32), jax.ShapeDtypeStruct((KV_W, n), F32),
                      jax.ShapeDtypeStruct((KV_W, n), F32)]
    return pl.pallas_call(
        functools.partial(_decode_prep_kernel, with_moba=moba is not None),
        grid=(1,),
        in_specs=in_specs,
        out_specs=[pl.BlockSpec(s.shape, const2) for s in out_shape],
        out_shape=out_shape,
        compiler_params=_cparams("arbitrary"),
        name="decode_prep",
    )(*args)


def _mem_decode_kernel(qt_ref, kt_ref, vt_ref, ot_ref, s_scr, p_scr, *, rows):
    i = pl.program_id(0)
    tokens = kt_ref.shape[-1]
    lanes = 128
    nchunk = tokens // lanes

    @pl.when(i == 0)
    def _():
        ot_ref[...] = jnp.zeros_like(ot_ref)

    for r in range(rows):
        qrep = _column_lanes(qt_ref, i * rows + r, lanes)
        for h in range(MEM_HEADS):
            qh = qrep[h * HEAD_DIM:(h + 1) * HEAD_DIM]
            for c in range(nchunk):
                s_scr[r * MEM_HEADS + h:r * MEM_HEADS + h + 1, c * lanes:(c + 1) * lanes] = jnp.sum(
                    qh * kt_ref[r, h, :, c * lanes:(c + 1) * lanes], axis=0, keepdims=True)
    s = s_scr[...]
    pr = jnp.exp(s - jnp.max(s, axis=1, keepdims=True))
    p_scr[...] = pr / jnp.sum(pr, axis=1, keepdims=True)
    for r in range(rows):
        accs = [p_scr[r * MEM_HEADS + h:r * MEM_HEADS + h + 1, :] * vt_ref[r, h] for h in range(MEM_HEADS)]
        _add_lane_sum_column(ot_ref, jnp.concatenate(accs, axis=0), i * rows + r)


def _mem_decode(mq_t, mem_kt, mem_vt, layer, rows=8):
    n = mq_t.shape[1]
    tokens = mem_kt.shape[-1]
    kv_spec = pl.BlockSpec((None, rows, MEM_HEADS, HEAD_DIM, tokens), lambda i: (layer, i, 0, 0, 0))
    return pl.pallas_call(
        functools.partial(_mem_decode_kernel, rows=rows),
        grid=(n // rows,),
        in_specs=[pl.BlockSpec((MEM_W, n), lambda i: (0, 0)), kv_spec, kv_spec],
        out_specs=pl.BlockSpec((MEM_W, n), lambda i: (0, 0)),
        out_shape=jax.ShapeDtypeStruct((MEM_W, n), F32),
        scratch_shapes=[pltpu.VMEM((rows * MEM_HEADS, tokens), F32), pltpu.VMEM((rows * MEM_HEADS, tokens), F32)],
        compiler_params=_cparams("arbitrary"),
        name="mem_decode",
    )(mq_t, mem_kt, mem_vt)


def _moba_prefill_kernel(q_ref, k_ref, vt_ref, kmean_ref, seg_ref, g_ref,
                         mq_ref, mk_ref, mv_ref, mseg_ref, mg_ref, o_ref, mem_ref,
                         qa_scr, s_scr, mcur_scr, alpha_scr, m_scr, acc_scr, *, nblk):
    mem_ref[...] = _mem_attend(mq_ref[...], mk_ref[0], mv_ref[0], mseg_ref[...], mg_ref[...])
    i = pl.program_id(1)
    blk = MOBA_BLOCK
    qn_t = _seg_rms(q_ref[...], seg_ref[...], g_ref[...]).T
    qs_t = (qn_t * (ATTN_SCALE * LOG2_E)).astype(BF16)
    kmean = kmean_ref[0]
    cols = Q_PER_KV * blk
    key_idx = lax.broadcasted_iota(jnp.int32, (blk, cols), 0)
    row_idx = lax.broadcasted_iota(jnp.int32, (blk, cols), 1) % blk
    causal = key_idx <= row_idx
    pad_t = jnp.zeros((KAUG_W - HEAD_DIM - nblk, blk), BF16)

    def start_of(n):
        return pl.multiple_of(jnp.where(n == 0, i, n - 1) * blk, blk)

    def score_phase(n, slot, g, diagonal=False):
        s = jnp.dot(k_ref[g, pl.ds(start_of(n), blk), :], qa_scr[g], preferred_element_type=F32)
        if diagonal:
            s = jnp.where(causal, s, MASK_VALUE)
        s_scr[slot, g] = s
        m_old = m_scr[g]
        m_new = jnp.maximum(m_old, jnp.max(s, axis=0, keepdims=True))
        alpha_scr[slot, g] = jnp.exp2(m_old - m_new)
        mcur_scr[slot, g] = m_new
        m_scr[g] = m_new

    def value_phase(n, slot, g):
        p = jnp.exp2(s_scr[slot, g] - mcur_scr[slot, g]).astype(BF16)
        acc_scr[g] = alpha_scr[slot, g] * acc_scr[g] + jnp.dot(
            vt_ref[g * VAUG_H:(g + 1) * VAUG_H, pl.ds(start_of(n), blk)], p,
            preferred_element_type=F32)

    m_scr[...] = jnp.full(m_scr.shape, MASK_VALUE, F32)
    acc_scr[...] = jnp.zeros_like(acc_scr)
    for g in range(MOBA_KV_HEADS):
        km = kmean[:, g * HEAD_DIM:(g + 1) * HEAD_DIM]
        parts = []
        for hh in range(Q_PER_KV):
            h = g * Q_PER_KV + hh
            gate_t = jnp.dot(km, qn_t[h * HEAD_DIM:(h + 1) * HEAD_DIM], precision=lax.Precision.HIGHEST,
                             preferred_element_type=F32)
            keep_t = _topk_keep(gate_t, i, axis=0)
            past_t = lax.broadcasted_iota(jnp.int32, gate_t.shape, 0) < i
            bias_t = jnp.where(past_t & ~keep_t, MASK_VALUE, 0.0).astype(BF16)
            parts.append(jnp.concatenate([qs_t[h * HEAD_DIM:(h + 1) * HEAD_DIM], bias_t, pad_t], axis=0))
        qa_scr[g] = jnp.concatenate(parts, axis=1)
        score_phase(0, 0, g, diagonal=True)

    def pair_step(t, _):
        for g in range(MOBA_KV_HEADS):
            value_phase(2 * t, 0, g)
            score_phase(2 * t + 1, 1, g)
        for g in range(MOBA_KV_HEADS):
            value_phase(2 * t + 1, 1, g)
            score_phase(2 * t + 2, 0, g)
        return 0

    lax.fori_loop(0, i // 2, pair_step, 0)

    @pl.when(i % 2 == 1)
    def _():
        for g in range(MOBA_KV_HEADS):
            value_phase(i - 1, 0, g)
            score_phase(i, 1, g)
        for g in range(MOBA_KV_HEADS):
            value_phase(i, 1, g)

    @pl.when(i % 2 == 0)
    def _():
        for g in range(MOBA_KV_HEADS):
            value_phase(i, 0, g)

    outs = []
    for g in range(MOBA_KV_HEADS):
        out_t = acc_scr[g, :HEAD_DIM] / acc_scr[g, HEAD_DIM:HEAD_DIM + 1]
        outs += [out_t[:, hh * blk:(hh + 1) * blk] for hh in range(Q_PER_KV)]
    o_ref[...] = jnp.concatenate(outs, axis=0).T.astype(BF16)


def _moba_prefill(proj, mem_args, k_aug, v_t, kmean, seg_mat, g_row, bsz, seq):
    nblk = seq // MOBA_BLOCK
    cols = Q_PER_KV * MOBA_BLOCK
    return pl.pallas_call(
        functools.partial(_moba_prefill_kernel, nblk=nblk),
        grid=(bsz, nblk),
        in_specs=[pl.BlockSpec((MOBA_BLOCK, MIX_W), lambda b, i: (b * nblk + i, 0)),
                  pl.BlockSpec((MOBA_KV_HEADS, seq, KAUG_W), lambda b, i: (0, b, 0)),
                  pl.BlockSpec((MOBA_KV_HEADS * VAUG_H, seq), lambda b, i: (0, b)),
                  pl.BlockSpec((1, nblk, KV_W), lambda b, i: (b, 0, 0)),
                  pl.BlockSpec((MIX_W, MIX_W), lambda b, i: (0, 0)),
                  pl.BlockSpec((1, MIX_W), lambda b, i: (0, 0))] + _mem_specs(MOBA_BLOCK, nblk, mem_args[0].shape[1]),
        out_specs=[pl.BlockSpec((MOBA_BLOCK, MIX_W), lambda b, i: (b * nblk + i, 0)),
                   pl.BlockSpec((MOBA_BLOCK, MEM_W), lambda b, i: (b * nblk + i, 0))],
        out_shape=[jax.ShapeDtypeStruct((bsz * seq, MIX_W), BF16),
                   jax.ShapeDtypeStruct((bsz * seq, MEM_W), BF16)],
        scratch_shapes=[pltpu.VMEM((MOBA_KV_HEADS, KAUG_W, cols), BF16),
                        pltpu.VMEM((2, MOBA_KV_HEADS, MOBA_BLOCK, cols), F32),
                        pltpu.VMEM((2, MOBA_KV_HEADS, 1, cols), F32),
                        pltpu.VMEM((2, MOBA_KV_HEADS, 1, cols), F32),
                        pltpu.VMEM((MOBA_KV_HEADS, 1, cols), F32),
                        pltpu.VMEM((MOBA_KV_HEADS, VAUG_H, cols), F32)],
        compiler_params=_cparams("parallel", "arbitrary"),
        name="moba_prefill",
    )(proj, k_aug, v_t, kmean, seg_mat, g_row, proj, *mem_args)


def _moba_decode_kernel(pt_ref, qt_ref, knt_ref, vnt_ref, ck_hbm, cv_hbm, ot_ref,
                        kbuf, vbuf, sem, s_scr, p_scr, own_scr, *, n_pages, page):
    b = pl.program_id(0)
    nblk = n_pages * page // MOBA_BLOCK
    slot = b % 2

    def page_copies(row, into):
        copies = []
        for p in range(n_pages):
            src = pt_ref[row * n_pages + p]
            copies.append(pltpu.make_async_copy(ck_hbm.at[src], kbuf.at[into, p], sem.at[0, into]))
            copies.append(pltpu.make_async_copy(cv_hbm.at[src], vbuf.at[into, p], sem.at[1, into]))
        return copies

    @pl.when(b == 0)
    def _():
        ot_ref[...] = jnp.zeros_like(ot_ref)
        for c in page_copies(0, 0):
            c.start()

    @pl.when(b + 1 < pl.num_programs(0))
    def _():
        for c in page_copies(b + 1, 1 - slot):
            c.start()

    for c in page_copies(b, slot):
        c.wait()
    k_pages = [kbuf.at[slot, p] for p in range(n_pages)]
    v_pages = [vbuf.at[slot, p] for p in range(n_pages)]

    qrep = _column_lanes(qt_ref, b, page)
    knrep = _column_lanes(knt_ref, b, page)
    vnrep = _column_lanes(vnt_ref, b, page)

    s_scr[MOBA_HEADS:, :] = jnp.zeros((s_scr.shape[0] - MOBA_HEADS, s_scr.shape[1]), F32)
    own_scr[MOBA_HEADS:, :] = jnp.zeros((own_scr.shape[0] - MOBA_HEADS, page), F32)
    for h in range(MOBA_HEADS):
        g = h // Q_PER_KV
        qh = qrep[h * HEAD_DIM:(h + 1) * HEAD_DIM]
        for p in range(n_pages):
            s_scr[h:h + 1, p * page:(p + 1) * page] = jnp.sum(qh * k_pages[p][g], axis=0, keepdims=True)
        own_scr[h:h + 1, :] = jnp.sum(qh * knrep[g * HEAD_DIM:(g + 1) * HEAD_DIM], axis=0, keepdims=True)
    s = s_scr[...]

    gates = [jnp.sum(s[:, j * MOBA_BLOCK:(j + 1) * MOBA_BLOCK], axis=1, keepdims=True) for j in range(nblk)]
    parts = []
    for j in range(nblk):
        rank = jnp.zeros(gates[j].shape, jnp.int32)
        for jj in range(nblk):
            if jj != j:
                beats = (gates[jj] >= gates[j]) if jj < j else (gates[jj] > gates[j])
                rank = rank + jnp.where(beats, 1, 0)
        bias = jnp.where(rank < MOBA_TOPK, 0.0, MASK_VALUE)
        parts.append(s[:, j * MOBA_BLOCK:(j + 1) * MOBA_BLOCK] * ATTN_SCALE + bias)
    s = jnp.concatenate(parts, axis=1)
    s_own = own_scr[:, :1] * ATTN_SCALE
    m = jnp.maximum(jnp.max(s, axis=1, keepdims=True), s_own)
    pr = jnp.exp(s - m)
    p_own = jnp.exp(s_own - m)
    inv = 1.0 / (jnp.sum(pr, axis=1, keepdims=True) + p_own)
    p_scr[...] = pr * inv
    own_scr[...] = jnp.broadcast_to(p_own * inv * (1.0 / page), own_scr.shape)

    accs = []
    for h in range(MOBA_HEADS):
        g = h // Q_PER_KV
        acc = own_scr[h:h + 1, :] * vnrep[g * HEAD_DIM:(g + 1) * HEAD_DIM]
        for p in range(n_pages):
            acc = acc + p_scr[h:h + 1, p * page:(p + 1) * page] * v_pages[p][g]
        accs.append(acc)
    _add_lane_sum_column(ot_ref, jnp.concatenate(accs, axis=0), b)


def _column_lanes(xt_ref, col, lanes):
    rows, n = xt_ref.shape
    at_lane0 = pltpu.roll(xt_ref[...], lax.rem(n - col, n), axis=1)
    return jnp.broadcast_to(at_lane0[:, :1], (rows, lanes))


def _add_lane_sum_column(ot_ref, acc, col):
    place = (lax.broadcasted_iota(jnp.int32, (acc.shape[1], ot_ref.shape[1]), 1) == col).astype(BF16)
    hi = acc.astype(BF16)
    lo = (acc - hi.astype(F32)).astype(BF16)
    ot_ref[...] += (jnp.dot(hi, place, preferred_element_type=F32)
                    + jnp.dot(lo, place, preferred_element_type=F32))


def _moba_decode(q_t, knew_t, vnew_t, cache_kt, cache_vt, page_table):
    n = q_t.shape[1]
    page = cache_kt.shape[3]
    n_pages = page_table.shape[0] // n
    past = n_pages * page
    hrows = -(-MOBA_HEADS // 8) * 8

    const2 = lambda b, pt: (0, 0)
    page_buf = pltpu.VMEM((2, n_pages, MOBA_KV_HEADS, HEAD_DIM, page), F32)
    grid_spec = pltpu.PrefetchScalarGridSpec(
        num_scalar_prefetch=1,
        grid=(n,),
        in_specs=[pl.BlockSpec((MIX_W, n), const2),
                  pl.BlockSpec((KV_W, n), const2),
                  pl.BlockSpec((KV_W, n), const2),
                  pl.BlockSpec(memory_space=pl.ANY),
                  pl.BlockSpec(memory_space=pl.ANY)],
        out_specs=pl.BlockSpec((MIX_W, n), const2),
        scratch_shapes=[page_buf, page_buf, pltpu.SemaphoreType.DMA((2, 2)),
                        pltpu.VMEM((hrows, past), F32), pltpu.VMEM((hrows, past), F32),
                        pltpu.VMEM((hrows, page), F32)],
    )
    return pl.pallas_call(
        functools.partial(_moba_decode_kernel, n_pages=n_pages, page=page),
        grid_spec=grid_spec,
        out_shape=jax.ShapeDtypeStruct((MIX_W, n), F32),
        compiler_params=_cparams("arbitrary"),
        name="moba_decode",
    )(page_table, q_t, knew_t, vnew_t, cache_kt, cache_vt)


def _seg_matrix(width):
    head = jnp.arange(width) // HEAD_DIM
    return (head[:, None] == head[None, :]).astype(BF16) / HEAD_DIM


def _scan_perm():
    r = jnp.arange(SCAN_T)
    t = (r % SCAN_SUB) * SCAN_STEPS + r // SCAN_SUB
    return (t[:, None] == jnp.arange(SCAN_T)[None, :]).astype(BF16)


def _block_diag(x, row_axes):
    eye = jnp.eye(x.shape[1], dtype=x.dtype)
    if row_axes == "gh":
        full = jnp.einsum("kghp,gG->kghGp", x, eye)
    else:
        full = jnp.einsum("kghp,gG->kgpGh", x, eye)
    k, g, a, _, b = full.shape
    return full.reshape(k, g * a, g * b)


def _s5_constants(p, l, ab_re, ab_im, bb_re, bb_im):
    gpb = SSM_KB // SSM_GROUP
    shape4 = (N_KB, gpb, SSM_GROUP, SSM_STATE)
    bmat = jnp.concatenate([_block_diag(bb_re[l].reshape(shape4), "gh"),
                            _block_diag(bb_im[l].reshape(shape4), "gh")], axis=-1).astype(BF16)
    c_re = p["ssm_c_re"][l].reshape(shape4)
    c_im = p["ssm_c_im"][l].reshape(shape4)
    cmat = jnp.concatenate([_block_diag(c_re, "gp"), -_block_diag(c_im, "gp")], axis=1).astype(BF16)
    a_re = ab_re[l, ::SSM_GROUP].reshape(1, N_STATE)
    a_im = ab_im[l, ::SSM_GROUP].reshape(1, N_STATE)
    d_row = p["ssm_d"][l].reshape(1, MIX_W)
    return bmat, a_re, a_im, cmat, d_row, p["w_glu"][l].astype(BF16)


def kernel(x_prompt, x_sample, mem_prompt, state_ssm_re, state_ssm_im, cache_k, cache_v, page_table,
           cache_mem_k, cache_mem_v, g_mix, w_in, w_out, g_ffn, w_gu, w_down, ssm_a_re, ssm_a_im,
           ssm_log_dt, ssm_b_re, ssm_b_im, ssm_c_re, ssm_c_im, ssm_d, w_glu, g_q, g_mq, g_mem,
           w_mem_kv, g_mk, g_kv, w_kv, g_k):
    bsz, seq, _ = x_prompt.shape
    n_p = bsz * seq
    n_s = x_sample.shape[0]
    tokens = mem_prompt.shape[1]
    p = dict(ssm_c_re=ssm_c_re, ssm_c_im=ssm_c_im, ssm_d=ssm_d, w_glu=w_glu)

    w_in_b = w_in.astype(BF16)
    w_out_b = w_out.astype(BF16)
    w_gu_b = w_gu.astype(BF16)
    w_down_b = w_down.astype(BF16)
    seg_kv = _seg_matrix(KV_W)
    seg_mix = _seg_matrix(MIX_W)
    gq_rows = jnp.tile(g_q, (1, MOBA_HEADS))
    gmq_rows = jnp.tile(g_mq, (1, MEM_HEADS))

    rep = lambda a: jnp.repeat(a, SSM_GROUP, axis=1)
    b_rows = lambda b: jnp.swapaxes(b, 2, 3).reshape(N_A_LAYERS, MIX_W, SSM_STATE)
    ab_re, ab_im, bb_re, bb_im = _ssm_prep(rep(ssm_a_re), rep(ssm_a_im), rep(ssm_log_dt)[..., None],
                                           b_rows(ssm_b_re), b_rows(ssm_b_im))
    s5 = [_s5_constants(p, l, ab_re, ab_im, bb_re, bb_im) for l in range(N_A_LAYERS)]
    perm = _scan_perm()
    perm_t = perm.T

    mem_flat = mem_prompt.reshape(bsz * tokens, D_MODEL)
    pm_k, pm_v = _kv_proj(mem_flat, g_mem[:, None, :], w_mem_kv.astype(BF16), seg_kv,
                          jnp.tile(g_mk, (1, MEM_HEADS))[:, None, :], tm=256)
    p_mem_k = pm_k.reshape(DEPTH, bsz, tokens, MEM_HEADS, HEAD_DIM)
    p_mem_v = pm_v.reshape(DEPTH, bsz, tokens, MEM_HEADS, HEAD_DIM)
    pm_k_b = pm_k.reshape(DEPTH, bsz, tokens, MEM_W).astype(BF16)
    pm_v_b = pm_v.reshape(DEPTH, bsz, tokens, MEM_W).astype(BF16)

    g_kv_rows = g_kv.reshape(1, 1, D_MODEL)
    w_kv_b = w_kv.astype(BF16)[None]
    g_k_rows = jnp.tile(g_k, MOBA_KV_HEADS).reshape(1, 1, KV_W)

    h = x_prompt.reshape(n_p, D_MODEL)
    p_fin_re, p_fin_im = [], []
    k_aug = v_t = kmean = p_k = p_v = None
    nblk = seq // MOBA_BLOCK
    g_mix_rows = g_mix[:, None, :]
    next_in = lambda l: (g_mix_rows, w_in_b) if l + 1 < DEPTH else None
    tail_weights = (w_out_b, g_ffn[:, None, :], w_gu_b, w_down_b)
    proj = _norm_matmul(h, g_mix_rows, w_in_b, 0, tm=512)
    for l in range(DEPTH):
        mem_args = (pm_k_b[l], pm_v_b[l], seg_kv, gmq_rows[l][None])
        if l < N_A_LAYERS:
            mix, mem, f_re, f_im = _s5_scan(proj, mem_args, bsz, seq, perm, perm_t, *s5[l])
            p_fin_re.append(f_re.reshape(bsz, SSM_GROUPS, SSM_STATE))
            p_fin_im.append(f_im.reshape(bsz, SSM_GROUPS, SSM_STATE))
        else:
            mix, mem = _moba_prefill(proj, mem_args, k_aug, v_t, kmean, seg_mix, gq_rows[l - N_A_LAYERS][None],
                                     bsz, seq)
        h, proj = _layer_tail(h, mix, mem, l, *tail_weights, tm=512, next_in=next_in(l))
        if l == N_A_LAYERS - 1:
            k_new, v_new, k_aug, v_t, kmean = _kv_proj(h, g_kv_rows, w_kv_b, seg_kv, g_k_rows, tm=MOBA_BLOCK,
                                                       blocks_per_seq=nblk)
            p_k = k_new.reshape(bsz, MOBA_KV_HEADS, HEAD_DIM, seq).transpose(0, 3, 1, 2)
            p_v = v_new.reshape(bsz, MOBA_KV_HEADS, HEAD_DIM, seq).transpose(0, 3, 1, 2)
            kmean = kmean.reshape(bsz, nblk, KV_W)
    y_prompt = h.reshape(bsz, seq, D_MODEL)

    ck_t = cache_k.transpose(0, 2, 3, 1)
    cv_t = cache_v.transpose(0, 2, 3, 1)
    cmk_t = cache_mem_k.transpose(0, 1, 3, 4, 2)
    cmv_t = cache_mem_v.transpose(0, 1, 3, 4, 2)

    h = x_sample.reshape(n_s, D_MODEL)
    s_fin_re, s_fin_im = [], []
    s_k = s_v = None
    proj = _norm_matmul(h, g_mix_rows, w_in_b, 0, tm=n_s)
    for l in range(DEPTH):
        if l < N_A_LAYERS:
            mix, x_re, x_im = _s5_step(proj, state_ssm_re[l].reshape(n_s, N_STATE),
                                       state_ssm_im[l].reshape(n_s, N_STATE), *s5[l])
            s_fin_re.append(x_re.reshape(n_s, SSM_GROUPS, SSM_STATE))
            s_fin_im.append(x_im.reshape(n_s, SSM_GROUPS, SSM_STATE))
            mq_t, = _decode_prep(proj, seg_kv, gmq_rows[l][None])
        else:
            mq_t, q_t, knew_t, vnew_t = _decode_prep(
                proj, seg_kv, gmq_rows[l][None],
                moba=(seg_mix, gq_rows[l - N_A_LAYERS][None], s_k.reshape(n_s, KV_W), s_v.reshape(n_s, KV_W)))
            mix = _moba_decode(q_t, knew_t, vnew_t, ck_t, cv_t, page_table.reshape(-1)).T.astype(BF16)
        mem = _mem_decode(mq_t, cmk_t, cmv_t, l).T.astype(BF16)
        h, proj = _layer_tail(h, mix, mem, l, *tail_weights, tm=n_s, next_in=next_in(l))
        if l == N_A_LAYERS - 1:
            s_k, s_v = _kv_proj(h, g_kv_rows, w_kv_b, seg_kv, g_k_rows, tm=n_s)
    y_sample = h.reshape(n_s, 1, D_MODEL)

    return (y_prompt, y_sample,
            jnp.stack(p_fin_re), jnp.stack(p_fin_im), p_k, p_v, p_mem_k, p_mem_v,
            jnp.stack(s_fin_re), jnp.stack(s_fin_im),
            s_k.reshape(n_s, 1, MOBA_KV_HEADS, HEAD_DIM), s_v.reshape(n_s, 1, MOBA_KV_HEADS, HEAD_DIM))
```

```python
import functools
import math

import jax
import jax.numpy as jnp
from jax import lax
from jax.experimental import pallas as pl
from jax.experimental.pallas import tpu as pltpu

F32 = jnp.float32
BF16 = jnp.bfloat16

D_MODEL = 1024
DEPTH = 4
N_A_LAYERS = DEPTH // 2
HEAD_DIM = 64
MIX_W = 3 * D_MODEL // 4
MEM_W = D_MODEL // 4
MEM_HEADS = MEM_W // HEAD_DIM
SSM_GROUP = 16
SSM_GROUPS = MIX_W // SSM_GROUP
SSM_STATE = 64
N_STATE = SSM_GROUPS * SSM_STATE
MOBA_HEADS = MIX_W // HEAD_DIM
MOBA_KV_HEADS = 4
Q_PER_KV = MOBA_HEADS // MOBA_KV_HEADS
KV_W = MOBA_KV_HEADS * HEAD_DIM
MOBA_BLOCK = 256
MOBA_TOPK = 3
D_FF = ((-(-8 * D_MODEL // 3) + 255) // 256) * 256
EPS = 1e-6
ATTN_SCALE = HEAD_DIM ** -0.5
MASK_VALUE = -1e30
KAUG_W = 128
VAUG_H = HEAD_DIM + 16
LOG2_E = math.log2(math.e)

SCAN_SUB = 8
SCAN_STEPS = 64
SCAN_T = SCAN_SUB * SCAN_STEPS
SSM_KB = 256
N_KB = MIX_W // SSM_KB
KB_STATES = (SSM_KB // SSM_GROUP) * SSM_STATE
SCAN_CW = 128

VMEM_LIMIT_BYTES = 56 * 1024 * 1024


def _cparams(*sem):
    return pltpu.CompilerParams(dimension_semantics=sem, vmem_limit_bytes=VMEM_LIMIT_BYTES)


def _rms(x, g_row):
    ms = jnp.mean(x * x, axis=-1, keepdims=True)
    return x * lax.rsqrt(ms + EPS) * g_row


def _seg_rms(x, seg_mat, g_row):
    sq = x * x
    hi = sq.astype(BF16)
    lo = (sq - hi.astype(F32)).astype(BF16)
    ms = (jnp.dot(hi, seg_mat, preferred_element_type=F32)
          + jnp.dot(lo, seg_mat, preferred_element_type=F32))
    return x * lax.rsqrt(ms + EPS) * g_row


def _dot_nt(a, b, precision=None):
    return lax.dot_general(a, b, (((1,), (1,)), ((), ())), precision=precision,
                           preferred_element_type=F32)


def _topk_keep(gate, n_valid, axis):
    n = gate.shape[axis]
    idx = lax.broadcasted_iota(jnp.int32, gate.shape, axis)
    rank = jnp.zeros(gate.shape, jnp.int32)
    for j in range(n):
        gj = lax.slice_in_dim(gate, j, j + 1, axis=axis)
        beats = (gj > gate) | ((gj == gate) & (j < idx))
        rank = rank + jnp.where(beats, jnp.where(j < n_valid, 1, 0), 0)
    return (rank < MOBA_TOPK) & (idx < n_valid)


def _norm_matmul_kernel(x_ref, g_ref, w_ref, o_ref):
    a = _rms(x_ref[...], g_ref[...]).astype(BF16)
    o_ref[...] = jnp.dot(a, w_ref[...], preferred_element_type=F32)


def _norm_matmul(x, g_rows, w, layer, tm):
    n, d = x.shape
    nout = w.shape[2]
    return pl.pallas_call(
        _norm_matmul_kernel,
        grid=(n // tm,),
        in_specs=[pl.BlockSpec((tm, d), lambda i: (i, 0)),
                  pl.BlockSpec((None, 1, d), lambda i: (layer, 0, 0)),
                  pl.BlockSpec((None, d, nout), lambda i: (layer, 0, 0))],
        out_specs=pl.BlockSpec((tm, nout), lambda i: (i, 0)),
        out_shape=jax.ShapeDtypeStruct((n, nout), F32),
        compiler_params=_cparams("parallel"),
        name="norm_matmul",
    )(x, g_rows, w)


def _kv_kernel(x_ref, g_ref, w_ref, seg_ref, gk_ref, k_ref, v_ref, *extra, half, blocks_per_seq):
    a = _rms(x_ref[...], g_ref[0]).astype(BF16)
    kv = jnp.dot(a, w_ref[0], preferred_element_type=F32)
    k = _seg_rms(kv[:, :half], seg_ref[...], gk_ref[0])
    v = kv[:, half:]
    if not blocks_per_seq:
        k_ref[0] = k
        v_ref[0] = v
    else:
        kaug_ref, vt_ref, kmean_ref = extra
        lane = lax.broadcasted_iota(jnp.int32, (k.shape[0], KAUG_W), 1)
        tail = jnp.where(lane == HEAD_DIM + pl.program_id(1) % blocks_per_seq, 1.0, 0.0)
        for g in range(MOBA_KV_HEADS):
            col = k[:, (g // 2) * KAUG_W:(g // 2 + 1) * KAUG_W]
            if g % 2:
                col = pltpu.roll(col, HEAD_DIM, axis=1)
            kaug_ref[g] = jnp.where(lane < HEAD_DIM, col, tail).astype(BF16)
        v_t = v.T
        k_ref[0] = k.T
        v_ref[0] = v_t
        ones_row = jnp.where(lax.broadcasted_iota(jnp.int32, (VAUG_H - HEAD_DIM, k.shape[0]), 0) == 0, 1.0, 0.0)
        vt_ref[...] = jnp.concatenate(
            [piece for g in range(MOBA_KV_HEADS) for piece in (v_t[g * HEAD_DIM:(g + 1) * HEAD_DIM], ones_row)],
            axis=0).astype(BF16)
        kmean_ref[0] = jnp.mean(k, axis=0, keepdims=True)


def _kv_proj(x, g_rows, w, seg_mat, gk_rows, tm, blocks_per_seq=0):
    n, d = x.shape
    layers, _, two_half = w.shape
    half = two_half // 2
    out_shape = [jax.ShapeDtypeStruct((layers, n, half), F32)] * 2
    out_specs = [pl.BlockSpec((1, tm, half), lambda l, i: (l, i, 0))] * 2
    if blocks_per_seq:
        assert layers == 1 and tm == MOBA_BLOCK and half == KV_W
        nseq = n // (tm * blocks_per_seq)
        out_shape = [jax.ShapeDtypeStruct((nseq, half, tm * blocks_per_seq), F32)] * 2
        out_specs = [pl.BlockSpec((1, half, tm), lambda l, i: (i // blocks_per_seq, 0, i % blocks_per_seq))] * 2
        out_shape += [jax.ShapeDtypeStruct((MOBA_KV_HEADS, n, KAUG_W), BF16),
                      jax.ShapeDtypeStruct((MOBA_KV_HEADS * VAUG_H, n), BF16),
                      jax.ShapeDtypeStruct((n // tm, 1, half), F32)]
        out_specs += [pl.BlockSpec((MOBA_KV_HEADS, tm, KAUG_W), lambda l, i: (0, i, 0)),
                      pl.BlockSpec((MOBA_KV_HEADS * VAUG_H, tm), lambda l, i: (0, i)),
                      pl.BlockSpec((1, 1, half), lambda l, i: (i, 0, 0))]
    return pl.pallas_call(
        functools.partial(_kv_kernel, half=half, blocks_per_seq=blocks_per_seq),
        grid=(layers, n // tm),
        in_specs=[pl.BlockSpec((tm, d), lambda l, i: (i, 0)),
                  pl.BlockSpec((1, 1, d), lambda l, i: (l, 0, 0)),
                  pl.BlockSpec((1, d, two_half), lambda l, i: (l, 0, 0)),
                  pl.BlockSpec((half, half), lambda l, i: (0, 0)),
                  pl.BlockSpec((1, 1, half), lambda l, i: (l, 0, 0))],
        out_specs=out_specs,
        out_shape=out_shape,
        compiler_params=_cparams("parallel", "parallel"),
        name="kv_proj",
    )(x, g_rows, w, seg_mat, gk_rows)


def _layer_tail_kernel(h_ref, mix_ref, mem_ref, wmix_ref, wmem_ref, g_ref, wgu_ref, wd_ref, *rest):
    h1 = (h_ref[...]
          + jnp.dot(mix_ref[...], wmix_ref[...], preferred_element_type=F32)
          + jnp.dot(mem_ref[...], wmem_ref[...], preferred_element_type=F32))
    a = _rms(h1, g_ref[...]).astype(BF16)
    gate = jnp.dot(a, wgu_ref[:, :D_FF], preferred_element_type=F32)
    up = jnp.dot(a, wgu_ref[:, D_FF:], preferred_element_type=F32)
    act = (gate * jax.nn.sigmoid(gate) * up).astype(BF16)
    h2 = h1 + jnp.dot(act, wd_ref[...], preferred_element_type=F32)
    if len(rest) == 1:
        rest[0][...] = h2
    else:
        gn_ref, win_ref, o_ref, proj_ref = rest
        o_ref[...] = h2
        proj_ref[...] = jnp.dot(_rms(h2, gn_ref[...]).astype(BF16), win_ref[...], preferred_element_type=F32)


def _layer_tail(h, mix, mem, layer, w_out, g_ffn, w_gu, w_down, tm, next_in=None):
    n, d = h.shape
    row = lambda w: pl.BlockSpec((tm, w), lambda i: (i, 0))

    def resident(a, l, rows=None, row_block=0):
        shape = (None, rows or a.shape[1], a.shape[2])
        return pl.BlockSpec(shape, lambda i: (l, row_block, 0), pipeline_mode=pl.Buffered(1))

    args = [h, mix, mem, w_out, w_out, g_ffn, w_gu, w_down]
    in_specs = [row(d), row(MIX_W), row(MEM_W),
                resident(w_out, layer, MIX_W), resident(w_out, layer, MEM_W, MIX_W // MEM_W),
                resident(g_ffn, layer), resident(w_gu, layer), resident(w_down, layer)]
    out_shape = [jax.ShapeDtypeStruct((n, d), F32)]
    if next_in is not None:
        args += list(next_in)
        in_specs += [resident(a, layer + 1) for a in next_in]
        out_shape.append(jax.ShapeDtypeStruct((n, next_in[1].shape[2]), F32))
    out = pl.pallas_call(
        _layer_tail_kernel,
        grid=(n // tm,),
        in_specs=in_specs,
        out_specs=[row(s.shape[1]) for s in out_shape],
        out_shape=out_shape,
        compiler_params=_cparams("parallel"),
        name="layer_tail",
    )(*args)
    return out if next_in is not None else (out[0], None)


def _ssm_prep_kernel(are_ref, aim_ref, ldt_ref, bre_ref, bim_ref, abr_ref, abi_ref, bbr_ref, bbi_ref):
    dt = jnp.exp(ldt_ref[0])
    lam_re = jnp.minimum(are_ref[0], -1e-4)
    lam_im = aim_ref[0]
    mag = jnp.exp(dt * lam_re)
    ang = dt * lam_im
    ab_re = mag * jnp.cos(ang)
    ab_im = mag * jnp.sin(ang)
    den = lam_re * lam_re + lam_im * lam_im
    num_re = ab_re - 1.0
    f_re = (num_re * lam_re + ab_im * lam_im) / den
    f_im = (ab_im * lam_re - num_re * lam_im) / den
    b_re = bre_ref[0]
    b_im = bim_ref[0]
    abr_ref[0] = ab_re
    abi_ref[0] = ab_im
    bbr_ref[0] = f_re * b_re - f_im * b_im
    bbi_ref[0] = f_re * b_im + f_im * b_re


def _ssm_prep(a_re, a_im, log_dt, b_re, b_im):
    layers = a_re.shape[0]
    big = pl.BlockSpec((1, MIX_W, SSM_STATE), lambda l: (l, 0, 0))
    return pl.pallas_call(
        _ssm_prep_kernel,
        grid=(layers,),
        in_specs=[big, big, pl.BlockSpec((1, MIX_W, 1), lambda l: (l, 0, 0)), big, big],
        out_specs=[big] * 4,
        out_shape=[jax.ShapeDtypeStruct((layers, MIX_W, SSM_STATE), F32)] * 4,
        compiler_params=_cparams("parallel"),
        name="ssm_prep",
    )(a_re, a_im, log_dt, b_re, b_im)


def _re_cols(n0):
    kb, off = divmod(n0, KB_STATES)
    return kb * 2 * KB_STATES + off


def _s5_output(y_state, u, d_ref, wglu_ref):
    y = jax.nn.gelu(y_state + d_ref[...] * u)
    z = jnp.dot(y.astype(BF16), wglu_ref[...], preferred_element_type=F32)
    return (y * jax.nn.sigmoid(z)).astype(BF16)


def _s5_scan_kernel(u_ref, perm_ref, permt_ref, bmat_ref, are_ref, aim_ref, cmat_ref, d_ref, wglu_ref,
                    mq_ref, mk_ref, mv_ref, mseg_ref, mg_ref,
                    y_ref, mem_ref, fre_ref, fim_ref, x_scr, apr_scr, api_scr, str_scr, sti_scr):
    c = pl.program_id(1)

    @pl.when(c == 0)
    def _():
        str_scr[...] = jnp.zeros_like(str_scr)
        sti_scr[...] = jnp.zeros_like(sti_scr)
        ar, ai = are_ref[...], aim_ref[...]
        pr, pi = ar, ai
        for i in range(SCAN_STEPS):
            apr_scr[i:i + 1, :] = pr
            api_scr[i:i + 1, :] = pi
            pr, pi = pr * ar - pi * ai, pr * ai + pi * ar

    mem_ref[...] = _mem_attend(mq_ref[...], mk_ref[0], mv_ref[0], mseg_ref[...], mg_ref[...])

    u = u_ref[...]
    up = jnp.dot(perm_ref[...], u.astype(BF16), preferred_element_type=F32).astype(BF16)
    sub = lax.broadcasted_iota(jnp.int32, (SCAN_SUB, SCAN_CW), 0)
    ys = []
    for kb in range(N_KB):
        x_scr[:, kb * 2 * KB_STATES:(kb + 1) * 2 * KB_STATES] = jnp.dot(
            up[:, kb * SSM_KB:(kb + 1) * SSM_KB], bmat_ref[kb], preferred_element_type=F32)
        for n0 in range(kb * KB_STATES, (kb + 1) * KB_STATES, SCAN_CW):
            rc = _re_cols(n0)
            ic = rc + KB_STATES
            ar = jnp.broadcast_to(are_ref[:, n0:n0 + SCAN_CW], (SCAN_SUB, SCAN_CW))
            ai = jnp.broadcast_to(aim_ref[:, n0:n0 + SCAN_CW], (SCAN_SUB, SCAN_CW))
            xr = jnp.zeros((SCAN_SUB, SCAN_CW), F32)
            xi = jnp.zeros((SCAN_SUB, SCAN_CW), F32)
            for i in range(SCAN_STEPS):
                r0 = i * SCAN_SUB
                xr, xi = (ar * xr - ai * xi + x_scr[r0:r0 + SCAN_SUB, rc:rc + SCAN_CW],
                          ar * xi + ai * xr + x_scr[r0:r0 + SCAN_SUB, ic:ic + SCAN_CW])
                x_scr[r0:r0 + SCAN_SUB, rc:rc + SCAN_CW] = xr
                x_scr[r0:r0 + SCAN_SUB, ic:ic + SCAN_CW] = xi

            a_t_r = apr_scr[SCAN_STEPS - 1:SCAN_STEPS, n0:n0 + SCAN_CW]
            a_t_i = api_scr[SCAN_STEPS - 1:SCAN_STEPS, n0:n0 + SCAN_CW]
            cr = str_scr[:, n0:n0 + SCAN_CW]
            ci = sti_scr[:, n0:n0 + SCAN_CW]
            car_r = jnp.zeros((SCAN_SUB, SCAN_CW), F32)
            car_i = jnp.zeros((SCAN_SUB, SCAN_CW), F32)
            for j in range(SCAN_SUB):
                car_r = jnp.where(sub == j, cr, car_r)
                car_i = jnp.where(sub == j, ci, car_i)
                cr, ci = (xr[j:j + 1] + a_t_r * cr - a_t_i * ci,
                          xi[j:j + 1] + a_t_r * ci + a_t_i * cr)
            str_scr[:, n0:n0 + SCAN_CW] = cr
            sti_scr[:, n0:n0 + SCAN_CW] = ci

            for i in range(SCAN_STEPS):
                r0 = i * SCAN_SUB
                pr = apr_scr[i:i + 1, n0:n0 + SCAN_CW]
                pi = api_scr[i:i + 1, n0:n0 + SCAN_CW]
                x_scr[r0:r0 + SCAN_SUB, rc:rc + SCAN_CW] = (x_scr[r0:r0 + SCAN_SUB, rc:rc + SCAN_CW]
                                                            + (pr * car_r - pi * car_i))
                x_scr[r0:r0 + SCAN_SUB, ic:ic + SCAN_CW] = (x_scr[r0:r0 + SCAN_SUB, ic:ic + SCAN_CW]
                                                            + (pr * car_i + pi * car_r))
        ys.append(jnp.dot(x_scr[:, kb * 2 * KB_STATES:(kb + 1) * 2 * KB_STATES].astype(BF16), cmat_ref[kb],
                          preferred_element_type=F32))
    yp = jnp.concatenate(ys, axis=1)
    hi = yp.astype(BF16)
    lo = (yp - hi.astype(F32)).astype(BF16)
    y_state = (jnp.dot(permt_ref[...], hi, preferred_element_type=F32)
               + jnp.dot(permt_ref[...], lo, preferred_element_type=F32))
    y_ref[...] = _s5_output(y_state, u, d_ref, wglu_ref)
    fre_ref[0] = str_scr[...]
    fim_ref[0] = sti_scr[...]


def _s5_scan(proj, mem_args, bsz, seq, perm, perm_t, bmat, a_re, a_im, cmat, d_row, w_glu):
    nchunk = seq // SCAN_T
    const2 = lambda b, c: (0, 0)
    const3 = lambda b, c: (0, 0, 0)
    state_spec = pl.BlockSpec((1, 1, N_STATE), lambda b, c: (b, 0, 0))
    return pl.pallas_call(
        _s5_scan_kernel,
        grid=(bsz, nchunk),
        in_specs=[pl.BlockSpec((SCAN_T, MIX_W), lambda b, c: (b * nchunk + c, 0)),
                  pl.BlockSpec((SCAN_T, SCAN_T), const2),
                  pl.BlockSpec((SCAN_T, SCAN_T), const2),
                  pl.BlockSpec((N_KB, SSM_KB, 2 * KB_STATES), const3),
                  pl.BlockSpec((1, N_STATE), const2),
                  pl.BlockSpec((1, N_STATE), const2),
                  pl.BlockSpec((N_KB, 2 * KB_STATES, SSM_KB), const3),
                  pl.BlockSpec((1, MIX_W), const2),
                  pl.BlockSpec((MIX_W, MIX_W), const2)] + _mem_specs(SCAN_T, nchunk, mem_args[0].shape[1]),
        out_specs=[pl.BlockSpec((SCAN_T, MIX_W), lambda b, c: (b * nchunk + c, 0)),
                   pl.BlockSpec((SCAN_T, MEM_W), lambda b, c: (b * nchunk + c, 0)), state_spec, state_spec],
        out_shape=[jax.ShapeDtypeStruct((bsz * seq, MIX_W), BF16),
                   jax.ShapeDtypeStruct((bsz * seq, MEM_W), BF16),
                   jax.ShapeDtypeStruct((bsz, 1, N_STATE), F32),
                   jax.ShapeDtypeStruct((bsz, 1, N_STATE), F32)],
        scratch_shapes=[pltpu.VMEM((SCAN_T, 2 * N_STATE), F32),
                        pltpu.VMEM((SCAN_STEPS, N_STATE), F32),
                        pltpu.VMEM((SCAN_STEPS, N_STATE), F32),
                        pltpu.VMEM((1, N_STATE), F32),
                        pltpu.VMEM((1, N_STATE), F32)],
        compiler_params=_cparams("parallel", "arbitrary"),
        name="s5_scan",
    )(proj, perm, perm_t, bmat, a_re, a_im, cmat, d_row, w_glu, proj, *mem_args)


def _s5_step_kernel(u_ref, hre_ref, him_ref, bmat_ref, are_ref, aim_ref, cmat_ref, d_ref, wglu_ref,
                    y_ref, xre_ref, xim_ref):
    u = u_ref[...]
    ub = u.astype(BF16)
    ys = []
    for kb in range(N_KB):
        bu = jnp.dot(ub[:, kb * SSM_KB:(kb + 1) * SSM_KB], bmat_ref[kb], preferred_element_type=F32)
        n0 = kb * KB_STATES
        ar, ai = are_ref[:, n0:n0 + KB_STATES], aim_ref[:, n0:n0 + KB_STATES]
        hr, hi = hre_ref[:, n0:n0 + KB_STATES], him_ref[:, n0:n0 + KB_STATES]
        xr = ar * hr - ai * hi + bu[:, :KB_STATES]
        xi = ar * hi + ai * hr + bu[:, KB_STATES:]
        xre_ref[:, n0:n0 + KB_STATES] = xr
        xim_ref[:, n0:n0 + KB_STATES] = xi
        x = jnp.concatenate([xr, xi], axis=1).astype(BF16)
        ys.append(jnp.dot(x, cmat_ref[kb], preferred_element_type=F32))
    y_ref[...] = _s5_output(jnp.concatenate(ys, axis=1), u, d_ref, wglu_ref)


def _s5_step(proj, h_re, h_im, bmat, a_re, a_im, cmat, d_row, w_glu):
    n = proj.shape[0]
    const2 = lambda i: (0, 0)
    const3 = lambda i: (0, 0, 0)
    return pl.pallas_call(
        _s5_step_kernel,
        grid=(1,),
        in_specs=[pl.BlockSpec((n, MIX_W), const2),
                  pl.BlockSpec((n, N_STATE), const2),
                  pl.BlockSpec((n, N_STATE), const2),
                  pl.BlockSpec((N_KB, SSM_KB, 2 * KB_STATES), const3),
                  pl.BlockSpec((1, N_STATE), const2),
                  pl.BlockSpec((1, N_STATE), const2),
                  pl.BlockSpec((N_KB, 2 * KB_STATES, SSM_KB), const3),
                  pl.BlockSpec((1, MIX_W), const2),
                  pl.BlockSpec((MIX_W, MIX_W), const2)],
        out_specs=[pl.BlockSpec((n, MIX_W), const2),
                   pl.BlockSpec((n, N_STATE), const2),
                   pl.BlockSpec((n, N_STATE), const2)],
        out_shape=[jax.ShapeDtypeStruct((n, MIX_W), BF16),
                   jax.ShapeDtypeStruct((n, N_STATE), F32),
                   jax.ShapeDtypeStruct((n, N_STATE), F32)],
        compiler_params=_cparams("arbitrary"),
        name="s5_step",
    )(proj, h_re, h_im, bmat, a_re, a_im, cmat, d_row, w_glu)


def _mem_attend(q, k, v, seg_mat, g_row):
    q = _seg_rms(q, seg_mat, g_row) * ATTN_SCALE
    lane_head = lax.broadcasted_iota(jnp.int32, q.shape, 1) // HEAD_DIM
    out = jnp.zeros(q.shape, F32)
    for h in range(MEM_HEADS):
        qh = jnp.where(lane_head == h, q, 0.0).astype(BF16)
        s = _dot_nt(qh, k)
        p = jnp.exp(s - jnp.max(s, axis=-1, keepdims=True))
        o = jnp.dot(p.astype(BF16), v, preferred_element_type=F32) / jnp.sum(p, axis=-1, keepdims=True)
        out = jnp.where(lane_head == h, o, out)
    return out.astype(BF16)


def _mem_specs(rows, blocks_per_seq, tokens):
    const2 = lambda b, i: (0, 0)
    kv_spec = pl.BlockSpec((1, tokens, MEM_W), lambda b, i: (b, 0, 0))
    return [pl.BlockSpec((rows, MEM_W), lambda b, i: (b * blocks_per_seq + i, MIX_W // MEM_W)),
            kv_spec, kv_spec, pl.BlockSpec((MEM_W, MEM_W), const2), pl.BlockSpec((1, MEM_W), const2)]


def _decode_prep_kernel(proj_ref, segk_ref, gmq_ref, *rest, with_moba):
    x = proj_ref[...]
    if with_moba:
        segm_ref, gq_ref, knew_ref, vnew_ref, mqt_ref, qt_ref, knt_ref, vnt_ref = rest
        qt_ref[...] = _seg_rms(x[:, :MIX_W], segm_ref[...], gq_ref[...]).T
        knt_ref[...] = knew_ref[...].T
        vnt_ref[...] = vnew_ref[...].T
    else:
        mqt_ref, = rest
    mqt_ref[...] = (_seg_rms(x[:, MIX_W:], segk_ref[...], gmq_ref[...]) * ATTN_SCALE).T


def _decode_prep(proj, seg_kv, gmq_row, moba=None):
    n, d = proj.shape
    const2 = lambda i: (0, 0)
    args = [proj, seg_kv, gmq_row]
    in_specs = [pl.BlockSpec((n, d), const2), pl.BlockSpec((MEM_W, MEM_W), const2), pl.BlockSpec((1, MEM_W), const2)]
    out_shape = [jax.ShapeDtypeStruct((MEM_W, n), F32)]
    if moba is not None:
        args += list(moba)
        in_specs += [pl.BlockSpec((MIX_W, MIX_W), const2), pl.BlockSpec((1, MIX_W), const2),
                     pl.BlockSpec((n, KV_W), const2), pl.BlockSpec((n, KV_W), const2)]
        out_shape += [jax.ShapeDtypeStruct((MIX_W, n), F32), jax.ShapeDtypeStruct((KV_W, n), F32),
                      jax.ShapeDtypeStruct((KV_W, n), F32)]
    return pl.pallas_call(
        functools.partial(_decode_prep_kernel, with_moba=moba is not None),
        grid=(1,),
        in_specs=in_specs,
        out_specs=[pl.BlockSpec(s.shape, const2) for s in out_shape],
        out_shape=out_shape,
        compiler_params=_cparams("arbitrary"),
        name="decode_prep",
    )(*args)


def _mem_decode_kernel(qt_ref, kt_ref, vt_ref, ot_ref, s_scr, p_scr, *, rows):
    i = pl.program_id(0)
    tokens = kt_ref.shape[-1]
    lanes = 128
    nchunk = tokens // lanes

    @pl.when(i == 0)
    def _():
        ot_ref[...] = jnp.zeros_like(ot_ref)

    for r in range(rows):
        qrep = _column_lanes(qt_ref, i * rows + r, lanes)
        for h in range(MEM_HEADS):
            qh = qrep[h * HEAD_DIM:(h + 1) * HEAD_DIM]
            for c in range(nchunk):
                s_scr[r * MEM_HEADS + h:r * MEM_HEADS + h + 1, c * lanes:(c + 1) * lanes] = jnp.sum(
                    qh * kt_ref[r, h, :, c * lanes:(c + 1) * lanes], axis=0, keepdims=True)
    s = s_scr[...]
    pr = jnp.exp(s - jnp.max(s, axis=1, keepdims=True))
    p_scr[...] = pr / jnp.sum(pr, axis=1, keepdims=True)
    for r in range(rows):
        accs = [p_scr[r * MEM_HEADS + h:r * MEM_HEADS + h + 1, :] * vt_ref[r, h] for h in range(MEM_HEADS)]
        _add_lane_sum_column(ot_ref, jnp.concatenate(accs, axis=0), i * rows + r)


def _mem_decode(mq_t, mem_kt, mem_vt, layer, rows=8):
    n = mq_t.shape[1]
    tokens = mem_kt.shape[-1]
    kv_spec = pl.BlockSpec((None, rows, MEM_HEADS, HEAD_DIM, tokens), lambda i: (layer, i, 0, 0, 0))
    return pl.pallas_call(
        functools.partial(_mem_decode_kernel, rows=rows),
        grid=(n // rows,),
        in_specs=[pl.BlockSpec((MEM_W, n), lambda i: (0, 0)), kv_spec, kv_spec],
        out_specs=pl.BlockSpec((MEM_W, n), lambda i: (0, 0)),
        out_shape=jax.ShapeDtypeStruct((MEM_W, n), F32),
        scratch_shapes=[pltpu.VMEM((rows * MEM_HEADS, tokens), F32), pltpu.VMEM((rows * MEM_HEADS, tokens), F32)],
        compiler_params=_cparams("arbitrary"),
        name="mem_decode",
    )(mq_t, mem_kt, mem_vt)


def _moba_prefill_kernel(q_ref, k_ref, vt_ref, kmean_ref, seg_ref, g_ref,
                         mq_ref, mk_ref, mv_ref, mseg_ref, mg_ref, o_ref, mem_ref,
                         qa_scr, s_scr, mcur_scr, alpha_scr, m_scr, acc_scr, *, nblk):
    mem_ref[...] = _mem_attend(mq_ref[...], mk_ref[0], mv_ref[0], mseg_ref[...], mg_ref[...])
    i = pl.program_id(1)
    blk = MOBA_BLOCK
    qn_t = _seg_rms(q_ref[...], seg_ref[...], g_ref[...]).T
    qs_t = (qn_t * (ATTN_SCALE * LOG2_E)).astype(BF16)
    kmean = kmean_ref[0]
    cols = Q_PER_KV * blk
    key_idx = lax.broadcasted_iota(jnp.int32, (blk, cols), 0)
    row_idx = lax.broadcasted_iota(jnp.int32, (blk, cols), 1) % blk
    causal = key_idx <= row_idx
    pad_t = jnp.zeros((KAUG_W - HEAD_DIM - nblk, blk), BF16)

    def start_of(n):
        return pl.multiple_of(jnp.where(n == 0, i, n - 1) * blk, blk)

    def score_phase(n, slot, g, diagonal=False):
        s = jnp.dot(k_ref[g, pl.ds(start_of(n), blk), :], qa_scr[g], preferred_element_type=F32)
        if diagonal:
            s = jnp.where(causal, s, MASK_VALUE)
        s_scr[slot, g] = s
        m_old = m_scr[g]
        m_new = jnp.maximum(m_old, jnp.max(s, axis=0, keepdims=True))
        alpha_scr[slot, g] = jnp.exp2(m_old - m_new)
        mcur_scr[slot, g] = m_new
        m_scr[g] = m_new

    def value_phase(n, slot, g):
        p = jnp.exp2(s_scr[slot, g] - mcur_scr[slot, g]).astype(BF16)
        acc_scr[g] = alpha_scr[slot, g] * acc_scr[g] + jnp.dot(
            vt_ref[g * VAUG_H:(g + 1) * VAUG_H, pl.ds(start_of(n), blk)], p,
            preferred_element_type=F32)

    m_scr[...] = jnp.full(m_scr.shape, MASK_VALUE, F32)
    acc_scr[...] = jnp.zeros_like(acc_scr)
    for g in range(MOBA_KV_HEADS):
        km = kmean[:, g * HEAD_DIM:(g + 1) * HEAD_DIM]
        parts = []
        for hh in range(Q_PER_KV):
            h = g * Q_PER_KV + hh
            gate_t = jnp.dot(km, qn_t[h * HEAD_DIM:(h + 1) * HEAD_DIM], precision=lax.Precision.HIGHEST,
                             preferred_element_type=F32)
            keep_t = _topk_keep(gate_t, i, axis=0)
            past_t = lax.broadcasted_iota(jnp.int32, gate_t.shape, 0) < i
            bias_t = jnp.where(past_t & ~keep_t, MASK_VALUE, 0.0).astype(BF16)
            parts.append(jnp.concatenate([qs_t[h * HEAD_DIM:(h + 1) * HEAD_DIM], bias_t, pad_t], axis=0))
        qa_scr[g] = jnp.concatenate(parts, axis=1)
        score_phase(0, 0, g, diagonal=True)

    def pair_step(t, _):
        for g in range(MOBA_KV_HEADS):
            value_phase(2 * t, 0, g)
            score_phase(2 * t + 1, 1, g)
        for g in range(MOBA_KV_HEADS):
            value_phase(2 * t + 1, 1, g)
            score_phase(2 * t + 2, 0, g)
        return 0

    lax.fori_loop(0, i // 2, pair_step, 0)

    @pl.when(i % 2 == 1)
    def _():
        for g in range(MOBA_KV_HEADS):
            value_phase(i - 1, 0, g)
            score_phase(i, 1, g)
        for g in range(MOBA_KV_HEADS):
            value_phase(i, 1, g)

    @pl.when(i % 2 == 0)
    def _():
        for g in range(MOBA_KV_HEADS):
            value_phase(i, 0, g)

    outs = []
    for g in range(MOBA_KV_HEADS):
        out_t = acc_scr[g, :HEAD_DIM] / acc_scr[g, HEAD_DIM:HEAD_DIM + 1]
        outs += [out_t[:, hh * blk:(hh + 1) * blk] for hh in range(Q_PER_KV)]
    o_ref[...] = jnp.concatenate(outs, axis=0).T.astype(BF16)


def _moba_prefill(proj, mem_args, k_aug, v_t, kmean, seg_mat, g_row, bsz, seq):
    nblk = seq // MOBA_BLOCK
    cols = Q_PER_KV * MOBA_BLOCK
    return pl.pallas_call(
        functools.partial(_moba_prefill_kernel, nblk=nblk),
        grid=(bsz, nblk),
        in_specs=[pl.BlockSpec((MOBA_BLOCK, MIX_W), lambda b, i: (b * nblk + i, 0)),
                  pl.BlockSpec((MOBA_KV_HEADS, seq, KAUG_W), lambda b, i: (0, b, 0)),
                  pl.BlockSpec((MOBA_KV_HEADS * VAUG_H, seq), lambda b, i: (0, b)),
                  pl.BlockSpec((1, nblk, KV_W), lambda b, i: (b, 0, 0)),
                  pl.BlockSpec((MIX_W, MIX_W), lambda b, i: (0, 0)),
                  pl.BlockSpec((1, MIX_W), lambda b, i: (0, 0))] + _mem_specs(MOBA_BLOCK, nblk, mem_args[0].shape[1]),
        out_specs=[pl.BlockSpec((MOBA_BLOCK, MIX_W), lambda b, i: (b * nblk + i, 0)),
                   pl.BlockSpec((MOBA_BLOCK, MEM_W), lambda b, i: (b * nblk + i, 0))],
        out_shape=[jax.ShapeDtypeStruct((bsz * seq, MIX_W), BF16),
                   jax.ShapeDtypeStruct((bsz * seq, MEM_W), BF16)],
        scratch_shapes=[pltpu.VMEM((MOBA_KV_HEADS, KAUG_W, cols), BF16),
                        pltpu.VMEM((2, MOBA_KV_HEADS, MOBA_BLOCK, cols), F32),
                        pltpu.VMEM((2, MOBA_KV_HEADS, 1, cols), F32),
                        pltpu.VMEM((2, MOBA_KV_HEADS, 1, cols), F32),
                        pltpu.VMEM((MOBA_KV_HEADS, 1, cols), F32),
                        pltpu.VMEM((MOBA_KV_HEADS, VAUG_H, cols), F32)],
        compiler_params=_cparams("parallel", "arbitrary"),
        name="moba_prefill",
    )(proj, k_aug, v_t, kmean, seg_mat, g_row, proj, *mem_args)


def _moba_decode_kernel(pt_ref, qt_ref, knt_ref, vnt_ref, ck_hbm, cv_hbm, ot_ref,
                        kbuf, vbuf, sem, s_scr, p_scr, own_scr, *, n_pages, page):
    b = pl.program_id(0)
    nblk = n_pages * page // MOBA_BLOCK
    slot = b % 2

    def page_copies(row, into):
        copies = []
        for p in range(n_pages):
            src = pt_ref[row * n_pages + p]
            copies.append(pltpu.make_async_copy(ck_hbm.at[src], kbuf.at[into, p], sem.at[0, into]))
            copies.append(pltpu.make_async_copy(cv_hbm.at[src], vbuf.at[into, p], sem.at[1, into]))
        return copies

    @pl.when(b == 0)
    def _():
        ot_ref[...] = jnp.zeros_like(ot_ref)
        for c in page_copies(0, 0):
            c.start()

    @pl.when(b + 1 < pl.num_programs(0))
    def _():
        for c in page_copies(b + 1, 1 - slot):
            c.start()

    for c in page_copies(b, slot):
        c.wait()
    k_pages = [kbuf.at[slot, p] for p in range(n_pages)]
    v_pages = [vbuf.at[slot, p] for p in range(n_pages)]

    qrep = _column_lanes(qt_ref, b, page)
    knrep = _column_lanes(knt_ref, b, page)
    vnrep = _column_lanes(vnt_ref, b, page)

    s_scr[MOBA_HEADS:, :] = jnp.zeros((s_scr.shape[0] - MOBA_HEADS, s_scr.shape[1]), F32)
    own_scr[MOBA_HEADS:, :] = jnp.zeros((own_scr.shape[0] - MOBA_HEADS, page), F32)
    for h in range(MOBA_HEADS):
        g = h // Q_PER_KV
        qh = qrep[h * HEAD_DIM:(h + 1) * HEAD_DIM]
        for p in range(n_pages):
            s_scr[h:h + 1, p * page:(p + 1) * page] = jnp.sum(qh * k_pages[p][g], axis=0, keepdims=True)
        own_scr[h:h + 1, :] = jnp.sum(qh * knrep[g * HEAD_DIM:(g + 1) * HEAD_DIM], axis=0, keepdims=True)
    s = s_scr[...]

    gates = [jnp.sum(s[:, j * MOBA_BLOCK:(j + 1) * MOBA_BLOCK], axis=1, keepdims=True) for j in range(nblk)]
    parts = []
    for j in range(nblk):
        rank = jnp.zeros(gates[j].shape, jnp.int32)
        for jj in range(nblk):
            if jj != j:
                beats = (gates[jj] >= gates[j]) if jj < j else (gates[jj] > gates[j])
                rank = rank + jnp.where(beats, 1, 0)
        bias = jnp.where(rank < MOBA_TOPK, 0.0, MASK_VALUE)
        parts.append(s[:, j * MOBA_BLOCK:(j + 1) * MOBA_BLOCK] * ATTN_SCALE + bias)
    s = jnp.concatenate(parts, axis=1)
    s_own = own_scr[:, :1] * ATTN_SCALE
    m = jnp.maximum(jnp.max(s, axis=1, keepdims=True), s_own)
    pr = jnp.exp(s - m)
    p_own = jnp.exp(s_own - m)
    inv = 1.0 / (jnp.sum(pr, axis=1, keepdims=True) + p_own)
    p_scr[...] = pr * inv
    own_scr[...] = jnp.broadcast_to(p_own * inv * (1.0 / page), own_scr.shape)

    accs = []
    for h in range(MOBA_HEADS):
        g = h // Q_PER_KV
        acc = own_scr[h:h + 1, :] * vnrep[g * HEAD_DIM:(g + 1) * HEAD_DIM]
        for p in range(n_pages):
            acc = acc + p_scr[h:h + 1, p * page:(p + 1) * page] * v_pages[p][g]
        accs.append(acc)
    _add_lane_sum_column(ot_ref, jnp.concatenate(accs, axis=0), b)


def _column_lanes(xt_ref, col, lanes):
    rows, n = xt_ref.shape
    at_lane0 = pltpu.roll(xt_ref[...], lax.rem(n - col, n), axis=1)
    return jnp.broadcast_to(at_lane0[:, :1], (rows, lanes))


def _add_lane_sum_column(ot_ref, acc, col):
    place = (lax.broadcasted_iota(jnp.int32, (acc.shape[1], ot_ref.shape[1]), 1) == col).astype(BF16)
    hi = acc.astype(BF16)
    lo = (acc - hi.astype(F32)).astype(BF16)
    ot_ref[...] += (jnp.dot(hi, place, preferred_element_type=F32)
                    + jnp.dot(lo, place, preferred_element_type=F32))


def _moba_decode(q_t, knew_t, vnew_t, cache_kt, cache_vt, page_table):
    n = q_t.shape[1]
    page = cache_kt.shape[3]
    n_pages = page_table.shape[0] // n
    past = n_pages * page
    hrows = -(-MOBA_HEADS // 8) * 8

    const2 = lambda b, pt: (0, 0)
    page_buf = pltpu.VMEM((2, n_pages, MOBA_KV_HEADS, HEAD_DIM, page), F32)
    grid_spec = pltpu.PrefetchScalarGridSpec(
        num_scalar_prefetch=1,
        grid=(n,),
        in_specs=[pl.BlockSpec((MIX_W, n), const2),
                  pl.BlockSpec((KV_W, n), const2),
                  pl.BlockSpec((KV_W, n), const2),
                  pl.BlockSpec(memory_space=pl.ANY),
                  pl.BlockSpec(memory_space=pl.ANY)],
        out_specs=pl.BlockSpec((MIX_W, n), const2),
        scratch_shapes=[page_buf, page_buf, pltpu.SemaphoreType.DMA((2, 2)),
                        pltpu.VMEM((hrows, past), F32), pltpu.VMEM((hrows, past), F32),
                        pltpu.VMEM((hrows, page), F32)],
    )
    return pl.pallas_call(
        functools.partial(_moba_decode_kernel, n_pages=n_pages, page=page),
        grid_spec=grid_spec,
        out_shape=jax.ShapeDtypeStruct((MIX_W, n), F32),
        compiler_params=_cparams("arbitrary"),
        name="moba_decode",
    )(page_table, q_t, knew_t, vnew_t, cache_kt, cache_vt)


def _seg_matrix(width):
    head = jnp.arange(width) // HEAD_DIM
    return (head[:, None] == head[None, :]).astype(BF16) / HEAD_DIM


def _scan_perm():
    r = jnp.arange(SCAN_T)
    t = (r % SCAN_SUB) * SCAN_STEPS + r // SCAN_SUB
    return (t[:, None] == jnp.arange(SCAN_T)[None, :]).astype(BF16)


def _block_diag(x, row_axes):
    eye = jnp.eye(x.shape[1], dtype=x.dtype)
    if row_axes == "gh":
        full = jnp.einsum("kghp,gG->kghGp", x, eye)
    else:
        full = jnp.einsum("kghp,gG->kgpGh", x, eye)
    k, g, a, _, b = full.shape
    return full.reshape(k, g * a, g * b)


def _s5_constants(p, l, ab_re, ab_im, bb_re, bb_im):
    gpb = SSM_KB // SSM_GROUP
    shape4 = (N_KB, gpb, SSM_GROUP, SSM_STATE)
    bmat = jnp.concatenate([_block_diag(bb_re[l].reshape(shape4), "gh"),
                            _block_diag(bb_im[l].reshape(shape4), "gh")], axis=-1).astype(BF16)
    c_re = p["ssm_c_re"][l].reshape(shape4)
    c_im = p["ssm_c_im"][l].reshape(shape4)
    cmat = jnp.concatenate([_block_diag(c_re, "gp"), -_block_diag(c_im, "gp")], axis=1).astype(BF16)
    a_re = ab_re[l, ::SSM_GROUP].reshape(1, N_STATE)
    a_im = ab_im[l, ::SSM_GROUP].reshape(1, N_STATE)
    d_row = p["ssm_d"][l].reshape(1, MIX_W)
    return bmat, a_re, a_im, cmat, d_row, p["w_glu"][l].astype(BF16)


def kernel(x_prompt, x_sample, mem_prompt, state_ssm_re, state_ssm_im, cache_k, cache_v, page_table,
           cache_mem_k, cache_mem_v, g_mix, w_in, w_out, g_ffn, w_gu, w_down, ssm_a_re, ssm_a_im,
           ssm_log_dt, ssm_b_re, ssm_b_im, ssm_c_re, ssm_c_im, ssm_d, w_glu, g_q, g_mq, g_mem,
           w_mem_kv, g_mk, g_kv, w_kv, g_k):
    bsz, seq, _ = x_prompt.shape
    n_p = bsz * seq
    n_s = x_sample.shape[0]
    tokens = mem_prompt.shape[1]
    p = dict(ssm_c_re=ssm_c_re, ssm_c_im=ssm_c_im, ssm_d=ssm_d, w_glu=w_glu)

    w_in_b = w_in.astype(BF16)
    w_out_b = w_out.astype(BF16)
    w_gu_b = w_gu.astype(BF16)
    w_down_b = w_down.astype(BF16)
    seg_kv = _seg_matrix(KV_W)
    seg_mix = _seg_matrix(MIX_W)
    gq_rows = jnp.tile(g_q, (1, MOBA_HEADS))
    gmq_rows = jnp.tile(g_mq, (1, MEM_HEADS))

    rep = lambda a: jnp.repeat(a, SSM_GROUP, axis=1)
    b_rows = lambda b: jnp.swapaxes(b, 2, 3).reshape(N_A_LAYERS, MIX_W, SSM_STATE)
    ab_re, ab_im, bb_re, bb_im = _ssm_prep(rep(ssm_a_re), rep(ssm_a_im), rep(ssm_log_dt)[..., None],
                                           b_rows(ssm_b_re), b_rows(ssm_b_im))
    s5 = [_s5_constants(p, l, ab_re, ab_im, bb_re, bb_im) for l in range(N_A_LAYERS)]
    perm = _scan_perm()
    perm_t = perm.T

    mem_flat = mem_prompt.reshape(bsz * tokens, D_MODEL)
    pm_k, pm_v = _kv_proj(mem_flat, g_mem[:, None, :], w_mem_kv.astype(BF16), seg_kv,
                          jnp.tile(g_mk, (1, MEM_HEADS))[:, None, :], tm=256)
    p_mem_k = pm_k.reshape(DEPTH, bsz, tokens, MEM_HEADS, HEAD_DIM)
    p_mem_v = pm_v.reshape(DEPTH, bsz, tokens, MEM_HEADS, HEAD_DIM)
    pm_k_b = pm_k.reshape(DEPTH, bsz, tokens, MEM_W).astype(BF16)
    pm_v_b = pm_v.reshape(DEPTH, bsz, tokens, MEM_W).astype(BF16)

    g_kv_rows = g_kv.reshape(1, 1, D_MODEL)
    w_kv_b = w_kv.astype(BF16)[None]
    g_k_rows = jnp.tile(g_k, MOBA_KV_HEADS).reshape(1, 1, KV_W)

    h = x_prompt.reshape(n_p, D_MODEL)
    p_fin_re, p_fin_im = [], []
    k_aug = v_t = kmean = p_k = p_v = None
    nblk = seq // MOBA_BLOCK
    g_mix_rows = g_mix[:, None, :]
    next_in = lambda l: (g_mix_rows, w_in_b) if l + 1 < DEPTH else None
    tail_weights = (w_out_b, g_ffn[:, None, :], w_gu_b, w_down_b)
    proj = _norm_matmul(h, g_mix_rows, w_in_b, 0, tm=512)
    for l in range(DEPTH):
        mem_args = (pm_k_b[l], pm_v_b[l], seg_kv, gmq_rows[l][None])
        if l < N_A_LAYERS:
            mix, mem, f_re, f_im = _s5_scan(proj, mem_args, bsz, seq, perm, perm_t, *s5[l])
            p_fin_re.append(f_re.reshape(bsz, SSM_GROUPS, SSM_STATE))
            p_fin_im.append(f_im.reshape(bsz, SSM_GROUPS, SSM_STATE))
        else:
            mix, mem = _moba_prefill(proj, mem_args, k_aug, v_t, kmean, seg_mix, gq_rows[l - N_A_LAYERS][None],
                                     bsz, seq)
        h, proj = _layer_tail(h, mix, mem, l, *tail_weights, tm=512, next_in=next_in(l))
        if l == N_A_LAYERS - 1:
            k_new, v_new, k_aug, v_t, kmean = _kv_proj(h, g_kv_rows, w_kv_b, seg_kv, g_k_rows, tm=MOBA_BLOCK,
                                                       blocks_per_seq=nblk)
            p_k = k_new.reshape(bsz, MOBA_KV_HEADS, HEAD_DIM, seq).transpose(0, 3, 1, 2)
            p_v = v_new.reshape(bsz, MOBA_KV_HEADS, HEAD_DIM, seq).transpose(0, 3, 1, 2)
            kmean = kmean.reshape(bsz, nblk, KV_W)
    y_prompt = h.reshape(bsz, seq, D_MODEL)

    ck_t = cache_k.transpose(0, 2, 3, 1)
    cv_t = cache_v.transpose(0, 2, 3, 1)
    cmk_t = cache_mem_k.transpose(0, 1, 3, 4, 2)
    cmv_t = cache_mem_v.transpose(0, 1, 3, 4, 2)

    h = x_sample.reshape(n_s, D_MODEL)
    s_fin_re, s_fin_im = [], []
    s_k = s_v = None
    proj = _norm_matmul(h, g_mix_rows, w_in_b, 0, tm=n_s)
    for l in range(DEPTH):
        if l < N_A_LAYERS:
            mix, x_re, x_im = _s5_step(proj, state_ssm_re[l].reshape(n_s, N_STATE),
                                       state_ssm_im[l].reshape(n_s, N_STATE), *s5[l])
            s_fin_re.append(x_re.reshape(n_s, SSM_GROUPS, SSM_STATE))
            s_fin_im.append(x_im.reshape(n_s, SSM_GROUPS, SSM_STATE))
            mq_t, = _decode_prep(proj, seg_kv, gmq_rows[l][None])
        else:
            mq_t, q_t, knew_t, vnew_t = _decode_prep(
                proj, seg_kv, gmq_rows[l][None],
                moba=(seg_mix, gq_rows[l - N_A_LAYERS][None], s_k.reshape(n_s, KV_W), s_v.reshape(n_s, KV_W)))
            mix = _moba_decode(q_t, knew_t, vnew_t, ck_t, cv_t, page_table.reshape(-1)).T.astype(BF16)
        mem = _mem_decode(mq_t, cmk_t, cmv_t, l).T.astype(BF16)
        h, proj = _layer_tail(h, mix, mem, l, *tail_weights, tm=n_s, next_in=next_in(l))
        if l == N_A_LAYERS - 1:
            s_k, s_v = _kv_proj(h, g_kv_rows, w_kv_b, seg_kv, g_k_rows, tm=n_s)
    y_sample = h.reshape(n_s, 1, D_MODEL)

    return (y_prompt, y_sample,
            jnp.stack(p_fin_re), jnp.stack(p_fin_im), p_k, p_v, p_mem_k, p_mem_v,
            jnp.stack(s_fin_re), jnp.stack(s_fin_im),
            s_k.reshape(n_s, 1, MOBA_KV_HEADS, HEAD_DIM), s_v.reshape(n_s, 1, MOBA_KV_HEADS, HEAD_DIM))
```
